```python
import math
import jax, jax.numpy as jnp
from jax import lax
import numpy as np

D_MODEL = 1024
BATCH = 4
SEQ = 4096
DEPTH = 4

GRID_W = 64
CTX_LEN = 256
MIX_WIDTH = D_MODEL
S5_WIDTH = MIX_WIDTH // 2
S5_GROUP = 16
S5_GROUPS = S5_WIDTH // S5_GROUP
S5_STATE = 64
HG_WIDTH = MIX_WIDTH - S5_WIDTH
HG_HEAD_DIM = 128
HG_HEADS = HG_WIDTH // HG_HEAD_DIM
HG_CHUNK = 32
D_FF = 11 * D_MODEL // 4
CONV_W = 3
DT_MIN = 1e-3
DT_MAX = 1e-1
ALPHA = (2 * DEPTH) ** 0.25
BETA = (8 * DEPTH) ** -0.25
LN_EPS = 1e-5
RMS_EPS = 1e-6
U_END = S5_WIDTH
FF_END = U_END + HG_WIDTH
FB_END = FF_END + HG_WIDTH
I_END = FB_END + HG_WIDTH
Q_END = I_END + HG_WIDTH
IN_COLS = Q_END + HG_WIDTH

kernel_name = "hybrid_s5_hgrn2_deepnorm_dit"


def layer_norm(x, g, b):
    xf = x.astype(jnp.float32)
    mu = jnp.mean(xf, axis=-1, keepdims=True)
    var = jnp.mean(jnp.square(xf - mu), axis=-1, keepdims=True)
    return ((xf - mu) * lax.rsqrt(var + LN_EPS) * g + b).astype(x.dtype)


def rms_norm(x, g):
    xf = x.astype(jnp.float32)
    return xf * lax.rsqrt(jnp.mean(jnp.square(xf), axis=-1, keepdims=True) + RMS_EPS) * g


def modulate(h, shift, scale):
    return h * (1 + scale) + shift


def cmul(ar, ai, br, bi):
    return ar * br - ai * bi, ar * bi + ai * br


def s5_discretise(lam_re, lam_im, log_dt, b_re, b_im):
    f32 = jnp.float32
    lr, li = lam_re.astype(f32), lam_im.astype(f32)
    dt = jnp.exp(log_dt.astype(f32))[:, None]
    mag, ang = jnp.exp(lr * dt), li * dt
    abar_re, abar_im = mag * jnp.cos(ang), mag * jnp.sin(ang)
    den = lr * lr + li * li
    nr, ni = abar_re - 1.0, abar_im
    coef_re = (nr * lr + ni * li) / den
    coef_im = (ni * lr - nr * li) / den
    bbar_re, bbar_im = cmul(coef_re[..., None], coef_im[..., None], b_re.astype(f32), b_im.astype(f32))
    return abar_re, abar_im, bbar_re, bbar_im


def ssm_combine(e1, e2):
    a1r, a1i, b1r, b1i = e1
    a2r, a2i, b2r, b2i = e2
    ar, ai = cmul(a2r, a2i, a1r, a1i)
    br, bi = cmul(a2r, a2i, b1r, b1i)
    return ar, ai, br + b2r, bi + b2i


def s5_scan(u, abar_re, abar_im, bbar_re, bbar_im, s0):
    bu_re = jnp.einsum('bngh,gph->bngp', u, bbar_re)
    bu_im = jnp.einsum('bngh,gph->bngp', u, bbar_im)
    if s0 is not None:
        init_re, init_im = cmul(abar_re, abar_im, s0[0], s0[1])
        bu_re = bu_re.at[:, 0].add(init_re)
        bu_im = bu_im.at[:, 0].add(init_im)
    a_re = jnp.broadcast_to(abar_re, bu_re.shape)
    a_im = jnp.broadcast_to(abar_im, bu_im.shape)
    _, _, x_re, x_im = lax.associative_scan(ssm_combine, (a_re, a_im, bu_re, bu_im), axis=1)
    return x_re, x_im


def gla_direction(k, v, logf, s0, q):
    bsz, n = k.shape[:2]
    nc = n // HG_CHUNK
    chunk = lambda t: t.reshape(bsz, nc, HG_CHUNK, HG_HEADS, HG_HEAD_DIM)
    k, v, logf = chunk(k), chunk(v), chunk(logf)
    bcum = jnp.cumsum(logf, axis=2)
    bend = bcum[:, :, -1:]
    kd = k * jnp.exp(bend - bcum)
    ds = jnp.einsum('bcshk,bcshv->cbhkv', kd, v)
    dec = jnp.exp(bend[:, :, 0]).transpose(1, 0, 2, 3)
    want_out = q is not None

    def step(s, inp):
        dec_c, ds_c = inp
        return dec_c[..., None] * s + ds_c, (s if want_out else None)

    s_fin, s_start = lax.scan(step, s0, (dec, ds))
    if not want_out:
        return None, s_fin
    q = chunk(q)
    o_inter = jnp.einsum('bclhk,cbhkv->bclhv', q * jnp.exp(bcum), s_start)
    att = jnp.einsum('bclhk,bcshk->bchls', q * jnp.exp(bcum - bend), kd)
    tril = jnp.tril(jnp.ones((HG_CHUNK, HG_CHUNK), dtype=bool))
    att = jnp.where(tril, att, 0.0)
    o_intra = jnp.einsum('bchls,bcshv->bclhv', att, v)
    return (o_inter + o_intra).reshape(bsz, n, HG_HEADS, HG_HEAD_DIM), s_fin


def token_mixer(h, w_in, s5p, hgp, init, with_out, with_states):
    lam_re, lam_im, log_dt, b_re, b_im, c_re, c_im, d_skip, w_glu, b_glu = s5p
    lb, norm_w = hgp
    bsz, n = h.shape[:2]
    cols = IN_COLS if with_out else I_END
    proj = (h @ w_in[:, :cols]).astype(jnp.float32)
    u = proj[..., :U_END]
    f_raws = (proj[..., U_END:FF_END], proj[..., FF_END:FB_END])
    heads = lambda t: t.reshape(bsz, n, HG_HEADS, HG_HEAD_DIM)
    v = heads(proj[..., FB_END:I_END])
    q = heads(jax.nn.silu(proj[..., I_END:Q_END])) if with_out else None

    ug = u.reshape(bsz, n, S5_GROUPS, S5_GROUP)
    s5_ys, s5_fin, hg_os, hg_fin = [], [], [], []
    for dr in range(2):
        flip = (lambda t: t[:, ::-1]) if dr == 1 else (lambda t: t)
        abar_re, abar_im, bbar_re, bbar_im = s5_discretise(lam_re[dr], lam_im[dr], log_dt[dr], b_re[dr], b_im[dr])
        x_re, x_im = s5_scan(flip(ug), abar_re, abar_im, bbar_re, bbar_im, None if init is None else init[0][dr])
        if with_states:
            s5_fin.append((x_re[:, -1], x_im[:, -1]))
        if with_out:
            y = jnp.einsum('bngp,ghp->bngh', x_re, c_re[dr]) - jnp.einsum('bngp,ghp->bngh', x_im, c_im[dr])
            s5_ys.append(flip(y).reshape(bsz, n, S5_WIDTH))
        f = lb[dr] + (1.0 - lb[dr]) * jax.nn.sigmoid(f_raws[dr])
        s0 = jnp.zeros((bsz, HG_HEADS, HG_HEAD_DIM, HG_HEAD_DIM), jnp.float32) if init is None else init[1][dr]
        o, s_fin = gla_direction(flip(heads(1.0 - f)), flip(v), flip(heads(jnp.log(f))), s0,
                                 flip(q) if with_out else None)
        if with_states:
            hg_fin.append(s_fin)
        if with_out:
            hg_os.append(flip(o))
    states = (s5_fin, hg_fin) if with_states else None
    if not with_out:
        return None, states
    s5_y = jax.nn.gelu(s5_ys[0] + s5_ys[1] + u * d_skip)
    s5_out = s5_y * jax.nn.sigmoid(s5_y @ w_glu + b_glu)
    hg_out = rms_norm(hg_os[0] + hg_os[1], norm_w).reshape(bsz, n, HG_WIDTH) * jax.nn.silu(proj[..., Q_END:IN_COLS])
    return jnp.concatenate([s5_out, hg_out], axis=-1), states


def dwconv(u, w, b):
    m = u.shape[-2]
    pad = CONV_W // 2
    up = jnp.pad(u, [(0, 0)] * (u.ndim - 2) + [(pad, pad), (0, 0)])
    out = up[..., 0:m, :] * w[0]
    for j in range(1, CONV_W):
        out = out + up[..., j:j + m, :] * w[j]
    return out + b


def conv_ffn(h, w_up, conv_w, conv_b, w_down, on_grid):
    bsz, n = h.shape[:2]
    up = h @ w_up
    if on_grid:
        rows = n // GRID_W
        up = up.reshape(bsz, rows, GRID_W, 2 * D_FF)
    up = dwconv(up, conv_w, conv_b).reshape(bsz, n, 2 * D_FF)
    a, g = jnp.split(up, 2, axis=-1)
    return (jax.nn.silu(a) * g) @ w_down


def setup_inputs(seed: int = 0) -> dict:
    key = jax.random.key(seed)
    ks = jax.random.split(key, 32)
    f32 = jnp.float32
    L, G, P, H = DEPTH, S5_GROUPS, S5_STATE, S5_GROUP
    nrm = lambda i, shape, scale: scale * jax.random.normal(ks[i], shape, f32)
    return {
        "x": nrm(0, (BATCH, SEQ, D_MODEL), 1.0),
        "c": nrm(1, (BATCH, D_MODEL), 1.0),
        "ctx": nrm(2, (BATCH, CTX_LEN, D_MODEL), 1.0),
        "c_ctx": nrm(3, (D_MODEL,), 1.0),
        "w_mod": nrm(4, (L, D_MODEL, 6 * D_MODEL), 0.5 * D_MODEL ** -0.5),
        "b_mod": nrm(5, (L, 6 * D_MODEL), 0.01),
        "w_in": nrm(6, (L, D_MODEL, IN_COLS), D_MODEL ** -0.5),
        "s5_lam_re": -0.5 + nrm(7, (L, 2, G, P), 0.01),
        "s5_lam_im": jnp.pi * jnp.arange(P, dtype=f32) + nrm(8, (L, 2, G, P), 0.01),
        "s5_log_dt": jax.random.uniform(ks[9], (L, 2, G), f32, math.log(DT_MIN), math.log(DT_MAX)),
        "s5_b_re": nrm(10, (L, 2, G, P, H), (2 * H) ** -0.5),
        "s5_b_im": nrm(11, (L, 2, G, P, H), (2 * H) ** -0.5),
        "s5_c_re": nrm(12, (L, 2, G, H, P), P ** -0.5),
        "s5_c_im": nrm(13, (L, 2, G, H, P), P ** -0.5),
        "s5_d": nrm(14, (L, S5_WIDTH), 1.0),
        "w_glu": nrm(15, (L, S5_WIDTH, S5_WIDTH), S5_WIDTH ** -0.5),
        "b_glu": nrm(16, (L, S5_WIDTH), 0.01),
        "hg_lb": nrm(17, (L, 2, HG_WIDTH), 0.1),
        "hg_norm_w": 1.0 + nrm(18, (L, HG_HEAD_DIM), 0.01),
        "w_out": nrm(19, (L, MIX_WIDTH, D_MODEL), BETA * MIX_WIDTH ** -0.5),
        "ln1_g": 1.0 + nrm(20, (L, D_MODEL), 0.01),
        "ln1_b": nrm(21, (L, D_MODEL), 0.01),
        "w_up": nrm(22, (L, D_MODEL, 2 * D_FF), D_MODEL ** -0.5),
        "conv_w": nrm(23, (L, CONV_W, 2 * D_FF), CONV_W ** -0.5),
        "conv_b": nrm(24, (L, 2 * D_FF), 0.01),
        "w_down": nrm(25, (L, D_FF, D_MODEL), BETA * D_FF ** -0.5),
        "ln2_g": 1.0 + nrm(26, (L, D_MODEL), 0.01),
        "ln2_b": nrm(27, (L, D_MODEL), 0.01),
    }


def reference(x, c, ctx, c_ctx, w_mod, b_mod, w_in, s5_lam_re, s5_lam_im, s5_log_dt, s5_b_re, s5_b_im,
              s5_c_re, s5_c_im, s5_d, w_glu, b_glu, hg_lb, hg_norm_w, w_out, ln1_g, ln1_b,
              w_up, conv_w, conv_b, w_down, ln2_g, ln2_b):
    lb_all = jnp.cumsum(jax.nn.softmax(hg_lb.astype(jnp.float32), axis=0), axis=0)
    lb_all = lb_all - lb_all[:1]
    silu_c = jax.nn.silu(c)
    silu_cc = jax.nn.silu(c_ctx)
    for l in range(DEPTH):
        last = l == DEPTH - 1
        mod_x = (silu_c @ w_mod[l] + b_mod[l])[:, None, :]
        sh1, sc1, g1, sh2, sc2, g2 = jnp.split(mod_x, 6, axis=-1)
        mc = jnp.split(silu_cc @ w_mod[l] + b_mod[l], 6, axis=-1)
        s5p = (s5_lam_re[l], s5_lam_im[l], s5_log_dt[l], s5_b_re[l], s5_b_im[l],
               s5_c_re[l], s5_c_im[l], s5_d[l], w_glu[l], b_glu[l])
        hgp = (lb_all[l], hg_norm_w[l])
        y_c, ctx_states = token_mixer(modulate(ctx, mc[0], mc[1]), w_in[l], s5p, hgp, None, not last, True)
        y_x, _ = token_mixer(modulate(x, sh1, sc1), w_in[l], s5p, hgp, ctx_states, True, False)
        x = layer_norm(ALPHA * x + g1 * (y_x @ w_out[l]), ln1_g[l], ln1_b[l])
        x = layer_norm(ALPHA * x + g2 * conv_ffn(modulate(x, sh2, sc2), w_up[l], conv_w[l], conv_b[l], w_down[l], True),
                       ln2_g[l], ln2_b[l])
        if not last:
            ctx = layer_norm(ALPHA * ctx + mc[2] * (y_c @ w_out[l]), ln1_g[l], ln1_b[l])
            ctx = layer_norm(ALPHA * ctx + mc[5] * conv_ffn(modulate(ctx, mc[3], mc[4]), w_up[l], conv_w[l],
                                                             conv_b[l], w_down[l], False),
                             ln2_g[l], ln2_b[l])
    return x
```

```python
import functools
import math

import jax
import jax.numpy as jnp
from jax import lax
from jax.experimental import pallas as pl
from jax.experimental.pallas import tpu as pltpu

F32 = jnp.float32
BF16 = jnp.bfloat16

GRID_W = 64
HG_CHUNK = 32
S5_CHUNK = 16
DT_MIN = 1e-3
DT_MAX = 1e-1
LN_EPS = 1e-5
RMS_EPS = 1e-6
LANES = 128
VMEM_LIMIT = 56 * 1024 * 1024


def _cparams(*sem):
    return pltpu.CompilerParams(dimension_semantics=sem, vmem_limit_bytes=VMEM_LIMIT)


def _silu(x):
    return x * jax.nn.sigmoid(x)


def _dot(a, b):
    return jnp.dot(a, b, preferred_element_type=F32)


def _dot_t0(a, b):
    return lax.dot_general(a, b, (((0,), (0,)), ((), ())), preferred_element_type=F32)


def _dot_t1(a, b):
    return lax.dot_general(a, b, (((1,), (1,)), ((), ())), preferred_element_type=F32)


def _split3(x):
    hi = x.astype(BF16)
    r1 = x - hi.astype(F32)
    mid = r1.astype(BF16)
    lo = (r1 - mid.astype(F32)).astype(BF16)
    return hi, mid, lo


def _layer_norm(z, g, b):
    mu = jnp.mean(z, axis=-1, keepdims=True)
    zc = z - mu
    var = jnp.mean(zc * zc, axis=-1, keepdims=True)
    return zc * lax.rsqrt(var + LN_EPS) * g + b


def _mod_kernel(c_ref, w_ref, b_ref, o_ref):
    s = _silu(c_ref[...]).astype(BF16)
    o_ref[0] = _dot(s, w_ref[0].astype(BF16)) + b_ref[0]


def _mod_vectors(rows, w_mod, b_mod):
    depth, d, d6 = w_mod.shape
    rb = rows.shape[0]
    tn = 1536 if d6 % 1536 == 0 else d6
    return pl.pallas_call(
        _mod_kernel,
        grid=(depth, d6 // tn),
        in_specs=[
            pl.BlockSpec((rb, d), lambda l, j: (0, 0)),
            pl.BlockSpec((1, d, tn), lambda l, j: (l, 0, j)),
            pl.BlockSpec((1, 1, tn), lambda l, j: (l, 0, j)),
        ],
        out_specs=pl.BlockSpec((1, rb, tn), lambda l, j: (l, 0, j)),
        out_shape=jax.ShapeDtypeStruct((depth, rb, d6), F32),
        compiler_params=_cparams("arbitrary", "arbitrary"),
        name="mod_vectors",
    )(rows, w_mod, b_mod.reshape(depth, 1, d6))


def _inproj_kernel(x_ref, sh_ref, sc_ref, w_ref, u_ref, f_ref, v_ref, q_ref, g_ref, *, w5, hg):
    h = (x_ref[0] * (1.0 + sc_ref[0]) + sh_ref[0]).astype(BF16)
    o = 0
    u_ref[0] = _dot(h, w_ref[:, o:o + w5]); o += w5
    f_ref[0] = _dot(h, w_ref[:, o:o + 2 * hg]); o += 2 * hg
    v_ref[0] = _dot(h, w_ref[:, o:o + hg]); o += hg
    q_ref[0] = _silu(_dot(h, w_ref[:, o:o + hg])); o += hg
    g_ref[0] = _silu(_dot(h, w_ref[:, o:o + hg]))


def _inproj(x, shift, scale, w_in, w5, hg):
    bsz, n, d = x.shape
    cols = w_in.shape[1]
    tb = min(n, 512)
    tok = lambda c: pl.BlockSpec((1, tb, c), lambda b, j: (b, j, 0))
    vec = pl.BlockSpec((1, 1, d), lambda b, j: (b, 0, 0))
    return pl.pallas_call(
        functools.partial(_inproj_kernel, w5=w5, hg=hg),
        grid=(bsz, n // tb),
        in_specs=[tok(d), vec, vec, pl.BlockSpec((d, cols), lambda b, j: (0, 0))],
        out_specs=[tok(w5), tok(2 * hg), tok(hg), tok(hg), tok(hg)],
        out_shape=[jax.ShapeDtypeStruct((bsz, n, c), F32) for c in (w5, 2 * hg, hg, hg, hg)],
        compiler_params=_cparams("arbitrary", "arbitrary"),
        name="inproj",
    )(x, shift, scale, w_in)


def _s5_state_in_kernel(u_ref, w_ref, s_ref):
    res = _dot(u_ref[0], w_ref[0])
    for k in range(4):
        s_ref[k, 0] = res[:, k * LANES:(k + 1) * LANES]


def _s5_state_in(u_flat, w_in_pair):
    npair, r, width = u_flat.shape
    tr = min(r, 256)
    return pl.pallas_call(
        _s5_state_in_kernel,
        grid=(npair, r // tr),
        in_specs=[
            pl.BlockSpec((1, tr, width), lambda p, i: (p, i, 0)),
            pl.BlockSpec((1, width, 4 * LANES), lambda p, i: (p, 0, 0)),
        ],
        out_specs=pl.BlockSpec((4, 1, tr, LANES), lambda p, i: (0, p, i, 0)),
        out_shape=jax.ShapeDtypeStruct((4, npair, r, LANES), F32),
        compiler_params=_cparams("arbitrary", "arbitrary"),
        name="s5_state_in",
    )(u_flat, w_in_pair)


def _s5_scan_kernel(s_ref, s0_ref, a_ref, xp_ref, xf_ref, *, nc):
    far, fai, bar, bai = a_ref[0], a_ref[1], a_ref[2], a_ref[3]

    def body(i, carry):
        fr, fi, br, bi = carry
        ib = nc - 1 - i
        xp_ref[0, 0, i] = fr
        xp_ref[1, 0, i] = fi
        xp_ref[2, 0, ib] = br
        xp_ref[3, 0, ib] = bi
        nfr = far * fr - fai * fi + s_ref[0, 0, i]
        nfi = far * fi + fai * fr + s_ref[1, 0, i]
        nbr = bar * br - bai * bi + s_ref[2, 0, ib]
        nbi = bar * bi + bai * br + s_ref[3, 0, ib]
        return nfr, nfi, nbr, nbi

    init = (s0_ref[0, 0], s0_ref[1, 0], s0_ref[2, 0], s0_ref[3, 0])
    fr, fi, br, bi = lax.fori_loop(0, nc, body, init)
    xf_ref[0, 0] = fr
    xf_ref[1, 0] = fi
    xf_ref[2, 0] = br
    xf_ref[3, 0] = bi


def _s5_scan(s_loc, s0, a_pow):
    _, bsz, nc, npair, _ = s_loc.shape
    return pl.pallas_call(
        functools.partial(_s5_scan_kernel, nc=nc),
        grid=(bsz,),
        in_specs=[
            pl.BlockSpec((4, 1, nc, npair, LANES), lambda b: (0, b, 0, 0, 0)),
            pl.BlockSpec((4, 1, npair, LANES), lambda b: (0, b, 0, 0)),
            pl.BlockSpec((4, npair, LANES), lambda b: (0, 0, 0)),
        ],
        out_specs=[
            pl.BlockSpec((4, 1, nc, npair, LANES), lambda b: (0, b, 0, 0, 0)),
            pl.BlockSpec((4, 1, npair, LANES), lambda b: (0, b, 0, 0)),
        ],
        out_shape=[
            jax.ShapeDtypeStruct(s_loc.shape, F32),
            jax.ShapeDtypeStruct((4, bsz, npair, LANES), F32),
        ],
        compiler_params=_cparams("arbitrary"),
        name="s5_scan",
    )(s_loc, s0, a_pow)


def _s5_out_kernel(u_ref, t_ref, w_ref, xp_ref, y_ref, *, gw):
    u = u_ref[0]
    y = None
    for d in range(2):
        xcat = jnp.concatenate([xp_ref[2 * d, 0], xp_ref[2 * d + 1, 0]], axis=1).astype(BF16)
        carry = _dot(xcat, w_ref[d, 0])
        intra = jnp.concatenate([_dot(u[:, a * gw:(a + 1) * gw], t_ref[d, 0, a]) for a in range(2)], axis=1)
        y = carry + intra if y is None else y + carry + intra
    y_ref[0] = y


def _s5_out(u_flat, toep, w_out_pair, x_prev):
    npair, r, width = u_flat.shape
    gw = width // 2
    tr = min(r, 256)
    return pl.pallas_call(
        functools.partial(_s5_out_kernel, gw=gw),
        grid=(npair, r // tr),
        in_specs=[
            pl.BlockSpec((1, tr, width), lambda p, i: (p, i, 0)),
            pl.BlockSpec((2, 1, 2, gw, gw), lambda p, i: (0, p, 0, 0, 0)),
            pl.BlockSpec((2, 1, 2 * LANES, width), lambda p, i: (0, p, 0, 0)),
            pl.BlockSpec((4, 1, tr, LANES), lambda p, i: (0, p, i, 0)),
        ],
        out_specs=pl.BlockSpec((1, tr, width), lambda p, i: (p, i, 0)),
        out_shape=jax.ShapeDtypeStruct((npair, r, width), F32),
        compiler_params=_cparams("arbitrary", "arbitrary"),
        name="s5_out",
    )(u_flat, toep, w_out_pair, x_prev)


def _s5_filters(lam_re, lam_im, log_dt, b_re, b_im, c_re, c_im):
    hp = lax.Precision.HIGHEST
    ndir, g, p = lam_re.shape
    h = b_re.shape[-1]
    ell = S5_CHUNK
    npair = g // 2
    lr, li = lam_re.astype(F32), lam_im.astype(F32)
    dt = jnp.exp(log_dt.astype(F32))[..., None]
    mag, ang = jnp.exp(lr * dt), li * dt
    abar_re, abar_im = mag * jnp.cos(ang), mag * jnp.sin(ang)
    den = lr * lr + li * li
    nr, ni = abar_re - 1.0, abar_im
    coef_re = ((nr * lr + ni * li) / den)[..., None]
    coef_im = ((ni * lr - nr * li) / den)[..., None]
    bb_re = coef_re * b_re - coef_im * b_im
    bb_im = coef_re * b_im + coef_im * b_re
    tau = jnp.arange(ell + 1, dtype=F32)[:, None, None, None]
    pmag, pang = jnp.exp(tau * (lr * dt)), tau * (li * dt)
    pw_re, pw_im = pmag * jnp.cos(pang), pmag * jnp.sin(pang)
    ca_re = c_re[None] * pw_re[:, :, :, None, :] - c_im[None] * pw_im[:, :, :, None, :]
    ca_im = c_re[None] * pw_im[:, :, :, None, :] + c_im[None] * pw_re[:, :, :, None, :]
    kern = (jnp.einsum('tdghp,dgpk->tdghk', ca_re[:ell], bb_re, precision=hp)
            - jnp.einsum('tdghp,dgpk->tdghk', ca_im[:ell], bb_im, precision=hp))
    lin = jnp.arange(ell)
    toeps, w_ins, w_outs = [], [], []
    for d in range(2):
        lag = (lin[None, :] - lin[:, None]) if d == 0 else (lin[:, None] - lin[None, :])
        kt = kern[jnp.clip(lag, 0, ell - 1), d]
        kt = jnp.where((lag >= 0)[:, :, None, None, None], kt, 0.0)
        toep = kt.transpose(2, 0, 4, 1, 3).reshape(npair, 2, ell * h, ell * h)
        toeps.append(toep)
        steps_in = (ell - 1 - lin) if d == 0 else lin
        wr = pw_re[steps_in, d][..., None] * bb_re[d][None] - pw_im[steps_in, d][..., None] * bb_im[d][None]
        wi = pw_re[steps_in, d][..., None] * bb_im[d][None] + pw_im[steps_in, d][..., None] * bb_re[d][None]
        w_ins.append([w.transpose(1, 0, 3, 2).reshape(g, ell * h, p) for w in (wr, wi)])
        steps_out = (lin + 1) if d == 0 else (ell - lin)
        orr = ca_re[steps_out, d]
        oii = -ca_im[steps_out, d]
        w_outs.append([w.transpose(1, 3, 0, 2).reshape(g, p, ell * h) for w in (orr, oii)])
    eye2 = jnp.eye(2, dtype=F32)

    def pair_cols(w):
        gg, rr, cc = w.shape
        w = w.reshape(npair, 2, rr, cc)
        return jnp.einsum('parc,ab->parbc', w, eye2).reshape(npair, 2 * rr, 2 * cc)

    w_in_pair = jnp.concatenate([pair_cols(w_ins[d][k]) for d in range(2) for k in range(2)], axis=-1)
    w_out_pair = jnp.stack([jnp.concatenate([pair_cols(w_outs[d][k]) for k in range(2)], axis=1) for d in range(2)])
    toep = jnp.stack(toeps)
    a_pow = jnp.stack([pw[ell, d].reshape(npair, 2 * p) for d in range(2) for pw in (pw_re, pw_im)])
    return toep.astype(BF16), w_in_pair.astype(BF16), w_out_pair.astype(BF16), a_pow


def _to_flat(u, npair):
    bsz, n, w = u.shape
    nc = n // S5_CHUNK
    h = w // (2 * npair)
    t = u.reshape(bsz, nc, S5_CHUNK, npair, 2, h).transpose(3, 0, 1, 4, 2, 5)
    return t.reshape(npair, bsz * nc, 2 * S5_CHUNK * h)


def _from_flat(y, bsz, n):
    npair, r, width = y.shape
    nc = n // S5_CHUNK
    h = width // (2 * S5_CHUNK)
    t = y.reshape(npair, bsz, nc, 2, S5_CHUNK, h).transpose(1, 2, 4, 0, 3, 5)
    return t.reshape(bsz, n, npair * 2 * h)


def _s5_mixer(u, filters, s0, with_out):
    toep, w_in_pair, w_out_pair, a_pow = filters
    bsz, n, _ = u.shape
    npair = toep.shape[1]
    nc = n // S5_CHUNK
    u_flat = _to_flat(u.astype(BF16), npair)
    s_loc = _s5_state_in(u_flat, w_in_pair)
    s_loc = s_loc.transpose(0, 2, 1, 3).reshape(4, bsz, nc, npair, LANES)
    x_prev, x_fin = _s5_scan(s_loc, s0, a_pow)
    if not with_out:
        return None, x_fin
    x_prev = x_prev.reshape(4, bsz * nc, npair, LANES).transpose(0, 2, 1, 3)
    y_flat = _s5_out(u_flat, toep, w_out_pair, x_prev)
    return _from_flat(y_flat, bsz, n), x_fin


def _gla_kernel(f_ref, v_ref, q_ref, lb_ref, s0_ref, o_ref, sfin_ref, s_scr, *, blk, reverse):
    j = pl.program_id(2)
    last = pl.num_programs(2) - 1

    @pl.when(j == 0)
    def _():
        s_scr[...] = s0_ref[0, 0]

    lb = lb_ref[...]
    f = lb + (1.0 - lb) * jax.nn.sigmoid(f_ref[0])
    k = 1.0 - f
    hi, mid, lo = _split3(jnp.log(f))
    row = lax.broadcasted_iota(jnp.int32, (blk, blk), 0)
    col = lax.broadcasted_iota(jnp.int32, (blk, blk), 1)
    tri = jnp.where((col >= row) if reverse else (col <= row), 1.0, 0.0).astype(BF16)
    c = _dot(tri, hi) + _dot(tri, mid) + _dot(tri, lo)
    ones = jnp.ones((blk, LANES), BF16)
    tot = _dot_t0(hi, ones) + _dot_t0(mid, ones) + _dot_t0(lo, ones)
    cend = c[0:1] if reverse else c[blk - 1:blk]
    q = q_ref[0]
    vb = v_ref[0].astype(BF16)
    state = s_scr[...]
    o_inter = _dot((q * jnp.exp(c)).astype(BF16), state.astype(BF16))
    kd = (k * jnp.exp(cend - c)).astype(BF16)
    s_scr[...] = jnp.exp(tot) * state + _dot_t0(kd, vb)

    nchunk = blk // HG_CHUNK
    for i in range(nchunk):
        r0, r1 = i * HG_CHUNK, (i + 1) * HG_CHUNK
        if reverse:
            c0, c1 = r0, blk
            ref = c[r1:r1 + 1] if i < nchunk - 1 else jnp.zeros((1, LANES), F32)
        else:
            c0, c1 = 0, r1
            ref = c[r0 - 1:r0] if i > 0 else jnp.zeros((1, LANES), F32)
        qs = (q[r0:r1] * jnp.exp(c[r0:r1] - ref)).astype(BF16)
        ks = (k[c0:c1] * jnp.exp(ref - c[c0:c1])).astype(BF16)
        att = _dot_t1(qs, ks)
        rr = lax.broadcasted_iota(jnp.int32, att.shape, 0) + r0
        cc = lax.broadcasted_iota(jnp.int32, att.shape, 1) + c0
        att = jnp.where((cc >= rr) if reverse else (cc <= rr), att, 0.0)
        o_ref[0, r0:r1, :] = o_inter[r0:r1] + _dot(att.astype(BF16), vb[c0:c1])

    @pl.when(j == last)
    def _():
        sfin_ref[0, 0] = s_scr[...]


def _gla(fraw, v, q, lb, s0, direction):
    bsz, n, hgw = v.shape
    heads = hgw // LANES
    blk = min(n, 256)
    nblk = n // blk
    reverse = direction == 1
    jmap = (lambda j: nblk - 1 - j) if reverse else (lambda j: j)
    return pl.pallas_call(
        functools.partial(_gla_kernel, blk=blk, reverse=reverse),
        grid=(bsz, heads, nblk),
        in_specs=[
            pl.BlockSpec((1, blk, LANES), lambda b, h, j: (b, jmap(j), direction * heads + h)),
            pl.BlockSpec((1, blk, LANES), lambda b, h, j: (b, jmap(j), h)),
            pl.BlockSpec((1, blk, LANES), lambda b, h, j: (b, jmap(j), h)),
            pl.BlockSpec((1, LANES), lambda b, h, j: (0, h)),
            pl.BlockSpec((1, 1, LANES, LANES), lambda b, h, j: (b, h, 0, 0)),
        ],
        out_specs=[
            pl.BlockSpec((1, blk, LANES), lambda b, h, j: (b, jmap(j), h)),
            pl.BlockSpec((1, 1, LANES, LANES), lambda b, h, j: (b, h, 0, 0)),
        ],
        out_shape=[
            jax.ShapeDtypeStruct((bsz, n, hgw), F32),
            jax.ShapeDtypeStruct((bsz, heads, LANES, LANES), F32),
        ],
        scratch_shapes=[pltpu.VMEM((LANES, LANES), F32)],
        compiler_params=_cparams("arbitrary", "arbitrary", "arbitrary"),
        name="gla_bwd" if reverse else "gla_fwd",
    )(fraw, v, q, lb, s0)


def _post_kernel(x_ref, u_ref, y_ref, of_ref, ob_ref, g_ref, gate_ref, d_ref, wg_ref, bg_ref, nw_ref,
                 wo_ref, lg_ref, lbias_ref, o_ref, *, alpha, w5, heads):
    s5_y = jax.nn.gelu(y_ref[0] + u_ref[0] * d_ref[...])
    s5_out = s5_y * jax.nn.sigmoid(_dot(s5_y.astype(BF16), wg_ref[...]) + bg_ref[...])
    proj = _dot(s5_out.astype(BF16), wo_ref[0:w5, :])
    o = of_ref[0] + ob_ref[0]
    gate = g_ref[0]
    nw = nw_ref[...]
    for hd in range(heads):
        sl = slice(hd * LANES, (hd + 1) * LANES)
        oh = o[:, sl]
        ms = jnp.mean(oh * oh, axis=-1, keepdims=True)
        hg = (oh * lax.rsqrt(ms + RMS_EPS) * nw * gate[:, sl]).astype(BF16)
        proj = proj + _dot(hg, wo_ref[w5 + hd * LANES:w5 + (hd + 1) * LANES, :])
    z = alpha * x_ref[0] + gate_ref[0] * proj
    o_ref[0] = _layer_norm(z, lg_ref[...], lbias_ref[...])


def _post(x, u, y5, o_f, o_b, g, gate, d_skip, w_glu, b_glu, norm_w, w_out, ln_g, ln_b, alpha):
    bsz, n, d = x.shape
    w5 = u.shape[-1]
    hgw = g.shape[-1]
    tb = min(n, 512)
    tok = lambda c: pl.BlockSpec((1, tb, c), lambda b, j: (b, j, 0))
    full = lambda a: pl.BlockSpec(a.shape, lambda b, j: (0,) * a.ndim)
    consts = [d_skip.reshape(1, w5), w_glu, b_glu.reshape(1, w5), norm_w.reshape(1, LANES), w_out,
              ln_g.reshape(1, d), ln_b.reshape(1, d)]
    return pl.pallas_call(
        functools.partial(_post_kernel, alpha=alpha, w5=w5, heads=hgw // LANES),
        grid=(bsz, n // tb),
        in_specs=[tok(d), tok(w5), tok(w5), tok(hgw), tok(hgw), tok(hgw),
                  pl.BlockSpec((1, 1, d), lambda b, j: (b, 0, 0))] + [full(a) for a in consts],
        out_specs=tok(d),
        out_shape=jax.ShapeDtypeStruct((bsz, n, d), F32),
        compiler_params=_cparams("arbitrary", "arbitrary"),
        name="mixer_post",
    )(x, u, y5, o_f, o_b, g, gate, *consts)


def _ffn_kernel(x_ref, sh_ref, sc_ref, gate_ref, wa_ref, wg_ref, cwa_ref, cwg_ref, cba_ref, cbg_ref,
                wd_ref, lg_ref, lbias_ref, o_ref, h_scr, acc_scr, *, alpha, row_w, tb):
    j = pl.program_id(2)
    last = pl.num_programs(2) - 1

    @pl.when(j == 0)
    def _():
        h_scr[...] = (x_ref[0] * (1.0 + sc_ref[0]) + sh_ref[0]).astype(BF16)
        acc_scr[...] = jnp.zeros_like(acc_scr)

    h = h_scr[...]
    pos = lax.broadcasted_iota(jnp.int32, (tb, 1), 0) % row_w
    has_prev = pos != 0
    has_next = pos != row_w - 1

    def conv(up, cw_ref, cb_ref):
        prev = jnp.where(has_prev, pltpu.roll(up, 1, 0), 0.0)
        nxt = jnp.where(has_next, pltpu.roll(up, tb - 1, 0), 0.0)
        return prev * cw_ref[0:1, :] + up * cw_ref[1:2, :] + nxt * cw_ref[2:3, :] + cb_ref[...]

    a = conv(_dot(h, wa_ref[...]), cwa_ref, cba_ref)
    g = conv(_dot(h, wg_ref[...]), cwg_ref, cbg_ref)
    acc_scr[...] += _dot((_silu(a) * g).astype(BF16), wd_ref[...])

    @pl.when(j == last)
    def _():
        z = alpha * x_ref[0] + gate_ref[0] * acc_scr[...]
        o_ref[0] = _layer_norm(z, lg_ref[...], lbias_ref[...])


def _conv_ffn(x, shift, scale, gate, w_up, conv_w, conv_b, w_down, ln_g, ln_b, alpha, row_w):
    bsz, n, d = x.shape
    dff = w_down.shape[0]
    tf = 256
    nj = dff // tf
    tb = min(n, 1024)
    conv_b = conv_b.reshape(1, 2 * dff)
    tok = pl.BlockSpec((1, tb, d), lambda b, i, j: (b, i, 0))
    vec = pl.BlockSpec((1, 1, d), lambda b, i, j: (b, 0, 0))
    const = pl.BlockSpec((1, d), lambda b, i, j: (0, 0))
    return pl.pallas_call(
        functools.partial(_ffn_kernel, alpha=alpha, row_w=row_w, tb=tb),
        grid=(bsz, n // tb, nj),
        in_specs=[
            tok, vec, vec, vec,
            pl.BlockSpec((d, tf), lambda b, i, j: (0, j)),
            pl.BlockSpec((d, tf), lambda b, i, j: (0, nj + j)),
            pl.BlockSpec((3, tf), lambda b, i, j: (0, j)),
            pl.BlockSpec((3, tf), lambda b, i, j: (0, nj + j)),
            pl.BlockSpec((1, tf), lambda b, i, j: (0, j)),
            pl.BlockSpec((1, tf), lambda b, i, j: (0, nj + j)),
            pl.BlockSpec((tf, d), lambda b, i, j: (j, 0)),
            const, const,
        ],
        out_specs=tok,
        out_shape=jax.ShapeDtypeStruct((bsz, n, d), F32),
        scratch_shapes=[pltpu.VMEM((tb, d), BF16), pltpu.VMEM((tb, d), F32)],
        compiler_params=_cparams("arbitrary", "arbitrary", "arbitrary"),
        name="conv_ffn",
    )(x, shift, scale, gate, w_up, w_up, conv_w, conv_w, conv_b, conv_b, w_down,
      ln_g.reshape(1, d), ln_b.reshape(1, d))


def _token_mixer(h_in, shift, scale, w_in, filters, lb, init, with_out, w5, hg):
    u, fraw, v, q, g = _inproj(h_in, shift, scale, w_in, w5, hg)
    s5_init, hg_init = init
    y5, s5_fin = _s5_mixer(u, filters, s5_init, with_out)
    o_f, hg_fin_f = _gla(fraw, v, q, lb[0:1], hg_init[0], 0)
    o_b, hg_fin_b = _gla(fraw, v, q, lb[1:2], hg_init[1], 1)
    return (u, y5, o_f, o_b, g), (s5_fin, (hg_fin_f, hg_fin_b))


def kernel(x, c, ctx, c_ctx, w_mod, b_mod, w_in, s5_lam_re, s5_lam_im, s5_log_dt, s5_b_re, s5_b_im,
           s5_c_re, s5_c_im, s5_d, w_glu, b_glu, hg_lb, hg_norm_w, w_out, ln1_g, ln1_b,
           w_up, conv_w, conv_b, w_down, ln2_g, ln2_b):
    depth = w_mod.shape[0]
    bsz, n, d = x.shape
    n_ctx = ctx.shape[1]
    w5 = s5_d.shape[-1]
    hg = hg_lb.shape[-1]
    heads = hg // LANES
    npair = s5_lam_re.shape[2] // 2
    alpha = (2 * depth) ** 0.25

    lb_all = jnp.cumsum(jax.nn.softmax(hg_lb.astype(F32), axis=0), axis=0)
    lb_all = lb_all - lb_all[:1]

    rb = -(-(bsz + 1) // 8) * 8
    rows = jnp.concatenate([c, c_ctx[None], jnp.zeros((rb - bsz - 1, d), F32)], axis=0)
    mods = _mod_vectors(rows, w_mod, b_mod)

    w_in_b, w_glu_b, w_out_b = w_in.astype(BF16), w_glu.astype(BF16), w_out.astype(BF16)
    w_up_b, w_down_b = w_up.astype(BF16), w_down.astype(BF16)

    zero_init = (jnp.zeros((4, bsz, npair, LANES), F32),
                 (jnp.zeros((bsz, heads, LANES, LANES), F32),) * 2)

    for l in range(depth):
        last = l == depth - 1
        mx = [mods[l, :bsz, i * d:(i + 1) * d][:, None, :] for i in range(6)]
        mc = [jnp.broadcast_to(mods[l, bsz, i * d:(i + 1) * d], (bsz, 1, d)) for i in range(6)]
        filters = _s5_filters(s5_lam_re[l], s5_lam_im[l], s5_log_dt[l], s5_b_re[l], s5_b_im[l],
                              s5_c_re[l], s5_c_im[l])
        lb = lb_all[l]
        c_parts, ctx_states = _token_mixer(ctx, mc[0], mc[1], w_in_b[l], filters, lb, zero_init, not last, w5, hg)
        x_parts, _ = _token_mixer(x, mx[0], mx[1], w_in_b[l], filters, lb, ctx_states, True, w5, hg)
        post = functools.partial(_post, d_skip=s5_d[l], w_glu=w_glu_b[l], b_glu=b_glu[l], norm_w=hg_norm_w[l],
                                 w_out=w_out_b[l], ln_g=ln1_g[l], ln_b=ln1_b[l], alpha=alpha)
        ffn = functools.partial(_conv_ffn, w_up=w_up_b[l], conv_w=conv_w[l], conv_b=conv_b[l], w_down=w_down_b[l],
                                ln_g=ln2_g[l], ln_b=ln2_b[l], alpha=alpha)
        x = post(x, *x_parts, gate=mx[2])
        x = ffn(x, mx[3], mx[4], mx[5], row_w=GRID_W)
        if not last:
            ctx = post(ctx, *c_parts, gate=mc[2])
            ctx = ffn(ctx, mc[3], mc[4], mc[5], row_w=n_ctx)
    return x
```

```python
import functools
import math

import jax
import jax.numpy as jnp
from jax import lax
from jax.experimental import pallas as pl
from jax.experimental.pallas import tpu as pltpu

F32 = jnp.float32
BF16 = jnp.bfloat16

GRID_W = 64
HG_CHUNK = 32
S5_CHUNK = 16
DT_MIN = 1e-3
DT_MAX = 1e-1
LN_EPS = 1e-5
RMS_EPS = 1e-6
LANES = 128
VMEM_LIMIT = 56 * 1024 * 1024


def _cparams(*sem):
    return pltpu.CompilerParams(dimension_semantics=sem, vmem_limit_bytes=VMEM_LIMIT)


def _silu(x):
    return x * jax.nn.sigmoid(x)


def _dot(a, b):
    return jnp.dot(a, b, preferred_element_type=F32)


def _dot_t0(a, b):
    return lax.dot_general(a, b, (((0,), (0,)), ((), ())), preferred_element_type=F32)


def _dot_t1(a, b):
    return lax.dot_general(a, b, (((1,), (1,)), ((), ())), preferred_element_type=F32)


def _split3(x):
    hi = x.astype(BF16)
    r1 = x - hi.astype(F32)
    mid = r1.astype(BF16)
    lo = (r1 - mid.astype(F32)).astype(BF16)
    return hi, mid, lo


def _layer_norm(z, g, b):
    mu = jnp.mean(z, axis=-1, keepdims=True)
    zc = z - mu
    var = jnp.mean(zc * zc, axis=-1, keepdims=True)
    return zc * lax.rsqrt(var + LN_EPS) * g + b


def _mod_kernel(c_ref, w_ref, b_ref, o_ref):
    s = _silu(c_ref[...]).astype(BF16)
    o_ref[0] = _dot(s, w_ref[0].astype(BF16)) + b_ref[0]


def _mod_vectors(rows, w_mod, b_mod):
    depth, d, d6 = w_mod.shape
    rb = rows.shape[0]
    tn = 1536 if d6 % 1536 == 0 else d6
    return pl.pallas_call(
        _mod_kernel,
        grid=(depth, d6 // tn),
        in_specs=[
            pl.BlockSpec((rb, d), lambda l, j: (0, 0)),
            pl.BlockSpec((1, d, tn), lambda l, j: (l, 0, j)),
            pl.BlockSpec((1, 1, tn), lambda l, j: (l, 0, j)),
        ],
        out_specs=pl.BlockSpec((1, rb, tn), lambda l, j: (l, 0, j)),
        out_shape=jax.ShapeDtypeStruct((depth, rb, d6), F32),
        compiler_params=_cparams("arbitrary", "arbitrary"),
        name="mod_vectors",
    )(rows, w_mod, b_mod.reshape(depth, 1, d6))


def _inproj_kernel(x_ref, sh_ref, sc_ref, w_ref, u_ref, f_ref, v_ref, q_ref, g_ref, *, w5, hg):
    h = (x_ref[0] * (1.0 + sc_ref[0]) + sh_ref[0]).astype(BF16)
    o = 0
    u_ref[0] = _dot(h, w_ref[:, o:o + w5]); o += w5
    f_ref[0] = _dot(h, w_ref[:, o:o + 2 * hg]); o += 2 * hg
    v_ref[0] = _dot(h, w_ref[:, o:o + hg]); o += hg
    q_ref[0] = _silu(_dot(h, w_ref[:, o:o + hg])); o += hg
    g_ref[0] = _silu(_dot(h, w_ref[:, o:o + hg]))


def _inproj(x, shift, scale, w_in, w5, hg):
    bsz, n, d = x.shape
    cols = w_in.shape[1]
    tb = min(n, 512)
    tok = lambda c: pl.BlockSpec((1, tb, c), lambda b, j: (b, j, 0))
    vec = pl.BlockSpec((1, 1, d), lambda b, j: (b, 0, 0))
    return pl.pallas_call(
        functools.partial(_inproj_kernel, w5=w5, hg=hg),
        grid=(bsz, n // tb),
        in_specs=[tok(d), vec, vec, pl.BlockSpec((d, cols), lambda b, j: (0, 0))],
        out_specs=[tok(w5), tok(2 * hg), tok(hg), tok(hg), tok(hg)],
        out_shape=[jax.ShapeDtypeStruct((bsz, n, c), F32) for c in (w5, 2 * hg, hg, hg, hg)],
        compiler_params=_cparams("arbitrary", "arbitrary"),
        name="inproj",
    )(x, shift, scale, w_in)


def _flat_col(g, l, h):
    return (g % 2) * S5_CHUNK * h + l * h


def _s5_in_kernel(*refs, tr, h, nblk):
    u_refs = refs[:nblk]
    t_ref, w_ref, yi_ref, s_ref, uf_scr = refs[nblk:]
    npair = uf_scr.shape[0]
    gw = S5_CHUNK * h
    gpb = LANES // h
    for blk, u_ref in enumerate(u_refs):
        for l in range(S5_CHUNK):
            piece = u_ref[pl.ds(l, tr, stride=S5_CHUNK), :]
            for gi in range(gpb):
                g = blk * gpb + gi
                col = _flat_col(g, l, h)
                uf_scr[g // 2, :, col:col + h] = piece[:, gi * h:(gi + 1) * h]
    for p in range(npair):
        ub = uf_scr[p].astype(BF16)
        s = _dot(ub, w_ref[p])
        for k in range(4):
            s_ref[k, pl.ds(p, tr, stride=npair), :] = s[:, k * LANES:(k + 1) * LANES]
        yi_ref[p] = jnp.concatenate([_dot(ub[:, a * gw:(a + 1) * gw], t_ref[p, a]) for a in range(2)], axis=1)


def _s5_in(u2d, toep, w_in_pair, h):
    tokens, w5 = u2d.shape
    npair, _, gw, _ = toep.shape
    nblk = w5 // LANES
    r = tokens // S5_CHUNK
    tr = min(r, 64)
    const = lambda a: pl.BlockSpec(a.shape, lambda i: (0,) * a.ndim, pipeline_mode=pl.Buffered(1))
    return pl.pallas_call(
        functools.partial(_s5_in_kernel, tr=tr, h=h, nblk=nblk),
        grid=(r // tr,),
        in_specs=[pl.BlockSpec((tr * S5_CHUNK, LANES), functools.partial(lambda i, blk: (i, blk), blk=blk))
                  for blk in range(nblk)] + [const(toep), const(w_in_pair)],
        out_specs=[
            pl.BlockSpec((npair, tr, 2 * gw), lambda i: (0, i, 0)),
            pl.BlockSpec((4, tr * npair, LANES), lambda i: (0, i, 0)),
        ],
        out_shape=[
            jax.ShapeDtypeStruct((npair, r, 2 * gw), F32),
            jax.ShapeDtypeStruct((4, r * npair, LANES), F32),
        ],
        scratch_shapes=[pltpu.VMEM((npair, tr, 2 * gw), F32)],
        compiler_params=_cparams("arbitrary"),
        name="s5_in",
    )(*([u2d] * nblk), toep, w_in_pair)


def _s5_scan_kernel(s_ref, s0_ref, a_ref, xp_ref, xf_ref, *, nc):
    far, fai, bar, bai = a_ref[0], a_ref[1], a_ref[2], a_ref[3]

    def body(i, carry):
        fr, fi, br, bi = carry
        ib = nc - 1 - i
        xp_ref[0, 0, i] = fr
        xp_ref[1, 0, i] = fi
        xp_ref[2, 0, ib] = br
        xp_ref[3, 0, ib] = bi
        nfr = far * fr - fai * fi + s_ref[0, 0, i]
        nfi = far * fi + fai * fr + s_ref[1, 0, i]
        nbr = bar * br - bai * bi + s_ref[2, 0, ib]
        nbi = bar * bi + bai * br + s_ref[3, 0, ib]
        return nfr, nfi, nbr, nbi

    init = (s0_ref[0, 0], s0_ref[1, 0], s0_ref[2, 0], s0_ref[3, 0])
    fr, fi, br, bi = lax.fori_loop(0, nc, body, init)
    xf_ref[0, 0] = fr
    xf_ref[1, 0] = fi
    xf_ref[2, 0] = br
    xf_ref[3, 0] = bi


def _s5_scan(s_loc, s0, a_pow):
    _, bsz, nc, npair, _ = s_loc.shape
    return pl.pallas_call(
        functools.partial(_s5_scan_kernel, nc=nc),
        grid=(bsz,),
        in_specs=[
            pl.BlockSpec((4, 1, nc, npair, LANES), lambda b: (0, b, 0, 0, 0)),
            pl.BlockSpec((4, 1, npair, LANES), lambda b: (0, b, 0, 0)),
            pl.BlockSpec((4, npair, LANES), lambda b: (0, 0, 0)),
        ],
        out_specs=[
            pl.BlockSpec((4, 1, nc, npair, LANES), lambda b: (0, b, 0, 0, 0)),
            pl.BlockSpec((4, 1, npair, LANES), lambda b: (0, b, 0, 0)),
        ],
        out_shape=[
            jax.ShapeDtypeStruct(s_loc.shape, F32),
            jax.ShapeDtypeStruct((4, bsz, npair, LANES), F32),
        ],
        compiler_params=_cparams("arbitrary"),
        name="s5_scan",
    )(s_loc, s0, a_pow)


def _s5_out_kernel(*refs, tr, h, nblk):
    yi_ref, xp_ref, w_ref = refs[:3]
    y_refs = refs[3:3 + nblk]
    yf_scr, row_scr = refs[3 + nblk:]
    npair = yi_ref.shape[0]
    gpb = LANES // h
    for p in range(npair):
        xcat = jnp.concatenate([xp_ref[k, pl.ds(p, tr, stride=npair), :] for k in range(4)], axis=1)
        yf_scr[p] = yi_ref[p] + _dot(xcat.astype(BF16), w_ref[p])
    for blk, y_ref in enumerate(y_refs):
        for l in range(S5_CHUNK):
            slot = (blk * S5_CHUNK + l) % row_scr.shape[0]
            for gi in range(gpb):
                g = blk * gpb + gi
                col = _flat_col(g, l, h)
                row_scr[slot, :, gi * h:(gi + 1) * h] = yf_scr[g // 2, :, col:col + h]
            y_ref[pl.ds(l, tr, stride=S5_CHUNK), :] = row_scr[slot]


def _s5_out(y_intra, x_prev, w_out_pair, h):
    npair, r, width = y_intra.shape
    nblk = npair * 2 * h // LANES
    tr = min(r, 64)
    return pl.pallas_call(
        functools.partial(_s5_out_kernel, tr=tr, h=h, nblk=nblk),
        grid=(r // tr,),
        in_specs=[
            pl.BlockSpec((npair, tr, width), lambda i: (0, i, 0)),
            pl.BlockSpec((4, tr * npair, LANES), lambda i: (0, i, 0)),
            pl.BlockSpec(w_out_pair.shape, lambda i: (0, 0, 0), pipeline_mode=pl.Buffered(1)),
        ],
        out_specs=[pl.BlockSpec((tr * S5_CHUNK, LANES), lambda i: (i, 0))] * nblk,
        out_shape=[jax.ShapeDtypeStruct((r * S5_CHUNK, LANES), F32)] * nblk,
        scratch_shapes=[pltpu.VMEM((npair, tr, width), F32), pltpu.VMEM((4, tr, LANES), F32)],
        compiler_params=_cparams("arbitrary"),
        name="s5_out",
    )(y_intra, x_prev, w_out_pair)


def _s5_filters(lam_re, lam_im, log_dt, b_re, b_im, c_re, c_im):
    hp = lax.Precision.HIGHEST
    ndir, g, p = lam_re.shape
    h = b_re.shape[-1]
    ell = S5_CHUNK
    npair = g // 2
    lr, li = lam_re.astype(F32), lam_im.astype(F32)
    dt = jnp.exp(log_dt.astype(F32))[..., None]
    mag, ang = jnp.exp(lr * dt), li * dt
    abar_re, abar_im = mag * jnp.cos(ang), mag * jnp.sin(ang)
    den = lr * lr + li * li
    nr, ni = abar_re - 1.0, abar_im
    coef_re = ((nr * lr + ni * li) / den)[..., None]
    coef_im = ((ni * lr - nr * li) / den)[..., None]
    bb_re = coef_re * b_re - coef_im * b_im
    bb_im = coef_re * b_im + coef_im * b_re
    tau = jnp.arange(ell + 1, dtype=F32)[:, None, None, None]
    pmag, pang = jnp.exp(tau * (lr * dt)), tau * (li * dt)
    pw_re, pw_im = pmag * jnp.cos(pang), pmag * jnp.sin(pang)
    ca_re = c_re[None] * pw_re[:, :, :, None, :] - c_im[None] * pw_im[:, :, :, None, :]
    ca_im = c_re[None] * pw_im[:, :, :, None, :] + c_im[None] * pw_re[:, :, :, None, :]
    kern = (jnp.einsum('tdghp,dgpk->tdghk', ca_re[:ell], bb_re, precision=hp)
            - jnp.einsum('tdghp,dgpk->tdghk', ca_im[:ell], bb_im, precision=hp))
    lin = jnp.arange(ell)
    toeps, w_ins, w_outs = [], [], []
    for d in range(2):
        lag = (lin[None, :] - lin[:, None]) if d == 0 else (lin[:, None] - lin[None, :])
        kt = kern[jnp.clip(lag, 0, ell - 1), d]
        kt = jnp.where((lag >= 0)[:, :, None, None, None], kt, 0.0)
        toep = kt.transpose(2, 0, 4, 1, 3).reshape(npair, 2, ell * h, ell * h)
        toeps.append(toep)
        steps_in = (ell - 1 - lin) if d == 0 else lin
        wr = pw_re[steps_in, d][..., None] * bb_re[d][None] - pw_im[steps_in, d][..., None] * bb_im[d][None]
        wi = pw_re[steps_in, d][..., None] * bb_im[d][None] + pw_im[steps_in, d][..., None] * bb_re[d][None]
        w_ins.append([w.transpose(1, 0, 3, 2).reshape(g, ell * h, p) for w in (wr, wi)])
        steps_out = (lin + 1) if d == 0 else (ell - lin)
        orr = ca_re[steps_out, d]
        oii = -ca_im[steps_out, d]
        w_outs.append([w.transpose(1, 3, 0, 2).reshape(g, p, ell * h) for w in (orr, oii)])
    eye2 = jnp.eye(2, dtype=F32)

    def pair_cols(w):
        gg, rr, cc = w.shape
        w = w.reshape(npair, 2, rr, cc)
        return jnp.einsum('parc,ab->parbc', w, eye2).reshape(npair, 2 * rr, 2 * cc)

    w_in_pair = jnp.concatenate([pair_cols(w_ins[d][k]) for d in range(2) for k in range(2)], axis=-1)
    w_out_pair = jnp.concatenate([pair_cols(w_outs[d][k]) for d in range(2) for k in range(2)], axis=1)
    toep = toeps[0] + toeps[1]
    a_pow = jnp.stack([pw[ell, d].reshape(npair, 2 * p) for d in range(2) for pw in (pw_re, pw_im)])
    return toep.astype(BF16), w_in_pair.astype(BF16), w_out_pair.astype(BF16), a_pow


def _s5_mixer(u, filters, s0, with_out):
    toep, w_in_pair, w_out_pair, a_pow = filters
    bsz, n, w5 = u.shape
    npair = toep.shape[0]
    h = w5 // (2 * npair)
    nc = n // S5_CHUNK
    y_intra, s_loc = _s5_in(u.reshape(bsz * n, w5), toep, w_in_pair, h)
    x_prev, x_fin = _s5_scan(s_loc.reshape(4, bsz, nc, npair, LANES), s0, a_pow)
    if not with_out:
        return None, x_fin
    y_blocks = _s5_out(y_intra, x_prev.reshape(4, bsz * nc * npair, LANES), w_out_pair, h)
    return [y.reshape(bsz, n, LANES) for y in y_blocks], x_fin


def _gla_kernel(f_ref, v_ref, q_ref, lb_ref, s0_ref, o_ref, sfin_ref, s_scr, *, blk, reverse):
    j = pl.program_id(2)
    last = pl.num_programs(2) - 1

    @pl.when(j == 0)
    def _():
        s_scr[...] = s0_ref[0, 0]

    lb = lb_ref[...]
    f = lb + (1.0 - lb) * jax.nn.sigmoid(f_ref[0])
    k = 1.0 - f
    hi, mid, lo = _split3(jnp.log(f))
    row = lax.broadcasted_iota(jnp.int32, (blk, blk), 0)
    col = lax.broadcasted_iota(jnp.int32, (blk, blk), 1)
    tri = jnp.where((col >= row) if reverse else (col <= row), 1.0, 0.0).astype(BF16)
    c = _dot(tri, hi) + _dot(tri, mid) + _dot(tri, lo)
    ones = jnp.ones((blk, LANES), BF16)
    tot = _dot_t0(hi, ones) + _dot_t0(mid, ones) + _dot_t0(lo, ones)
    cend = c[0:1] if reverse else c[blk - 1:blk]
    q = q_ref[0]
    vb = v_ref[0].astype(BF16)
    state = s_scr[...]
    o_inter = _dot((q * jnp.exp(c)).astype(BF16), state.astype(BF16))
    kd = (k * jnp.exp(cend - c)).astype(BF16)
    s_scr[...] = jnp.exp(tot) * state + _dot_t0(kd, vb)

    nchunk = blk // HG_CHUNK
    for i in range(nchunk):
        r0, r1 = i * HG_CHUNK, (i + 1) * HG_CHUNK
        if reverse:
            c0, c1 = r0, blk
            ref = c[r1:r1 + 1] if i < nchunk - 1 else jnp.zeros((1, LANES), F32)
        else:
            c0, c1 = 0, r1
            ref = c[r0 - 1:r0] if i > 0 else jnp.zeros((1, LANES), F32)
        qs = (q[r0:r1] * jnp.exp(c[r0:r1] - ref)).astype(BF16)
        ks = (k[c0:c1] * jnp.exp(ref - c[c0:c1])).astype(BF16)
        att = _dot_t1(qs, ks)
        rr = lax.broadcasted_iota(jnp.int32, att.shape, 0) + r0
        cc = lax.broadcasted_iota(jnp.int32, att.shape, 1) + c0
        att = jnp.where((cc >= rr) if reverse else (cc <= rr), att, 0.0)
        o_ref[0, r0:r1, :] = o_inter[r0:r1] + _dot(att.astype(BF16), vb[c0:c1])

    @pl.when(j == last)
    def _():
        sfin_ref[0, 0] = s_scr[...]


def _gla(fraw, v, q, lb, s0, direction):
    bsz, n, hgw = v.shape
    heads = hgw // LANES
    blk = min(n, 256)
    nblk = n // blk
    reverse = direction == 1
    jmap = (lambda j: nblk - 1 - j) if reverse else (lambda j: j)
    return pl.pallas_call(
        functools.partial(_gla_kernel, blk=blk, reverse=reverse),
        grid=(bsz, heads, nblk),
        in_specs=[
            pl.BlockSpec((1, blk, LANES), lambda b, h, j: (b, jmap(j), direction * heads + h)),
            pl.BlockSpec((1, blk, LANES), lambda b, h, j: (b, jmap(j), h)),
            pl.BlockSpec((1, blk, LANES), lambda b, h, j: (b, jmap(j), h)),
            pl.BlockSpec((1, LANES), lambda b, h, j: (0, h)),
            pl.BlockSpec((1, 1, LANES, LANES), lambda b, h, j: (b, h, 0, 0)),
        ],
        out_specs=[
            pl.BlockSpec((1, blk, LANES), lambda b, h, j: (b, jmap(j), h)),
            pl.BlockSpec((1, 1, LANES, LANES), lambda b, h, j: (b, h, 0, 0)),
        ],
        out_shape=[
            jax.ShapeDtypeStruct((bsz, n, hgw), F32),
            jax.ShapeDtypeStruct((bsz, heads, LANES, LANES), F32),
        ],
        scratch_shapes=[pltpu.VMEM((LANES, LANES), F32)],
        compiler_params=_cparams("arbitrary", "arbitrary", "arbitrary"),
        name="gla_bwd" if reverse else "gla_fwd",
    )(fraw, v, q, lb, s0)


def _post_kernel(*refs, alpha, w5, heads, nblk):
    y_refs = refs[:nblk]
    (x_ref, u_ref, of_ref, ob_ref, g_ref, gate_ref, d_ref, wg_ref, bg_ref, nw_ref,
     wo_ref, lg_ref, lbias_ref, o_ref) = refs[nblk:]
    y5 = jnp.concatenate([y_ref[0] for y_ref in y_refs], axis=1)
    s5_y = jax.nn.gelu(y5 + u_ref[0] * d_ref[...])
    s5_out = s5_y * jax.nn.sigmoid(_dot(s5_y.astype(BF16), wg_ref[...]) + bg_ref[...])
    proj = _dot(s5_out.astype(BF16), wo_ref[0:w5, :])
    o = of_ref[0] + ob_ref[0]
    gate = g_ref[0]
    nw = nw_ref[...]
    for hd in range(heads):
        sl = slice(hd * LANES, (hd + 1) * LANES)
        oh = o[:, sl]
        ms = jnp.mean(oh * oh, axis=-1, keepdims=True)
        hg = (oh * lax.rsqrt(ms + RMS_EPS) * nw * gate[:, sl]).astype(BF16)
        proj = proj + _dot(hg, wo_ref[w5 + hd * LANES:w5 + (hd + 1) * LANES, :])
    z = alpha * x_ref[0] + gate_ref[0] * proj
    o_ref[0] = _layer_norm(z, lg_ref[...], lbias_ref[...])


def _post(x, u, y5, o_f, o_b, g, gate, d_skip, w_glu, b_glu, norm_w, w_out, ln_g, ln_b, alpha):
    bsz, n, d = x.shape
    w5 = u.shape[-1]
    hgw = g.shape[-1]
    tb = min(n, 512)
    tok = lambda c: pl.BlockSpec((1, tb, c), lambda b, j: (b, j, 0))
    full = lambda a: pl.BlockSpec(a.shape, lambda b, j: (0,) * a.ndim)
    consts = [d_skip.reshape(1, w5), w_glu, b_glu.reshape(1, w5), norm_w.reshape(1, LANES), w_out,
              ln_g.reshape(1, d), ln_b.reshape(1, d)]
    nblk = len(y5)
    return pl.pallas_call(
        functools.partial(_post_kernel, alpha=alpha, w5=w5, heads=hgw // LANES, nblk=nblk),
        grid=(bsz, n // tb),
        in_specs=[tok(LANES)] * nblk + [tok(d), tok(w5), tok(hgw), tok(hgw), tok(hgw),
                                        pl.BlockSpec((1, 1, d), lambda b, j: (b, 0, 0))] + [full(a) for a in consts],
        out_specs=tok(d),
        out_shape=jax.ShapeDtypeStruct((bsz, n, d), F32),
        compiler_params=_cparams("arbitrary", "arbitrary"),
        name="mixer_post",
    )(*y5, x, u, o_f, o_b, g, gate, *consts)


def _ffn_kernel(x_ref, sh_ref, sc_ref, gate_ref, wa_ref, wg_ref, cwa_ref, cwg_ref, cba_ref, cbg_ref,
                wd_ref, lg_ref, lbias_ref, o_ref, h_scr, acc_scr, *, alpha, row_w, tb):
    j = pl.program_id(2)
    last = pl.num_programs(2) - 1

    @pl.when(j == 0)
    def _():
        h_scr[...] = (x_ref[0] * (1.0 + sc_ref[0]) + sh_ref[0]).astype(BF16)
        acc_scr[...] = jnp.zeros_like(acc_scr)

    h = h_scr[...]
    pos = lax.broadcasted_iota(jnp.int32, (tb, 1), 0) % row_w
    has_prev = pos != 0
    has_next = pos != row_w - 1

    def conv(up, cw_ref, cb_ref):
        prev = jnp.where(has_prev, pltpu.roll(up, 1, 0), 0.0)
        nxt = jnp.where(has_next, pltpu.roll(up, tb - 1, 0), 0.0)
        return prev * cw_ref[0:1, :] + up * cw_ref[1:2, :] + nxt * cw_ref[2:3, :] + cb_ref[...]

    a = conv(_dot(h, wa_ref[...]), cwa_ref, cba_ref)
    g = conv(_dot(h, wg_ref[...]), cwg_ref, cbg_ref)
    acc_scr[...] += _dot((_silu(a) * g).astype(BF16), wd_ref[...])

    @pl.when(j == last)
    def _():
        z = alpha * x_ref[0] + gate_ref[0] * acc_scr[...]
        o_ref[0] = _layer_norm(z, lg_ref[...], lbias_ref[...])


def _conv_ffn(x, shift, scale, gate, w_up, conv_w, conv_b, w_down, ln_g, ln_b, alpha, row_w):
    bsz, n, d = x.shape
    dff = w_down.shape[0]
    tf = 256
    nj = dff // tf
    tb = min(n, 1024)
    conv_b = conv_b.reshape(1, 2 * dff)
    tok = pl.BlockSpec((1, tb, d), lambda b, i, j: (b, i, 0))
    vec = pl.BlockSpec((1, 1, d), lambda b, i, j: (b, 0, 0))
    const = pl.BlockSpec((1, d), lambda b, i, j: (0, 0))
    return pl.pallas_call(
        functools.partial(_ffn_kernel, alpha=alpha, row_w=row_w, tb=tb),
        grid=(bsz, n // tb, nj),
        in_specs=[
            tok, vec, vec, vec,
            pl.BlockSpec((d, tf), lambda b, i, j: (0, j)),
            pl.BlockSpec((d, tf), lambda b, i, j: (0, nj + j)),
            pl.BlockSpec((3, tf), lambda b, i, j: (0, j)),
            pl.BlockSpec((3, tf), lambda b, i, j: (0, nj + j)),
            pl.BlockSpec((1, tf), lambda b, i, j: (0, j)),
            pl.BlockSpec((1, tf), lambda b, i, j: (0, nj + j)),
            pl.BlockSpec((tf, d), lambda b, i, j: (j, 0)),
            const, const,
        ],
        out_specs=tok,
        out_shape=jax.ShapeDtypeStruct((bsz, n, d), F32),
        scratch_shapes=[pltpu.VMEM((tb, d), BF16), pltpu.VMEM((tb, d), F32)],
        compiler_params=_cparams("arbitrary", "arbitrary", "arbitrary"),
        name="conv_ffn",
    )(x, shift, scale, gate, w_up, w_up, conv_w, conv_w, conv_b, conv_b, w_down,
      ln_g.reshape(1, d), ln_b.reshape(1, d))


def _token_mixer(h_in, shift, scale, w_in, filters, lb, init, with_out, w5, hg):
    u, fraw, v, q, g = _inproj(h_in, shift, scale, w_in, w5, hg)
    s5_init, hg_init = init
    y5, s5_fin = _s5_mixer(u, filters, s5_init, with_out)
    o_f, hg_fin_f = _gla(fraw, v, q, lb[0:1], hg_init[0], 0)
    o_b, hg_fin_b = _gla(fraw, v, q, lb[1:2], hg_init[1], 1)
    return (u, y5, o_f, o_b, g), (s5_fin, (hg_fin_f, hg_fin_b))


def kernel(x, c, ctx, c_ctx, w_mod, b_mod, w_in, s5_lam_re, s5_lam_im, s5_log_dt, s5_b_re, s5_b_im,
           s5_c_re, s5_c_im, s5_d, w_glu, b_glu, hg_lb, hg_norm_w, w_out, ln1_g, ln1_b,
           w_up, conv_w, conv_b, w_down, ln2_g, ln2_b):
    depth = w_mod.shape[0]
    bsz, n, d = x.shape
    n_ctx = ctx.shape[1]
    w5 = s5_d.shape[-1]
    hg = hg_lb.shape[-1]
    heads = hg // LANES
    npair = s5_lam_re.shape[2] // 2
    alpha = (2 * depth) ** 0.25

    lb_all = jnp.cumsum(jax.nn.softmax(hg_lb.astype(F32), axis=0), axis=0)
    lb_all = lb_all - lb_all[:1]

    rb = -(-(bsz + 1) // 8) * 8
    rows = jnp.concatenate([c, c_ctx[None], jnp.zeros((rb - bsz - 1, d), F32)], axis=0)
    mods = _mod_vectors(rows, w_mod, b_mod)

    w_in_b, w_glu_b, w_out_b = w_in.astype(BF16), w_glu.astype(BF16), w_out.astype(BF16)
    w_up_b, w_down_b = w_up.astype(BF16), w_down.astype(BF16)

    zero_init = (jnp.zeros((4, bsz, npair, LANES), F32),
                 (jnp.zeros((bsz, heads, LANES, LANES), F32),) * 2)

    for l in range(depth):
        last = l == depth - 1
        mx = [mods[l, :bsz, i * d:(i + 1) * d][:, None, :] for i in range(6)]
        mc = [jnp.broadcast_to(mods[l, bsz, i * d:(i + 1) * d], (bsz, 1, d)) for i in range(6)]
        filters = _s5_filters(s5_lam_re[l], s5_lam_im[l], s5_log_dt[l], s5_b_re[l], s5_b_im[l],
                              s5_c_re[l], s5_c_im[l])
        lb = lb_all[l]
        c_parts, ctx_states = _token_mixer(ctx, mc[0], mc[1], w_in_b[l], filters, lb, zero_init, not last, w5, hg)
        x_parts, _ = _token_mixer(x, mx[0], mx[1], w_in_b[l], filters, lb, ctx_states, True, w5, hg)
        post = functools.partial(_post, d_skip=s5_d[l], w_glu=w_glu_b[l], b_glu=b_glu[l], norm_w=hg_norm_w[l],
                                 w_out=w_out_b[l], ln_g=ln1_g[l], ln_b=ln1_b[l], alpha=alpha)
        ffn = functools.partial(_conv_ffn, w_up=w_up_b[l], conv_w=conv_w[l], conv_b=conv_b[l], w_down=w_down_b[l],
                                ln_g=ln2_g[l], ln_b=ln2_b[l], alpha=alpha)
        x = post(x, *x_parts, gate=mx[2])
        x = ffn(x, mx[3], mx[4], mx[5], row_w=GRID_W)
        if not last:
            ctx = post(ctx, *c_parts, gate=mc[2])
            ctx = ffn(ctx, mc[3], mc[4], mc[5], row_w=n_ctx)
    return x
```

```python
import functools
import math

import jax
import jax.numpy as jnp
from jax import lax
from jax.experimental import pallas as pl
from jax.experimental.pallas import tpu as pltpu

F32 = jnp.float32
BF16 = jnp.bfloat16

GRID_W = 64
HG_CHUNK = 32
S5_CHUNK = 16
DT_MIN = 1e-3
DT_MAX = 1e-1
LN_EPS = 1e-5
RMS_EPS = 1e-6
LANES = 128
VMEM_LIMIT = 56 * 1024 * 1024


def _cparams(*sem):
    return pltpu.CompilerParams(dimension_semantics=sem, vmem_limit_bytes=VMEM_LIMIT)


def _silu(x):
    return x * jax.nn.sigmoid(x)


def _dot(a, b):
    return jnp.dot(a, b, preferred_element_type=F32)


def _dot_t0(a, b):
    return lax.dot_general(a, b, (((0,), (0,)), ((), ())), preferred_element_type=F32)


def _dot_t1(a, b):
    return lax.dot_general(a, b, (((1,), (1,)), ((), ())), preferred_element_type=F32)


def _split3(x):
    hi = x.astype(BF16)
    r1 = x - hi.astype(F32)
    mid = r1.astype(BF16)
    lo = (r1 - mid.astype(F32)).astype(BF16)
    return hi, mid, lo


def _layer_norm(z, g, b):
    mu = jnp.mean(z, axis=-1, keepdims=True)
    zc = z - mu
    var = jnp.mean(zc * zc, axis=-1, keepdims=True)
    return zc * lax.rsqrt(var + LN_EPS) * g + b


def _mod_kernel(c_ref, w_ref, b_ref, o_ref):
    s = _silu(c_ref[...]).astype(BF16)
    o_ref[0] = _dot(s, w_ref[0].astype(BF16)) + b_ref[0]


def _mod_vectors(rows, w_mod, b_mod):
    depth, d, d6 = w_mod.shape
    rb = rows.shape[0]
    tn = 1536 if d6 % 1536 == 0 else d6
    return pl.pallas_call(
        _mod_kernel,
        grid=(depth, d6 // tn),
        in_specs=[
            pl.BlockSpec((rb, d), lambda l, j: (0, 0)),
            pl.BlockSpec((1, d, tn), lambda l, j: (l, 0, j)),
            pl.BlockSpec((1, 1, tn), lambda l, j: (l, 0, j)),
        ],
        out_specs=pl.BlockSpec((1, rb, tn), lambda l, j: (l, 0, j)),
        out_shape=jax.ShapeDtypeStruct((depth, rb, d6), F32),
        compiler_params=_cparams("arbitrary", "arbitrary"),
        name="mod_vectors",
    )(rows, w_mod, b_mod.reshape(depth, 1, d6))


def _inproj_kernel(x_ref, sh_ref, sc_ref, w_ref, u_ref, f_ref, v_ref, q_ref, g_ref, *, w5, hg):
    h = (x_ref[0] * (1.0 + sc_ref[0]) + sh_ref[0]).astype(BF16)
    o = 0
    u_ref[0] = _dot(h, w_ref[:, o:o + w5]); o += w5
    f_ref[0] = _dot(h, w_ref[:, o:o + 2 * hg]); o += 2 * hg
    v_ref[0] = _dot(h, w_ref[:, o:o + hg]); o += hg
    q_ref[0] = _silu(_dot(h, w_ref[:, o:o + hg])); o += hg
    g_ref[0] = _silu(_dot(h, w_ref[:, o:o + hg]))


def _inproj(x, shift, scale, w_in, w5, hg):
    bsz, n, d = x.shape
    cols = w_in.shape[1]
    tb = min(n, 512)
    tok = lambda c: pl.BlockSpec((1, tb, c), lambda b, j: (b, j, 0))
    vec = pl.BlockSpec((1, 1, d), lambda b, j: (b, 0, 0))
    return pl.pallas_call(
        functools.partial(_inproj_kernel, w5=w5, hg=hg),
        grid=(bsz, n // tb),
        in_specs=[tok(d), vec, vec, pl.BlockSpec((d, cols), lambda b, j: (0, 0))],
        out_specs=[tok(w5), tok(2 * hg), tok(hg), tok(hg), tok(hg)],
        out_shape=[jax.ShapeDtypeStruct((bsz, n, c), F32) for c in (w5, 2 * hg, hg, hg, hg)],
        compiler_params=_cparams("arbitrary", "arbitrary"),
        name="inproj",
    )(x, shift, scale, w_in)


def _flat_col(g, l, h):
    return (g % 2) * S5_CHUNK * h + l * h


def _s5_in_kernel(*refs, tr, h, nblk):
    u_refs = refs[:nblk]
    t_ref, w_ref, yi_ref, s_ref, uf_scr = refs[nblk:]
    npair = uf_scr.shape[0]
    gw = S5_CHUNK * h
    gpb = LANES // h
    for blk, u_ref in enumerate(u_refs):
        for l in range(S5_CHUNK):
            piece = u_ref[pl.ds(l, tr, stride=S5_CHUNK), :]
            for gi in range(gpb):
                g = blk * gpb + gi
                col = _flat_col(g, l, h)
                uf_scr[g // 2, :, col:col + h] = piece[:, gi * h:(gi + 1) * h]
    for p in range(npair):
        ub = uf_scr[p].astype(BF16)
        s = _dot(ub, w_ref[p])
        for k in range(4):
            s_ref[k, pl.ds(p, tr, stride=npair), :] = s[:, k * LANES:(k + 1) * LANES]
        yi_ref[p] = jnp.concatenate([_dot(ub[:, a * gw:(a + 1) * gw], t_ref[p, a]) for a in range(2)], axis=1)


def _s5_in(u2d, toep, w_in_pair, h):
    tokens, w5 = u2d.shape
    npair, _, gw, _ = toep.shape
    nblk = w5 // LANES
    r = tokens // S5_CHUNK
    tr = min(r, 64)
    const = lambda a: pl.BlockSpec(a.shape, lambda i: (0,) * a.ndim, pipeline_mode=pl.Buffered(1))
    return pl.pallas_call(
        functools.partial(_s5_in_kernel, tr=tr, h=h, nblk=nblk),
        grid=(r // tr,),
        in_specs=[pl.BlockSpec((tr * S5_CHUNK, LANES), functools.partial(lambda i, blk: (i, blk), blk=blk))
                  for blk in range(nblk)] + [const(toep), const(w_in_pair)],
        out_specs=[
            pl.BlockSpec((npair, tr, 2 * gw), lambda i: (0, i, 0)),
            pl.BlockSpec((4, tr * npair, LANES), lambda i: (0, i, 0)),
        ],
        out_shape=[
            jax.ShapeDtypeStruct((npair, r, 2 * gw), F32),
            jax.ShapeDtypeStruct((4, r * npair, LANES), F32),
        ],
        scratch_shapes=[pltpu.VMEM((npair, tr, 2 * gw), F32)],
        compiler_params=_cparams("arbitrary"),
        name="s5_in",
    )(*([u2d] * nblk), toep, w_in_pair)


def _s5_scan_kernel(s_ref, s0_ref, a_ref, xp_ref, xf_ref, *, nc):
    far, fai, bar, bai = a_ref[0], a_ref[1], a_ref[2], a_ref[3]

    def body(i, carry):
        fr, fi, br, bi = carry
        ib = nc - 1 - i
        xp_ref[0, 0, i] = fr
        xp_ref[1, 0, i] = fi
        xp_ref[2, 0, ib] = br
        xp_ref[3, 0, ib] = bi
        nfr = far * fr - fai * fi + s_ref[0, 0, i]
        nfi = far * fi + fai * fr + s_ref[1, 0, i]
        nbr = bar * br - bai * bi + s_ref[2, 0, ib]
        nbi = bar * bi + bai * br + s_ref[3, 0, ib]
        return nfr, nfi, nbr, nbi

    init = (s0_ref[0, 0], s0_ref[1, 0], s0_ref[2, 0], s0_ref[3, 0])
    fr, fi, br, bi = lax.fori_loop(0, nc, body, init)
    xf_ref[0, 0] = fr
    xf_ref[1, 0] = fi
    xf_ref[2, 0] = br
    xf_ref[3, 0] = bi


def _s5_scan(s_loc, s0, a_pow):
    _, bsz, nc, npair, _ = s_loc.shape
    return pl.pallas_call(
        functools.partial(_s5_scan_kernel, nc=nc),
        grid=(bsz,),
        in_specs=[
            pl.BlockSpec((4, 1, nc, npair, LANES), lambda b: (0, b, 0, 0, 0)),
            pl.BlockSpec((4, 1, npair, LANES), lambda b: (0, b, 0, 0)),
            pl.BlockSpec((4, npair, LANES), lambda b: (0, 0, 0)),
        ],
        out_specs=[
            pl.BlockSpec((4, 1, nc, npair, LANES), lambda b: (0, b, 0, 0, 0)),
            pl.BlockSpec((4, 1, npair, LANES), lambda b: (0, b, 0, 0)),
        ],
        out_shape=[
            jax.ShapeDtypeStruct(s_loc.shape, F32),
            jax.ShapeDtypeStruct((4, bsz, npair, LANES), F32),
        ],
        compiler_params=_cparams("arbitrary"),
        name="s5_scan",
    )(s_loc, s0, a_pow)


def _s5_out_kernel(*refs, tr, h, nblk):
    yi_ref, xp_ref, w_ref = refs[:3]
    y_refs = refs[3:3 + nblk]
    yf_scr, row_scr = refs[3 + nblk:]
    npair = yi_ref.shape[0]
    gpb = LANES // h
    for p in range(npair):
        xcat = jnp.concatenate([xp_ref[k, pl.ds(p, tr, stride=npair), :] for k in range(4)], axis=1)
        yf_scr[p] = yi_ref[p] + _dot(xcat.astype(BF16), w_ref[p])
    for blk, y_ref in enumerate(y_refs):
        for l in range(S5_CHUNK):
            slot = (blk * S5_CHUNK + l) % row_scr.shape[0]
            for gi in range(gpb):
                g = blk * gpb + gi
                col = _flat_col(g, l, h)
                row_scr[slot, :, gi * h:(gi + 1) * h] = yf_scr[g // 2, :, col:col + h]
            y_ref[pl.ds(l, tr, stride=S5_CHUNK), :] = row_scr[slot]


def _s5_out(y_intra, x_prev, w_out_pair, h):
    npair, r, width = y_intra.shape
    nblk = npair * 2 * h // LANES
    tr = min(r, 64)
    return pl.pallas_call(
        functools.partial(_s5_out_kernel, tr=tr, h=h, nblk=nblk),
        grid=(r // tr,),
        in_specs=[
            pl.BlockSpec((npair, tr, width), lambda i: (0, i, 0)),
            pl.BlockSpec((4, tr * npair, LANES), lambda i: (0, i, 0)),
            pl.BlockSpec(w_out_pair.shape, lambda i: (0, 0, 0), pipeline_mode=pl.Buffered(1)),
        ],
        out_specs=[pl.BlockSpec((tr * S5_CHUNK, LANES), lambda i: (i, 0))] * nblk,
        out_shape=[jax.ShapeDtypeStruct((r * S5_CHUNK, LANES), F32)] * nblk,
        scratch_shapes=[pltpu.VMEM((npair, tr, width), F32), pltpu.VMEM((4, tr, LANES), F32)],
        compiler_params=_cparams("arbitrary"),
        name="s5_out",
    )(y_intra, x_prev, w_out_pair)


def _s5_filters(lam_re, lam_im, log_dt, b_re, b_im, c_re, c_im):
    hp = lax.Precision.HIGHEST
    ndir, g, p = lam_re.shape
    h = b_re.shape[-1]
    ell = S5_CHUNK
    npair = g // 2
    lr, li = lam_re.astype(F32), lam_im.astype(F32)
    dt = jnp.exp(log_dt.astype(F32))[..., None]
    mag, ang = jnp.exp(lr * dt), li * dt
    abar_re, abar_im = mag * jnp.cos(ang), mag * jnp.sin(ang)
    den = lr * lr + li * li
    nr, ni = abar_re - 1.0, abar_im
    coef_re = ((nr * lr + ni * li) / den)[..., None]
    coef_im = ((ni * lr - nr * li) / den)[..., None]
    bb_re = coef_re * b_re - coef_im * b_im
    bb_im = coef_re * b_im + coef_im * b_re
    tau = jnp.arange(ell + 1, dtype=F32)[:, None, None, None]
    pmag, pang = jnp.exp(tau * (lr * dt)), tau * (li * dt)
    pw_re, pw_im = pmag * jnp.cos(pang), pmag * jnp.sin(pang)
    ca_re = c_re[None] * pw_re[:, :, :, None, :] - c_im[None] * pw_im[:, :, :, None, :]
    ca_im = c_re[None] * pw_im[:, :, :, None, :] + c_im[None] * pw_re[:, :, :, None, :]
    kern = (jnp.einsum('dgpk,tdghp->dgkth', bb_re, ca_re[:ell], precision=hp)
            - jnp.einsum('dgpk,tdghp->dgkth', bb_im, ca_im[:ell], precision=hp))
    toeps, w_ins, w_outs = [], [], []
    for d in range(2):
        lagrow = (kern[d] if d == 0 else kern[d, :, :, ::-1]).reshape(g, h, ell * h)
        pad = (ell - 1) * h
        lagrow = jnp.pad(lagrow, ((0, 0), (0, 0), (pad, 0) if d == 0 else (0, pad)))
        starts = [pad - lp * h if d == 0 else (ell - 1 - lp) * h for lp in range(ell)]
        toep = jnp.stack([lagrow[:, :, st:st + ell * h] for st in starts], axis=1)
        toeps.append(toep.reshape(npair, 2, ell * h, ell * h))
        pin_re, pin_im = [(pw[:ell, d][::-1] if d == 0 else pw[:ell, d])[..., None] for pw in (pw_re, pw_im)]
        wr = pin_re * bb_re[d][None] - pin_im * bb_im[d][None]
        wi = pin_re * bb_im[d][None] + pin_im * bb_re[d][None]
        w_ins.append([w.transpose(1, 0, 3, 2).reshape(g, ell * h, p) for w in (wr, wi)])
        orr, oii = [(ca[1:, d] if d == 0 else ca[1:, d][::-1]) for ca in (ca_re, -ca_im)]
        w_outs.append([w.transpose(1, 3, 0, 2).reshape(g, p, ell * h) for w in (orr, oii)])
    eye2 = jnp.eye(2, dtype=F32)

    def pair_cols(w):
        gg, rr, cc = w.shape
        w = w.reshape(npair, 2, rr, cc)
        return jnp.einsum('parc,ab->parbc', w, eye2).reshape(npair, 2 * rr, 2 * cc)

    w_in_pair = jnp.concatenate([pair_cols(w_ins[d][k]) for d in range(2) for k in range(2)], axis=-1)
    w_out_pair = jnp.concatenate([pair_cols(w_outs[d][k]) for d in range(2) for k in range(2)], axis=1)
    toep = toeps[0] + toeps[1]
    a_pow = jnp.stack([pw[ell, d].reshape(npair, 2 * p) for d in range(2) for pw in (pw_re, pw_im)])
    return toep.astype(BF16), w_in_pair.astype(BF16), w_out_pair.astype(BF16), a_pow


def _s5_mixer(u, filters, s0, with_out):
    toep, w_in_pair, w_out_pair, a_pow = filters
    bsz, n, w5 = u.shape
    npair = toep.shape[0]
    h = w5 // (2 * npair)
    nc = n // S5_CHUNK
    y_intra, s_loc = _s5_in(u.reshape(bsz * n, w5), toep, w_in_pair, h)
    x_prev, x_fin = _s5_scan(s_loc.reshape(4, bsz, nc, npair, LANES), s0, a_pow)
    if not with_out:
        return None, x_fin
    y_blocks = _s5_out(y_intra, x_prev.reshape(4, bsz * nc * npair, LANES), w_out_pair, h)
    return [y.reshape(bsz, n, LANES) for y in y_blocks], x_fin


def _gla_direction(fr, q, v, lb, s_ref, o_ref, *, blk, reverse, heads):
    n = blk // HG_CHUNK
    width = heads * LANES
    f = lb + (1.0 - lb) * jax.nn.sigmoid(fr)
    k = 1.0 - f
    hi, mid, lo = _split3(jnp.log(f))
    row = lax.broadcasted_iota(jnp.int32, (blk, blk), 0)
    col = lax.broadcasted_iota(jnp.int32, (blk, blk), 1)
    tri = jnp.where((col >= row) if reverse else (col <= row), 1.0, 0.0).astype(BF16)
    c = _dot(tri, hi) + _dot(tri, mid) + _dot(tri, lo)

    def rows(i):
        return slice(blk - (i + 1) * HG_CHUNK, blk - i * HG_CHUNK) if reverse else slice(i * HG_CHUNK, (i + 1) * HG_CHUNK)

    def mem_order(chunks):
        return sorted(chunks, reverse=reverse)

    r = [jnp.zeros((1, width), F32)]
    for i in range(n):
        edge = rows(i).start if reverse else rows(i).stop - 1
        r.append(c[edge:edge + 1])

    def per_chunk(vals):
        return jnp.concatenate([jnp.broadcast_to(vals[i], (HG_CHUNK, width)) for i in mem_order(range(n))], axis=0)

    qs = q * jnp.exp(c - per_chunk(r[:n]))
    kdl = k * jnp.exp(per_chunk(r[1:]) - c)
    qi = (qs * per_chunk([jnp.exp(r[i]) for i in range(n)])).astype(BF16)
    kd = (kdl * per_chunk([jnp.exp(r[n] - r[i + 1]) for i in range(n)])).astype(BF16)
    vb = v.astype(BF16)
    kdlb = kdl.astype(BF16)
    hop = {(i, j): jnp.exp(r[i] - r[j + 1]) for j in range(n) for i in range(j, n)}
    dparts = _split3(jnp.concatenate([r[n], jnp.zeros((7, width), F32)], axis=0))
    ones8 = jnp.ones((8, LANES), BF16)
    lr = lax.broadcasted_iota(jnp.int32, (HG_CHUNK, HG_CHUNK), 0)
    lc = lax.broadcasted_iota(jnp.int32, (HG_CHUNK, HG_CHUNK), 1)
    causal = (lc >= lr) if reverse else (lc <= lr)

    for hd in range(heads):
        sl = slice(hd * LANES, (hd + 1) * LANES)
        state = s_ref[hd]
        dcol = sum(_dot_t0(p[:, sl], ones8) for p in dparts)
        o_inter = _dot(qi[:, sl], state.astype(BF16))
        s_ref[hd] = jnp.exp(dcol) * state + _dot_t0(kd[:, sl], vb[:, sl])
        acc = {i: o_inter[rows(i)] for i in range(n)}
        for j in range(n):
            queries = mem_order(range(j, n))
            lhs = jnp.concatenate([qs[rows(i), sl] * hop[i, j][:, sl] for i in queries], axis=0).astype(BF16)
            att = _dot_t1(lhs, kdlb[rows(j), sl])
            pieces = [att[a * HG_CHUNK:(a + 1) * HG_CHUNK] for a in range(len(queries))]
            dpos = queries.index(j)
            pieces[dpos] = jnp.where(causal, pieces[dpos], 0.0)
            o_j = _dot(jnp.concatenate(pieces, axis=0).astype(BF16), vb[rows(j), sl])
            for a, i in enumerate(queries):
                acc[i] = acc[i] + o_j[a * HG_CHUNK:(a + 1) * HG_CHUNK]
        o_ref[0, :, sl] = jnp.concatenate([acc[i] for i in mem_order(range(n))], axis=0)


def _gla_kernel(ff_ref, fb_ref, vf_ref, vb_ref, qf_ref, qb_ref, lb_ref, s0f_ref, s0b_ref,
                of_ref, ob_ref, sff_ref, sfb_ref, s_scr, *, blk, heads):
    j = pl.program_id(1)
    last = pl.num_programs(1) - 1

    @pl.when(j == 0)
    def _():
        s_scr[0] = s0f_ref[0]
        s_scr[1] = s0b_ref[0]

    _gla_direction(ff_ref[0], qf_ref[0], vf_ref[0], lb_ref[0:1], s_scr.at[0], of_ref, blk=blk, reverse=False, heads=heads)
    _gla_direction(fb_ref[0], qb_ref[0], vb_ref[0], lb_ref[1:2], s_scr.at[1], ob_ref, blk=blk, reverse=True, heads=heads)

    @pl.when(j == last)
    def _():
        sff_ref[0] = s_scr[0]
        sfb_ref[0] = s_scr[1]


def _gla(fraw, v, q, lb, s0f, s0b):
    bsz, n, hgw = v.shape
    heads = hgw // LANES
    blk = min(n, 256)
    nblk = n // blk
    fwd = lambda c: pl.BlockSpec((1, blk, hgw), lambda b, j: (b, j, c))
    bwd = lambda c: pl.BlockSpec((1, blk, hgw), lambda b, j: (b, nblk - 1 - j, c))
    st = pl.BlockSpec((1, heads, LANES, LANES), lambda b, j: (b, 0, 0, 0))
    return pl.pallas_call(
        functools.partial(_gla_kernel, blk=blk, heads=heads),
        grid=(bsz, nblk),
        in_specs=[fwd(0), bwd(1), fwd(0), bwd(0), fwd(0), bwd(0),
                  pl.BlockSpec((2, hgw), lambda b, j: (0, 0)), st, st],
        out_specs=[fwd(0), bwd(0), st, st],
        out_shape=[jax.ShapeDtypeStruct((bsz, n, hgw), F32)] * 2
        + [jax.ShapeDtypeStruct((bsz, heads, LANES, LANES), F32)] * 2,
        scratch_shapes=[pltpu.VMEM((2, heads, LANES, LANES), F32)],
        compiler_params=_cparams("arbitrary", "arbitrary"),
        name="gla",
    )(fraw, fraw, v, v, q, q, lb, s0f, s0b)


def _post_kernel(*refs, alpha, w5, heads, nblk):
    y_refs = refs[:nblk]
    (x_ref, u_ref, of_ref, ob_ref, g_ref, gate_ref, d_ref, wg_ref, bg_ref, nw_ref,
     wo_ref, lg_ref, lbias_ref, o_ref) = refs[nblk:]
    y5 = jnp.concatenate([y_ref[0] for y_ref in y_refs], axis=1)
    s5_y = jax.nn.gelu(y5 + u_ref[0] * d_ref[...])
    s5_out = s5_y * jax.nn.sigmoid(_dot(s5_y.astype(BF16), wg_ref[...]) + bg_ref[...])
    proj = _dot(s5_out.astype(BF16), wo_ref[0:w5, :])
    o = of_ref[0] + ob_ref[0]
    gate = g_ref[0]
    nw = nw_ref[...]
    for hd in range(heads):
        sl = slice(hd * LANES, (hd + 1) * LANES)
        oh = o[:, sl]
        ms = jnp.mean(oh * oh, axis=-1, keepdims=True)
        hg = (oh * lax.rsqrt(ms + RMS_EPS) * nw * gate[:, sl]).astype(BF16)
        proj = proj + _dot(hg, wo_ref[w5 + hd * LANES:w5 + (hd + 1) * LANES, :])
    z = alpha * x_ref[0] + gate_ref[0] * proj
    o_ref[0] = _layer_norm(z, lg_ref[...], lbias_ref[...])


def _post(x, u, y5, o_f, o_b, g, gate, d_skip, w_glu, b_glu, norm_w, w_out, ln_g, ln_b, alpha):
    bsz, n, d = x.shape
    w5 = u.shape[-1]
    hgw = g.shape[-1]
    tb = min(n, 512)
    tok = lambda c: pl.BlockSpec((1, tb, c), lambda b, j: (b, j, 0))
    full = lambda a: pl.BlockSpec(a.shape, lambda b, j: (0,) * a.ndim)
    consts = [d_skip.reshape(1, w5), w_glu, b_glu.reshape(1, w5), norm_w.reshape(1, LANES), w_out,
              ln_g.reshape(1, d), ln_b.reshape(1, d)]
    nblk = len(y5)
    return pl.pallas_call(
        functools.partial(_post_kernel, alpha=alpha, w5=w5, heads=hgw // LANES, nblk=nblk),
        grid=(bsz, n // tb),
        in_specs=[tok(LANES)] * nblk + [tok(d), tok(w5), tok(hgw), tok(hgw), tok(hgw),
                                        pl.BlockSpec((1, 1, d), lambda b, j: (b, 0, 0))] + [full(a) for a in consts],
        out_specs=tok(d),
        out_shape=jax.ShapeDtypeStruct((bsz, n, d), F32),
        compiler_params=_cparams("arbitrary", "arbitrary"),
        name="mixer_post",
    )(*y5, x, u, o_f, o_b, g, gate, *consts)


def _ffn_kernel(x_ref, sh_ref, sc_ref, gate_ref, wu_ref, cw_ref, cb_ref, wd_ref, lg_ref, lbias_ref, o_ref, act_scr,
                *, alpha, row_w, tb, tf, dff):
    x = x_ref[0]
    h = (x * (1.0 + sc_ref[0]) + sh_ref[0]).astype(BF16)
    pos = lax.broadcasted_iota(jnp.int32, (tb, 1), 0) % row_w
    has_prev = pos != 0
    has_next = pos != row_w - 1

    def conv(up, c0):
        prev = jnp.where(has_prev, pltpu.roll(up, 1, 0), 0.0)
        nxt = jnp.where(has_next, pltpu.roll(up, tb - 1, 0), 0.0)
        cols = slice(c0, c0 + tf)
        return prev * cw_ref[0:1, cols] + up * cw_ref[1:2, cols] + nxt * cw_ref[2:3, cols] + cb_ref[:, cols]

    for t in range(dff // tf):
        a = conv(_dot(h, wu_ref[:, t * tf:(t + 1) * tf]), t * tf)
        g = conv(_dot(h, wu_ref[:, dff + t * tf:dff + (t + 1) * tf]), dff + t * tf)
        act_scr[:, t * tf:(t + 1) * tf] = (_silu(a) * g).astype(BF16)
    z = alpha * x + gate_ref[0] * _dot(act_scr[...], wd_ref[...])
    o_ref[0] = _layer_norm(z, lg_ref[...], lbias_ref[...])


def _conv_ffn(x, shift, scale, gate, w_up, conv_w, conv_b, w_down, ln_g, ln_b, alpha, row_w):
    bsz, n, d = x.shape
    dff = w_down.shape[0]
    tf = 256
    tb = min(n, 512)
    tok = pl.BlockSpec((1, tb, d), lambda b, i: (b, i, 0))
    vec = pl.BlockSpec((1, 1, d), lambda b, i: (b, 0, 0))
    const = lambda a: pl.BlockSpec(a.shape, lambda b, i: (0,) * a.ndim, pipeline_mode=pl.Buffered(1))
    consts = [w_up, conv_w, conv_b.reshape(1, 2 * dff), w_down, ln_g.reshape(1, d), ln_b.reshape(1, d)]
    return pl.pallas_call(
        functools.partial(_ffn_kernel, alpha=alpha, row_w=row_w, tb=tb, tf=tf, dff=dff),
        grid=(bsz, n // tb),
        in_specs=[tok, vec, vec, vec] + [const(a) for a in consts],
        out_specs=tok,
        out_shape=jax.ShapeDtypeStruct((bsz, n, d), F32),
        scratch_shapes=[pltpu.VMEM((tb, dff), BF16)],
        compiler_params=_cparams("arbitrary", "arbitrary"),
        name="conv_ffn",
    )(x, shift, scale, gate, *consts)


def _token_mixer(h_in, shift, scale, w_in, filters, lb, init, with_out, w5, hg):
    u, fraw, v, q, g = _inproj(h_in, shift, scale, w_in, w5, hg)
    s5_init, hg_init = init
    y5, s5_fin = _s5_mixer(u, filters, s5_init, with_out)
    o_f, o_b, hg_fin_f, hg_fin_b = _gla(fraw, v, q, lb, hg_init[0], hg_init[1])
    return (u, y5, o_f, o_b, g), (s5_fin, (hg_fin_f, hg_fin_b))


def kernel(x, c, ctx, c_ctx, w_mod, b_mod, w_in, s5_lam_re, s5_lam_im, s5_log_dt, s5_b_re, s5_b_im,
           s5_c_re, s5_c_im, s5_d, w_glu, b_glu, hg_lb, hg_norm_w, w_out, ln1_g, ln1_b,
           w_up, conv_w, conv_b, w_down, ln2_g, ln2_b):
    depth = w_mod.shape[0]
    bsz, n, d = x.shape
    n_ctx = ctx.shape[1]
    w5 = s5_d.shape[-1]
    hg = hg_lb.shape[-1]
    heads = hg // LANES
    npair = s5_lam_re.shape[2] // 2
    alpha = (2 * depth) ** 0.25

    lb_all = jnp.cumsum(jax.nn.softmax(hg_lb.astype(F32), axis=0), axis=0)
    lb_all = lb_all - lb_all[:1]

    rb = -(-(bsz + 1) // 8) * 8
    rows = jnp.concatenate([c, c_ctx[None], jnp.zeros((rb - bsz - 1, d), F32)], axis=0)
    mods = _mod_vectors(rows, w_mod, b_mod)

    w_in_b, w_glu_b, w_out_b = w_in.astype(BF16), w_glu.astype(BF16), w_out.astype(BF16)
    w_up_b, w_down_b = w_up.astype(BF16), w_down.astype(BF16)

    zero_init = (jnp.zeros((4, bsz, npair, LANES), F32),
                 (jnp.zeros((bsz, heads, LANES, LANES), F32),) * 2)

    filters_all = jax.vmap(_s5_filters)(s5_lam_re, s5_lam_im, s5_log_dt, s5_b_re, s5_b_im, s5_c_re, s5_c_im)

    for l in range(depth):
        last = l == depth - 1
        mx = [mods[l, :bsz, i * d:(i + 1) * d][:, None, :] for i in range(6)]
        mc = [jnp.broadcast_to(mods[l, bsz, i * d:(i + 1) * d], (bsz, 1, d)) for i in range(6)]
        filters = [a[l] for a in filters_all]
        lb = lb_all[l]
        c_parts, ctx_states = _token_mixer(ctx, mc[0], mc[1], w_in_b[l], filters, lb, zero_init, not last, w5, hg)
        x_parts, _ = _token_mixer(x, mx[0], mx[1], w_in_b[l], filters, lb, ctx_states, True, w5, hg)
        post = functools.partial(_post, d_skip=s5_d[l], w_glu=w_glu_b[l], b_glu=b_glu[l], norm_w=hg_norm_w[l],
                                 w_out=w_out_b[l], ln_g=ln1_g[l], ln_b=ln1_b[l], alpha=alpha)
        ffn = functools.partial(_conv_ffn, w_up=w_up_b[l], conv_w=conv_w[l], conv_b=conv_b[l], w_down=w_down_b[l],
                                ln_g=ln2_g[l], ln_b=ln2_b[l], alpha=alpha)
        x = post(x, *x_parts, gate=mx[2])
        x = ffn(x, mx[3], mx[4], mx[5], row_w=GRID_W)
        if not last:
            ctx = post(ctx, *c_parts, gate=mc[2])
            ctx = ffn(ctx, mc[3], mc[4], mc[5], row_w=n_ctx)
    return x
```

```python
import functools

import jax
import jax.numpy as jnp
from jax import lax
from jax.experimental import pallas as pl
from jax.experimental.pallas import tpu as pltpu

F32 = jnp.float32
BF16 = jnp.bfloat16

GRID_W = 64
HG_CHUNK = 32
S5_CHUNK = 16
LN_EPS = 1e-5
RMS_EPS = 1e-6
LANES = 128
SUBLANES = 8
VMEM_LIMIT = 56 * 1024 * 1024


def _cparams(*sem):
    return pltpu.CompilerParams(dimension_semantics=sem, vmem_limit_bytes=VMEM_LIMIT)


def _silu(x):
    return x * jax.nn.sigmoid(x)


def _dot(a, b):
    return jnp.dot(a, b, preferred_element_type=F32)


def _dot_t0(a, b):
    return lax.dot_general(a, b, (((0,), (0,)), ((), ())), preferred_element_type=F32)


def _dot_t1(a, b):
    return lax.dot_general(a, b, (((1,), (1,)), ((), ())), preferred_element_type=F32)


def _split(x, parts):
    out = []
    for _ in range(parts - 1):
        piece = x.astype(BF16)
        out.append(piece)
        x = x - piece.astype(F32)
    out.append(x.astype(BF16))
    return out


def _hp_dot_t1(a, b):
    ah, al = _split(a, 2)
    bh, bl = _split(b, 2)
    return _dot_t1(ah, bh) + _dot_t1(ah, bl) + _dot_t1(al, bh)


def _layer_norm(z, g, b):
    mu = jnp.mean(z, axis=-1, keepdims=True)
    zc = z - mu
    var = jnp.mean(zc * zc, axis=-1, keepdims=True)
    return zc * lax.rsqrt(var + LN_EPS) * g + b


def _layer_spec(a, l, ngrid, single=False):
    mode = dict(pipeline_mode=pl.Buffered(1)) if single else {}
    return pl.BlockSpec((None,) + a.shape[1:], lambda *_: (l,) + (0,) * (a.ndim - 1), **mode)


def _mod_spec(mods, l, chunk, d):
    return pl.BlockSpec((None, mods.shape[1], d), lambda *_: (l, 0, chunk))


def _mod_row(m_ref, ctx_row):
    row = pl.program_id(0) if ctx_row is None else ctx_row
    return m_ref[pl.ds(row, 1), :]


def _mod_kernel(c_ref, w_ref, b_ref, o_ref):
    s = _silu(c_ref[...]).astype(BF16)
    o_ref[0] = _dot(s, w_ref[0].astype(BF16)) + b_ref[0]


def _mod_vectors(rows, w_mod, b_mod):
    depth, d, d6 = w_mod.shape
    rb = rows.shape[0]
    tn = 1536 if d6 % 1536 == 0 else d6
    return pl.pallas_call(
        _mod_kernel,
        grid=(depth, d6 // tn),
        in_specs=[
            pl.BlockSpec((rb, d), lambda l, j: (0, 0)),
            pl.BlockSpec((1, d, tn), lambda l, j: (l, 0, j)),
            pl.BlockSpec((1, 1, tn), lambda l, j: (l, 0, j)),
        ],
        out_specs=pl.BlockSpec((1, rb, tn), lambda l, j: (l, 0, j)),
        out_shape=jax.ShapeDtypeStruct((depth, rb, d6), F32),
        compiler_params=_cparams("arbitrary", "arbitrary"),
        name="mod_vectors",
    )(rows, w_mod, b_mod.reshape(depth, 1, d6))


def _inproj_kernel(x_ref, sh_ref, sc_ref, w_ref, u_ref, f_ref, v_ref, q_ref, g_ref, *, w5, hg, ctx_row):
    h = (x_ref[0] * (1.0 + _mod_row(sc_ref, ctx_row)) + _mod_row(sh_ref, ctx_row)).astype(BF16)
    o = 0
    u_ref[0] = _dot(h, w_ref[:, o:o + w5]); o += w5
    f_ref[0] = _dot(h, w_ref[:, o:o + 2 * hg]); o += 2 * hg
    v_ref[0] = _dot(h, w_ref[:, o:o + hg]).astype(BF16); o += hg
    q_ref[0] = _silu(_dot(h, w_ref[:, o:o + hg])).astype(BF16); o += hg
    g_ref[0] = _silu(_dot(h, w_ref[:, o:o + hg])).astype(BF16)


def _inproj(x, mods, w_in, l, w5, hg, ctx_row):
    bsz, n, d = x.shape
    tb = min(n, 512)
    tok = lambda c: pl.BlockSpec((1, tb, c), lambda b, j: (b, j, 0))
    return pl.pallas_call(
        functools.partial(_inproj_kernel, w5=w5, hg=hg, ctx_row=ctx_row),
        grid=(bsz, n // tb),
        in_specs=[tok(d), _mod_spec(mods, l, 0, d), _mod_spec(mods, l, 1, d), _layer_spec(w_in, l, 2)],
        out_specs=[tok(w5), tok(2 * hg), tok(hg), tok(hg), tok(hg)],
        out_shape=[jax.ShapeDtypeStruct((bsz, n, c), t)
                   for c, t in ((w5, F32), (2 * hg, F32), (hg, BF16), (hg, BF16), (hg, BF16))],
        compiler_params=_cparams("arbitrary", "arbitrary"),
        name="inproj",
    )(x, mods, mods, w_in)


def _flat_col(g, l, h):
    return (g % 2) * S5_CHUNK * h + l * h


def _toeplitz_kernel(bbt_ref, caf_ref, cab_ref, t_ref, *, h):
    gw = S5_CHUNK * h
    lane = lax.broadcasted_iota(jnp.int32, (h, gw), 1)
    for a in range(2):
        rf = _hp_dot_t1(bbt_ref[0, a], caf_ref[a])
        rb = _hp_dot_t1(bbt_ref[1, a], cab_ref[a])
        for lp in range(S5_CHUNK):
            sf, sb = lp * h, (S5_CHUNK - 1 - lp) * h
            fwd = rf if sf == 0 else jnp.where(lane >= sf, pltpu.roll(rf, sf, 1), 0.0)
            bwd = rb if sb == 0 else jnp.where(lane < gw - sb, pltpu.roll(rb, gw - sb, 1), 0.0)
            t_ref[a, lp * h:(lp + 1) * h, :] = (fwd + bwd).astype(BF16)


def _s5_toeplitz(bbt, caf, cab):
    depth, _, g, h, p2 = bbt.shape
    gw = S5_CHUNK * h
    npair = g // 2
    return pl.pallas_call(
        functools.partial(_toeplitz_kernel, h=h),
        grid=(depth, npair),
        in_specs=[
            pl.BlockSpec((None, 2, 2, h, p2), lambda l, p: (l, 0, p, 0, 0)),
            pl.BlockSpec((None, 2, gw, p2), lambda l, p: (l, p, 0, 0)),
            pl.BlockSpec((None, 2, gw, p2), lambda l, p: (l, p, 0, 0)),
        ],
        out_specs=pl.BlockSpec((None, None, 2, gw, gw), lambda l, p: (l, p, 0, 0, 0)),
        out_shape=jax.ShapeDtypeStruct((depth, npair, 2, gw, gw), BF16),
        compiler_params=_cparams("arbitrary", "arbitrary"),
        name="s5_toeplitz",
    )(bbt, caf, cab)


def _s5_in_kernel(*refs, tr, h, nblk):
    u_refs = refs[:nblk]
    t_ref, w_ref, yi_ref, s_ref, uf_scr = refs[nblk:]
    npair = uf_scr.shape[0]
    gw = S5_CHUNK * h
    gpb = LANES // h
    for blk, u_ref in enumerate(u_refs):
        for l in range(S5_CHUNK):
            piece = u_ref[pl.ds(l, tr, stride=S5_CHUNK), :]
            for gi in range(gpb):
                g = blk * gpb + gi
                col = _flat_col(g, l, h)
                uf_scr[g // 2, :, col:col + h] = piece[:, gi * h:(gi + 1) * h]
    for p in range(npair):
        ub = uf_scr[p].astype(BF16)
        s = _dot(ub, w_ref[p])
        for k in range(4):
            s_ref[k, pl.ds(p, tr, stride=npair), :] = s[:, k * LANES:(k + 1) * LANES]
        yi_ref[p] = jnp.concatenate([_dot(ub[:, a * gw:(a + 1) * gw], t_ref[p, a]) for a in range(2)], axis=1)


def _s5_in(u2d, toep, w_in_pair, l, h):
    tokens, w5 = u2d.shape
    _, npair, _, gw, _ = toep.shape
    nblk = w5 // LANES
    r = tokens // S5_CHUNK
    tr = min(r, 64)
    return pl.pallas_call(
        functools.partial(_s5_in_kernel, tr=tr, h=h, nblk=nblk),
        grid=(r // tr,),
        in_specs=[pl.BlockSpec((tr * S5_CHUNK, LANES), functools.partial(lambda i, blk: (i, blk), blk=blk))
                  for blk in range(nblk)] + [_layer_spec(toep, l, 1, True), _layer_spec(w_in_pair, l, 1, True)],
        out_specs=[
            pl.BlockSpec((npair, tr, 2 * gw), lambda i: (0, i, 0)),
            pl.BlockSpec((4, tr * npair, LANES), lambda i: (0, i, 0)),
        ],
        out_shape=[
            jax.ShapeDtypeStruct((npair, r, 2 * gw), F32),
            jax.ShapeDtypeStruct((4, r * npair, LANES), F32),
        ],
        scratch_shapes=[pltpu.VMEM((npair, tr, 2 * gw), F32)],
        compiler_params=_cparams("arbitrary"),
        name="s5_in",
    )(*([u2d] * nblk), toep, w_in_pair)


def _s5_scan_kernel(s_ref, s0_ref, a_ref, xp_ref, xf_ref, *, nc):
    far, fai, bar, bai = a_ref[0], a_ref[1], a_ref[2], a_ref[3]

    def body(i, carry):
        fr, fi, br, bi = carry
        ib = nc - 1 - i
        xp_ref[0, 0, i] = fr
        xp_ref[1, 0, i] = fi
        xp_ref[2, 0, ib] = br
        xp_ref[3, 0, ib] = bi
        nfr = far * fr - fai * fi + s_ref[0, 0, i]
        nfi = far * fi + fai * fr + s_ref[1, 0, i]
        nbr = bar * br - bai * bi + s_ref[2, 0, ib]
        nbi = bar * bi + bai * br + s_ref[3, 0, ib]
        return nfr, nfi, nbr, nbi

    init = (s0_ref[0, 0], s0_ref[1, 0], s0_ref[2, 0], s0_ref[3, 0])
    fr, fi, br, bi = lax.fori_loop(0, nc, body, init)
    xf_ref[0, 0] = fr
    xf_ref[1, 0] = fi
    xf_ref[2, 0] = br
    xf_ref[3, 0] = bi


def _s5_scan(s_loc, s0, a_pow, l):
    _, bsz, nc, npair, _ = s_loc.shape
    return pl.pallas_call(
        functools.partial(_s5_scan_kernel, nc=nc),
        grid=(bsz,),
        in_specs=[
            pl.BlockSpec((4, 1, nc, npair, LANES), lambda b: (0, b, 0, 0, 0)),
            pl.BlockSpec((4, 1, npair, LANES), lambda b: (0, b, 0, 0)),
            _layer_spec(a_pow, l, 1),
        ],
        out_specs=[
            pl.BlockSpec((4, 1, nc, npair, LANES), lambda b: (0, b, 0, 0, 0)),
            pl.BlockSpec((4, 1, npair, LANES), lambda b: (0, b, 0, 0)),
        ],
        out_shape=[
            jax.ShapeDtypeStruct(s_loc.shape, F32),
            jax.ShapeDtypeStruct((4, bsz, npair, LANES), F32),
        ],
        compiler_params=_cparams("arbitrary"),
        name="s5_scan",
    )(s_loc, s0, a_pow)


def _s5_out_kernel(*refs, tr, h, nblk):
    yi_ref, xp_ref, w_ref = refs[:3]
    y_refs = refs[3:3 + nblk]
    yf_scr, row_scr = refs[3 + nblk:]
    npair = yi_ref.shape[0]
    gpb = LANES // h
    for p in range(npair):
        xcat = jnp.concatenate([xp_ref[k, pl.ds(p, tr, stride=npair), :] for k in range(4)], axis=1)
        yf_scr[p] = yi_ref[p] + _dot(xcat.astype(BF16), w_ref[p])
    for blk, y_ref in enumerate(y_refs):
        for l in range(S5_CHUNK):
            slot = (blk * S5_CHUNK + l) % row_scr.shape[0]
            for gi in range(gpb):
                g = blk * gpb + gi
                col = _flat_col(g, l, h)
                row_scr[slot, :, gi * h:(gi + 1) * h] = yf_scr[g // 2, :, col:col + h]
            y_ref[pl.ds(l, tr, stride=S5_CHUNK), :] = row_scr[slot]


def _s5_out(y_intra, x_prev, w_out_pair, l, h):
    npair, r, width = y_intra.shape
    nblk = npair * 2 * h // LANES
    tr = min(r, 64)
    return pl.pallas_call(
        functools.partial(_s5_out_kernel, tr=tr, h=h, nblk=nblk),
        grid=(r // tr,),
        in_specs=[
            pl.BlockSpec((npair, tr, width), lambda i: (0, i, 0)),
            pl.BlockSpec((4, tr * npair, LANES), lambda i: (0, i, 0)),
            _layer_spec(w_out_pair, l, 1, True),
        ],
        out_specs=[pl.BlockSpec((tr * S5_CHUNK, LANES), lambda i: (i, 0))] * nblk,
        out_shape=[jax.ShapeDtypeStruct((r * S5_CHUNK, LANES), F32)] * nblk,
        scratch_shapes=[pltpu.VMEM((npair, tr, width), F32), pltpu.VMEM((4, tr, LANES), F32)],
        compiler_params=_cparams("arbitrary"),
        name="s5_out",
    )(y_intra, x_prev, w_out_pair)


def _s5_filter_inputs(lam_re, lam_im, log_dt, b_re, b_im, c_re, c_im):
    ndir, g, p = lam_re.shape
    h = b_re.shape[-1]
    ell = S5_CHUNK
    npair = g // 2
    lr, li = lam_re.astype(F32), lam_im.astype(F32)
    dt = jnp.exp(log_dt.astype(F32))[..., None]
    mag, ang = jnp.exp(lr * dt), li * dt
    abar_re, abar_im = mag * jnp.cos(ang), mag * jnp.sin(ang)
    den = lr * lr + li * li
    nr, ni = abar_re - 1.0, abar_im
    coef_re = ((nr * lr + ni * li) / den)[..., None]
    coef_im = ((ni * lr - nr * li) / den)[..., None]
    bb_re = coef_re * b_re - coef_im * b_im
    bb_im = coef_re * b_im + coef_im * b_re
    tau = jnp.arange(ell + 1, dtype=F32)[:, None, None, None]
    pmag, pang = jnp.exp(tau * (lr * dt)), tau * (li * dt)
    pw_re, pw_im = pmag * jnp.cos(pang), pmag * jnp.sin(pang)
    ca_re = c_re[None] * pw_re[:, :, :, None, :] - c_im[None] * pw_im[:, :, :, None, :]
    ca_im = c_re[None] * pw_im[:, :, :, None, :] + c_im[None] * pw_re[:, :, :, None, :]
    bbt = jnp.concatenate([bb_re.transpose(0, 1, 3, 2), -bb_im.transpose(0, 1, 3, 2)], axis=-1)

    def lag_rows(d, descending):
        ca = jnp.concatenate([ca_re[:ell, d], ca_im[:ell, d]], axis=-1)
        ca = ca[::-1] if descending else ca
        return ca.transpose(1, 0, 2, 3).reshape(g, ell * h, 2 * p)

    w_ins, w_outs = [], []
    for d in range(2):
        pin_re, pin_im = [(pw[:ell, d][::-1] if d == 0 else pw[:ell, d])[..., None] for pw in (pw_re, pw_im)]
        wr = pin_re * bb_re[d][None] - pin_im * bb_im[d][None]
        wi = pin_re * bb_im[d][None] + pin_im * bb_re[d][None]
        w_ins.append([w.transpose(1, 0, 3, 2).reshape(g, ell * h, p) for w in (wr, wi)])
        orr, oii = [(ca[1:, d] if d == 0 else ca[1:, d][::-1]) for ca in (ca_re, -ca_im)]
        w_outs.append([w.transpose(1, 3, 0, 2).reshape(g, p, ell * h) for w in (orr, oii)])
    eye2 = jnp.eye(2, dtype=F32)

    def pair_cols(w):
        gg, rr, cc = w.shape
        w = w.reshape(npair, 2, rr, cc)
        return jnp.einsum('parc,ab->parbc', w, eye2).reshape(npair, 2 * rr, 2 * cc)

    w_in_pair = jnp.concatenate([pair_cols(w_ins[d][k]) for d in range(2) for k in range(2)], axis=-1)
    w_out_pair = jnp.concatenate([pair_cols(w_outs[d][k]) for d in range(2) for k in range(2)], axis=1)
    a_pow = jnp.stack([pw[ell, d].reshape(npair, 2 * p) for d in range(2) for pw in (pw_re, pw_im)])
    return bbt, lag_rows(0, False), lag_rows(1, True), w_in_pair.astype(BF16), w_out_pair.astype(BF16), a_pow


def _s5_mixer(u, filters, l, s0, with_out):
    toep, w_in_pair, w_out_pair, a_pow = filters
    bsz, n, w5 = u.shape
    npair = toep.shape[1]
    h = w5 // (2 * npair)
    nc = n // S5_CHUNK
    y_intra, s_loc = _s5_in(u.reshape(bsz * n, w5), toep, w_in_pair, l, h)
    x_prev, x_fin = _s5_scan(s_loc.reshape(4, bsz, nc, npair, LANES), s0, a_pow, l)
    if not with_out:
        return None, x_fin
    y_blocks = _s5_out(y_intra, x_prev.reshape(4, bsz * nc * npair, LANES), w_out_pair, l, h)
    return [y.reshape(bsz, n, LANES) for y in y_blocks], x_fin


def _gla_direction(fr, q, v, lb, s_ref, o_ref, *, blk, reverse, heads):
    n = blk // HG_CHUNK
    width = heads * LANES
    f = lb + (1.0 - lb) * jax.nn.sigmoid(fr)
    k = 1.0 - f
    hi, lo = _split(jnp.log(f), 2)
    row = lax.broadcasted_iota(jnp.int32, (blk, blk), 0)
    col = lax.broadcasted_iota(jnp.int32, (blk, blk), 1)
    tri = jnp.where((col >= row) if reverse else (col <= row), 1.0, 0.0).astype(BF16)
    c = _dot(tri, hi) + _dot(tri, lo)

    def rows(i):
        return slice(blk - (i + 1) * HG_CHUNK, blk - i * HG_CHUNK) if reverse else slice(i * HG_CHUNK, (i + 1) * HG_CHUNK)

    def mem_order(chunks):
        return sorted(chunks, reverse=reverse)

    r = [jnp.zeros((1, width), F32)]
    for i in range(n):
        edge = rows(i).start if reverse else rows(i).stop - 1
        r.append(c[edge:edge + 1])

    def per_chunk(vals):
        return jnp.concatenate([jnp.broadcast_to(vals[i], (HG_CHUNK, width)) for i in mem_order(range(n))], axis=0)

    qs = q * jnp.exp(c - per_chunk(r[:n]))
    kdl = k * jnp.exp(per_chunk(r[1:]) - c)
    qi = (qs * per_chunk([jnp.exp(r[i]) for i in range(n)])).astype(BF16)
    kd = (kdl * per_chunk([jnp.exp(r[n] - r[i + 1]) for i in range(n)])).astype(BF16)
    kdlb = kdl.astype(BF16)
    hop = {(i, j): jnp.exp(r[i] - r[j + 1]) for j in range(n) for i in range(j, n)}
    dparts = _split(jnp.concatenate([r[n], jnp.zeros((SUBLANES - 1, width), F32)], axis=0), 3)
    ones8 = jnp.ones((SUBLANES, LANES), BF16)
    lr = lax.broadcasted_iota(jnp.int32, (HG_CHUNK, HG_CHUNK), 0)
    lc = lax.broadcasted_iota(jnp.int32, (HG_CHUNK, HG_CHUNK), 1)
    causal = (lc >= lr) if reverse else (lc <= lr)

    for hd in range(heads):
        sl = slice(hd * LANES, (hd + 1) * LANES)
        state = s_ref[hd]
        dcol = sum(_dot_t0(p[:, sl], ones8) for p in dparts)
        o_inter = _dot(qi[:, sl], state.astype(BF16))
        s_ref[hd] = jnp.exp(dcol) * state + _dot_t0(kd[:, sl], v[:, sl])
        acc = {i: o_inter[rows(i)] for i in range(n)}
        for j in range(n):
            queries = mem_order(range(j, n))
            lhs = jnp.concatenate([qs[rows(i), sl] * hop[i, j][:, sl] for i in queries], axis=0).astype(BF16)
            att = _dot_t1(lhs, kdlb[rows(j), sl])
            pieces = [att[a * HG_CHUNK:(a + 1) * HG_CHUNK] for a in range(len(queries))]
            dpos = queries.index(j)
            pieces[dpos] = jnp.where(causal, pieces[dpos], 0.0)
            o_j = _dot(jnp.concatenate(pieces, axis=0).astype(BF16), v[rows(j), sl])
            for a, i in enumerate(queries):
                acc[i] = acc[i] + o_j[a * HG_CHUNK:(a + 1) * HG_CHUNK]
        o_ref[0, :, sl] = jnp.concatenate([acc[i] for i in mem_order(range(n))], axis=0)


def _gla_kernel(ff_ref, fb_ref, vf_ref, vb_ref, qf_ref, qb_ref, lb_ref, s0f_ref, s0b_ref,
                of_ref, ob_ref, sff_ref, sfb_ref, s_scr, *, blk, heads):
    j = pl.program_id(1)
    last = pl.num_programs(1) - 1

    @pl.when(j == 0)
    def _():
        s_scr[0] = s0f_ref[0]
        s_scr[1] = s0b_ref[0]

    _gla_direction(ff_ref[0], qf_ref[0].astype(F32), vf_ref[0], lb_ref[0:1], s_scr.at[0], of_ref,
                   blk=blk, reverse=False, heads=heads)
    _gla_direction(fb_ref[0], qb_ref[0].astype(F32), vb_ref[0], lb_ref[1:2], s_scr.at[1], ob_ref,
                   blk=blk, reverse=True, heads=heads)

    @pl.when(j == last)
    def _():
        sff_ref[0] = s_scr[0]
        sfb_ref[0] = s_scr[1]


def _gla(fraw, v, q, lb_all, l, s0f, s0b):
    bsz, n, hgw = v.shape
    heads = hgw // LANES
    blk = min(n, 256)
    nblk = n // blk
    fwd = lambda c: pl.BlockSpec((1, blk, hgw), lambda b, j: (b, j, c))
    bwd = lambda c: pl.BlockSpec((1, blk, hgw), lambda b, j: (b, nblk - 1 - j, c))
    st = pl.BlockSpec((1, heads, LANES, LANES), lambda b, j: (b, 0, 0, 0))
    return pl.pallas_call(
        functools.partial(_gla_kernel, blk=blk, heads=heads),
        grid=(bsz, nblk),
        in_specs=[fwd(0), bwd(1), fwd(0), bwd(0), fwd(0), bwd(0), _layer_spec(lb_all, l, 2), st, st],
        out_specs=[fwd(0), bwd(0), st, st],
        out_shape=[jax.ShapeDtypeStruct((bsz, n, hgw), F32)] * 2
        + [jax.ShapeDtypeStruct((bsz, heads, LANES, LANES), F32)] * 2,
        scratch_shapes=[pltpu.VMEM((2, heads, LANES, LANES), F32)],
        compiler_params=_cparams("arbitrary", "arbitrary"),
        name="gla",
    )(fraw, fraw, v, v, q, q, lb_all, s0f, s0b)


def _post_kernel(*refs, alpha, w5, heads, nblk, ctx_row):
    y_refs = refs[:nblk]
    (x_ref, u_ref, of_ref, ob_ref, g_ref, gate_ref, d_ref, wg_ref, bg_ref, nw_ref,
     wo_ref, lg_ref, lbias_ref, o_ref) = refs[nblk:]
    y5 = jnp.concatenate([y_ref[0] for y_ref in y_refs], axis=1)
    s5_y = jax.nn.gelu(y5 + u_ref[0] * d_ref[...])
    s5_out = s5_y * jax.nn.sigmoid(_dot(s5_y.astype(BF16), wg_ref[...]) + bg_ref[...])
    proj = _dot(s5_out.astype(BF16), wo_ref[0:w5, :])
    o = of_ref[0] + ob_ref[0]
    gate = g_ref[0].astype(F32)
    nw = nw_ref[...]
    for hd in range(heads):
        sl = slice(hd * LANES, (hd + 1) * LANES)
        oh = o[:, sl]
        ms = jnp.mean(oh * oh, axis=-1, keepdims=True)
        hg = (oh * lax.rsqrt(ms + RMS_EPS) * nw * gate[:, sl]).astype(BF16)
        proj = proj + _dot(hg, wo_ref[w5 + hd * LANES:w5 + (hd + 1) * LANES, :])
    z = alpha * x_ref[0] + _mod_row(gate_ref, ctx_row) * proj
    o_ref[0] = _layer_norm(z, lg_ref[...], lbias_ref[...])


def _post(x, u, y5, o_f, o_b, g, mods, consts, l, alpha, ctx_row):
    bsz, n, d = x.shape
    w5 = u.shape[-1]
    hgw = g.shape[-1]
    tb = min(n, 512)
    tok = lambda c: pl.BlockSpec((1, tb, c), lambda b, j: (b, j, 0))
    nblk = len(y5)
    return pl.pallas_call(
        functools.partial(_post_kernel, alpha=alpha, w5=w5, heads=hgw // LANES, nblk=nblk, ctx_row=ctx_row),
        grid=(bsz, n // tb),
        in_specs=[tok(LANES)] * nblk + [tok(d), tok(w5), tok(hgw), tok(hgw), tok(hgw), _mod_spec(mods, l, 2, d)]
        + [_layer_spec(a, l, 2) for a in consts],
        out_specs=tok(d),
        out_shape=jax.ShapeDtypeStruct((bsz, n, d), F32),
        compiler_params=_cparams("arbitrary", "arbitrary"),
        name="mixer_post",
    )(*y5, x, u, o_f, o_b, g, mods, *consts)


def _ffn_kernel(x_ref, sh_ref, sc_ref, gate_ref, wu_ref, cw_ref, cb_ref, wd_ref, lg_ref, lbias_ref, o_ref, act_scr,
                *, alpha, row_w, tb, tf, dff, ctx_row):
    x = x_ref[0]
    h = (x * (1.0 + _mod_row(sc_ref, ctx_row)) + _mod_row(sh_ref, ctx_row)).astype(BF16)
    pos = lax.broadcasted_iota(jnp.int32, (tb, 1), 0) % row_w
    has_prev = pos != 0
    has_next = pos != row_w - 1

    def conv(up, c0):
        prev = jnp.where(has_prev, pltpu.roll(up, 1, 0), 0.0)
        nxt = jnp.where(has_next, pltpu.roll(up, tb - 1, 0), 0.0)
        cols = slice(c0, c0 + tf)
        return prev * cw_ref[0:1, cols] + up * cw_ref[1:2, cols] + nxt * cw_ref[2:3, cols] + cb_ref[:, cols]

    for t in range(dff // tf):
        a = conv(_dot(h, wu_ref[:, t * tf:(t + 1) * tf]), t * tf)
        g = conv(_dot(h, wu_ref[:, dff + t * tf:dff + (t + 1) * tf]), dff + t * tf)
        act_scr[:, t * tf:(t + 1) * tf] = (_silu(a) * g).astype(BF16)
    z = alpha * x + _mod_row(gate_ref, ctx_row) * _dot(act_scr[...], wd_ref[...])
    o_ref[0] = _layer_norm(z, lg_ref[...], lbias_ref[...])


def _conv_ffn(x, mods, consts, l, alpha, row_w, ctx_row):
    bsz, n, d = x.shape
    dff = consts[3].shape[1]
    tf = 256
    tb = min(n, 512)
    tok = pl.BlockSpec((1, tb, d), lambda b, i: (b, i, 0))
    return pl.pallas_call(
        functools.partial(_ffn_kernel, alpha=alpha, row_w=row_w, tb=tb, tf=tf, dff=dff, ctx_row=ctx_row),
        grid=(bsz, n // tb),
        in_specs=[tok] + [_mod_spec(mods, l, i, d) for i in (3, 4, 5)] + [_layer_spec(a, l, 2, True) for a in consts],
        out_specs=tok,
        out_shape=jax.ShapeDtypeStruct((bsz, n, d), F32),
        scratch_shapes=[pltpu.VMEM((tb, dff), BF16)],
        compiler_params=_cparams("arbitrary", "arbitrary"),
        name="conv_ffn",
    )(x, mods, mods, mods, *consts)


def _token_mixer(h_in, mods, w_in, filters, lb_all, l, init, with_out, w5, hg, ctx_row):
    u, fraw, v, q, g = _inproj(h_in, mods, w_in, l, w5, hg, ctx_row)
    s5_init, hg_init = init
    y5, s5_fin = _s5_mixer(u, filters, l, s5_init, with_out)
    o_f, o_b, hg_fin_f, hg_fin_b = _gla(fraw, v, q, lb_all, l, hg_init[0], hg_init[1])
    return (u, y5, o_f, o_b, g), (s5_fin, (hg_fin_f, hg_fin_b))


def kernel(x, c, ctx, c_ctx, w_mod, b_mod, w_in, s5_lam_re, s5_lam_im, s5_log_dt, s5_b_re, s5_b_im,
           s5_c_re, s5_c_im, s5_d, w_glu, b_glu, hg_lb, hg_norm_w, w_out, ln1_g, ln1_b,
           w_up, conv_w, conv_b, w_down, ln2_g, ln2_b):
    depth = w_mod.shape[0]
    bsz, n, d = x.shape
    n_ctx = ctx.shape[1]
    w5 = s5_d.shape[-1]
    hg = hg_lb.shape[-1]
    heads = hg // LANES
    npair = s5_lam_re.shape[2] // 2
    alpha = (2 * depth) ** 0.25

    lb_all = jnp.cumsum(jax.nn.softmax(hg_lb.astype(F32), axis=0), axis=0)
    lb_all = lb_all - lb_all[:1]

    rb = -(-(bsz + 1) // SUBLANES) * SUBLANES
    rows = jnp.concatenate([c, c_ctx[None], jnp.zeros((rb - bsz - 1, d), F32)], axis=0)
    mods = _mod_vectors(rows, w_mod, b_mod)

    vec = lambda a: a.reshape(depth, 1, a.shape[-1])
    w_in_b = w_in.astype(BF16)
    post_consts = [vec(s5_d), w_glu.astype(BF16), vec(b_glu), vec(hg_norm_w), w_out.astype(BF16), vec(ln1_g), vec(ln1_b)]
    ffn_consts = [w_up.astype(BF16), conv_w, vec(conv_b), w_down.astype(BF16), vec(ln2_g), vec(ln2_b)]

    bbt, caf, cab, w_in_pair, w_out_pair, a_pow = jax.vmap(_s5_filter_inputs)(
        s5_lam_re, s5_lam_im, s5_log_dt, s5_b_re, s5_b_im, s5_c_re, s5_c_im)
    filters = (_s5_toeplitz(bbt, caf, cab), w_in_pair, w_out_pair, a_pow)

    zero_init = (jnp.zeros((4, bsz, npair, LANES), F32),
                 (jnp.zeros((bsz, heads, LANES, LANES), F32),) * 2)

    for l in range(depth):
        last = l == depth - 1
        c_parts, ctx_states = _token_mixer(ctx, mods, w_in_b, filters, lb_all, l, zero_init, not last, w5, hg, bsz)
        x_parts, _ = _token_mixer(x, mods, w_in_b, filters, lb_all, l, ctx_states, True, w5, hg, None)
        x = _post(x, *x_parts, mods, post_consts, l, alpha, None)
        x = _conv_ffn(x, mods, ffn_consts, l, alpha, GRID_W, None)
        if not last:
            ctx = _post(ctx, *c_parts, mods, post_consts, l, alpha, bsz)
            ctx = _conv_ffn(ctx, mods, ffn_consts, l, alpha, n_ctx, bsz)
    return x
```

```python
import functools

import jax
import jax.numpy as jnp
from jax import lax
from jax.experimental import pallas as pl
from jax.experimental.pallas import tpu as pltpu

F32 = jnp.float32
BF16 = jnp.bfloat16

GRID_W = 64
HG_CHUNK = 32
S5_CHUNK = 16
LN_EPS = 1e-5
RMS_EPS = 1e-6
LANES = 128
SUBLANES = 8
GRAN = LANES // SUBLANES
VMEM_LIMIT = 56 * 1024 * 1024


def _cparams(*sem):
    return pltpu.CompilerParams(dimension_semantics=sem, vmem_limit_bytes=VMEM_LIMIT)


def _silu(x):
    return x * jax.nn.sigmoid(x)


def _dot(a, b):
    return jnp.dot(a, b, preferred_element_type=F32)


def _dot_t0(a, b):
    return lax.dot_general(a, b, (((0,), (0,)), ((), ())), preferred_element_type=F32)


def _dot_t1(a, b):
    return lax.dot_general(a, b, (((1,), (1,)), ((), ())), preferred_element_type=F32)


def _split(x, parts):
    out = []
    for _ in range(parts - 1):
        piece = x.astype(BF16)
        out.append(piece)
        x = x - piece.astype(F32)
    out.append(x.astype(BF16))
    return out


def _hp_dot_t1(a, b):
    ah, al = _split(a, 2)
    bh, bl = _split(b, 2)
    return _dot_t1(ah, bh) + _dot_t1(ah, bl) + _dot_t1(al, bh)


def _layer_norm(z, g, b):
    mu = jnp.mean(z, axis=-1, keepdims=True)
    zc = z - mu
    var = jnp.mean(zc * zc, axis=-1, keepdims=True)
    return zc * lax.rsqrt(var + LN_EPS) * g + b


def _layer_spec(a, l, ngrid, single=False):
    mode = dict(pipeline_mode=pl.Buffered(1)) if single else {}
    return pl.BlockSpec((None,) + a.shape[1:], lambda *_: (l,) + (0,) * (a.ndim - 1), **mode)


def _mod_spec(mods, l, chunk, d):
    return pl.BlockSpec((None, mods.shape[1], d), lambda *_: (l, 0, chunk))


def _mod_row(m_ref, ctx_row):
    row = pl.program_id(0) if ctx_row is None else ctx_row
    return m_ref[pl.ds(row, 1), :]


def _mod_kernel(c_ref, w_ref, b_ref, o_ref):
    s = _silu(c_ref[...]).astype(BF16)
    o_ref[0] = _dot(s, w_ref[0].astype(BF16)) + b_ref[0]


def _mod_vectors(rows, w_mod, b_mod):
    depth, d, d6 = w_mod.shape
    rb = rows.shape[0]
    tn = 1536 if d6 % 1536 == 0 else d6
    return pl.pallas_call(
        _mod_kernel,
        grid=(depth, d6 // tn),
        in_specs=[
            pl.BlockSpec((rb, d), lambda l, j: (0, 0)),
            pl.BlockSpec((1, d, tn), lambda l, j: (l, 0, j)),
            pl.BlockSpec((1, 1, tn), lambda l, j: (l, 0, j)),
        ],
        out_specs=pl.BlockSpec((1, rb, tn), lambda l, j: (l, 0, j)),
        out_shape=jax.ShapeDtypeStruct((depth, rb, d6), F32),
        compiler_params=_cparams("arbitrary", "arbitrary"),
        name="mod_vectors",
    )(rows, w_mod, b_mod.reshape(depth, 1, d6))


def _inproj_kernel(x_ref, sh_ref, sc_ref, w_ref, u_ref, f_ref, v_ref, q_ref, g_ref, *, w5, hg, ctx_row):
    h = (x_ref[0] * (1.0 + _mod_row(sc_ref, ctx_row)) + _mod_row(sh_ref, ctx_row)).astype(BF16)
    o = 0
    u_ref[0] = _dot(h, w_ref[:, o:o + w5]); o += w5
    f_ref[0] = _dot(h, w_ref[:, o:o + 2 * hg]); o += 2 * hg
    v_ref[0] = _dot(h, w_ref[:, o:o + hg]).astype(BF16); o += hg
    q_ref[0] = _silu(_dot(h, w_ref[:, o:o + hg])).astype(BF16); o += hg
    g_ref[0] = _silu(_dot(h, w_ref[:, o:o + hg])).astype(BF16)


def _inproj(x, mods, w_in, l, w5, hg, ctx_row):
    bsz, n, d = x.shape
    tb = min(n, 512)
    tok = lambda c: pl.BlockSpec((1, tb, c), lambda b, j: (b, j, 0))
    return pl.pallas_call(
        functools.partial(_inproj_kernel, w5=w5, hg=hg, ctx_row=ctx_row),
        grid=(bsz, n // tb),
        in_specs=[tok(d), _mod_spec(mods, l, 0, d), _mod_spec(mods, l, 1, d), _layer_spec(w_in, l, 2)],
        out_specs=[tok(w5), tok(2 * hg), tok(hg), tok(hg), tok(hg)],
        out_shape=[jax.ShapeDtypeStruct((bsz, n, c), t)
                   for c, t in ((w5, F32), (2 * hg, F32), (hg, BF16), (hg, BF16), (hg, BF16))],
        compiler_params=_cparams("arbitrary", "arbitrary"),
        name="inproj",
    )(x, mods, mods, w_in)


def _slot_masks(shape):
    lane = lax.broadcasted_iota(jnp.int32, shape, len(shape) - 1)
    return [(lane // GRAN) == s for s in range(SUBLANES)]


def _toeplitz_kernel(bbt_ref, caf_ref, cab_ref, t_ref, *, h, gpb):
    gw = S5_CHUNK * h
    lane = lax.broadcasted_iota(jnp.int32, (h, gw), 1)
    for gm in range(gpb):
        rf = _hp_dot_t1(bbt_ref[0, gm], caf_ref[gm])
        rb = _hp_dot_t1(bbt_ref[1, gm], cab_ref[gm])
        for lp in range(S5_CHUNK):
            sf, sb = lp * h, (S5_CHUNK - 1 - lp) * h
            fwd = rf if sf == 0 else jnp.where(lane >= sf, pltpu.roll(rf, sf, 1), 0.0)
            bwd = rb if sb == 0 else jnp.where(lane < gw - sb, pltpu.roll(rb, gw - sb, 1), 0.0)
            row = fwd + bwd
            if gm:
                row = jnp.concatenate([pltpu.roll(row[:, t * LANES:(t + 1) * LANES], gm * h, 1)
                                       for t in range(gw // LANES)], axis=1)
            r0 = (lp // SUBLANES) * LANES + ((gm + lp) % SUBLANES) * h
            t_ref[gm // 2, gm % 2, r0:r0 + h, :] = row.astype(BF16)


def _s5_toeplitz(bbt, caf, cab):
    depth, _, g, h, p2 = bbt.shape
    gw = S5_CHUNK * h
    gpb = LANES // h
    npair = g // 2
    return pl.pallas_call(
        functools.partial(_toeplitz_kernel, h=h, gpb=gpb),
        grid=(depth, g // gpb),
        in_specs=[
            pl.BlockSpec((None, 2, gpb, h, p2), lambda l, b: (l, 0, b, 0, 0)),
            pl.BlockSpec((None, gpb, gw, p2), lambda l, b: (l, b, 0, 0)),
            pl.BlockSpec((None, gpb, gw, p2), lambda l, b: (l, b, 0, 0)),
        ],
        out_specs=pl.BlockSpec((None, gpb // 2, 2, gw, gw), lambda l, b: (l, b, 0, 0, 0)),
        out_shape=jax.ShapeDtypeStruct((depth, npair, 2, gw, gw), BF16),
        compiler_params=_cparams("arbitrary", "arbitrary"),
        name="s5_toeplitz",
    )(bbt, caf, cab)


def _row_copies(hbm_ref, buf_ref, sem_ref, step, slot, tr, to_hbm):
    out = []
    for l in range(S5_CHUNK):
        hbm = hbm_ref.at[pl.ds(step * tr, tr), l, :]
        vm = buf_ref.at[slot, l]
        out.append(pltpu.make_async_copy(vm, hbm, sem_ref.at[slot, l]) if to_hbm
                   else pltpu.make_async_copy(hbm, vm, sem_ref.at[slot, l]))
    return out


def _s5_in_kernel(u_hbm, t_ref, w_ref, yi_ref, s_ref, buf, sem, uf_scr, *, tr, h, nsteps):
    i = pl.program_id(0)
    slot = i % 2
    npair = uf_scr.shape[0]
    gw = S5_CHUNK * h
    gpb = LANES // h

    @pl.when(i == 0)
    def _():
        for cp in _row_copies(u_hbm, buf, sem, 0, 0, tr, False):
            cp.start()

    for cp in _row_copies(u_hbm, buf, sem, i, slot, tr, False):
        cp.wait()

    @pl.when(i + 1 < nsteps)
    def _():
        for cp in _row_copies(u_hbm, buf, sem, i + 1, 1 - slot, tr, False):
            cp.start()

    masks = _slot_masks((tr, LANES))
    for blk in range(buf.shape[-1] // LANES):
        for t in range(S5_CHUNK // SUBLANES):
            rot = []
            for l8 in range(SUBLANES):
                x = buf[slot, t * SUBLANES + l8, :, blk * LANES:(blk + 1) * LANES]
                rot.append(pltpu.roll(x, l8 * h, 1) if l8 else x)
            for gi in range(gpb):
                g = blk * gpb + gi
                dest = rot[0]
                for l8 in range(1, SUBLANES):
                    dest = jnp.where(masks[(gi + l8) % SUBLANES], rot[l8], dest)
                c0 = (g % 2) * gw + t * LANES
                uf_scr[g // 2, :, c0:c0 + LANES] = dest
    for p in range(npair):
        ub = uf_scr[p].astype(BF16)
        s = _dot(ub, w_ref[p])
        for k in range(4):
            s_ref[k, pl.ds(p, tr, stride=npair), :] = s[:, k * LANES:(k + 1) * LANES]
        yi_ref[p] = jnp.concatenate([_dot(ub[:, a * gw:(a + 1) * gw], t_ref[p, a]) for a in range(2)], axis=1)


def _s5_in(u3, toep, w_in_pair, l, h, tr):
    r, ell, w5 = u3.shape
    _, npair, _, gw, _ = toep.shape
    nsteps = r // tr
    return pl.pallas_call(
        functools.partial(_s5_in_kernel, tr=tr, h=h, nsteps=nsteps),
        grid=(nsteps,),
        in_specs=[pl.BlockSpec(memory_space=pl.ANY), _layer_spec(toep, l, 1, True), _layer_spec(w_in_pair, l, 1, True)],
        out_specs=[
            pl.BlockSpec((npair, tr, 2 * gw), lambda i: (0, i, 0)),
            pl.BlockSpec((4, tr * npair, LANES), lambda i: (0, i, 0)),
        ],
        out_shape=[
            jax.ShapeDtypeStruct((npair, r, 2 * gw), F32),
            jax.ShapeDtypeStruct((4, r * npair, LANES), F32),
        ],
        scratch_shapes=[pltpu.VMEM((2, ell, tr, w5), F32), pltpu.SemaphoreType.DMA((2, ell)),
                        pltpu.VMEM((npair, tr, 2 * gw), F32)],
        compiler_params=_cparams("arbitrary"),
        name="s5_in",
    )(u3, toep, w_in_pair)


def _s5_scan_kernel(s_ref, s0_ref, a_ref, xp_ref, xf_ref, *, nc):
    far, fai, bar, bai = a_ref[0], a_ref[1], a_ref[2], a_ref[3]

    def body(i, carry):
        fr, fi, br, bi = carry
        ib = nc - 1 - i
        xp_ref[0, 0, i] = fr
        xp_ref[1, 0, i] = fi
        xp_ref[2, 0, ib] = br
        xp_ref[3, 0, ib] = bi
        nfr = far * fr - fai * fi + s_ref[0, 0, i]
        nfi = far * fi + fai * fr + s_ref[1, 0, i]
        nbr = bar * br - bai * bi + s_ref[2, 0, ib]
        nbi = bar * bi + bai * br + s_ref[3, 0, ib]
        return nfr, nfi, nbr, nbi

    init = (s0_ref[0, 0], s0_ref[1, 0], s0_ref[2, 0], s0_ref[3, 0])
    fr, fi, br, bi = lax.fori_loop(0, nc, body, init)
    xf_ref[0, 0] = fr
    xf_ref[1, 0] = fi
    xf_ref[2, 0] = br
    xf_ref[3, 0] = bi


def _s5_scan(s_loc, s0, a_pow, l):
    _, bsz, nc, npair, _ = s_loc.shape
    return pl.pallas_call(
        functools.partial(_s5_scan_kernel, nc=nc),
        grid=(bsz,),
        in_specs=[
            pl.BlockSpec((4, 1, nc, npair, LANES), lambda b: (0, b, 0, 0, 0)),
            pl.BlockSpec((4, 1, npair, LANES), lambda b: (0, b, 0, 0)),
            _layer_spec(a_pow, l, 1),
        ],
        out_specs=[
            pl.BlockSpec((4, 1, nc, npair, LANES), lambda b: (0, b, 0, 0, 0)),
            pl.BlockSpec((4, 1, npair, LANES), lambda b: (0, b, 0, 0)),
        ],
        out_shape=[
            jax.ShapeDtypeStruct(s_loc.shape, F32),
            jax.ShapeDtypeStruct((4, bsz, npair, LANES), F32),
        ],
        compiler_params=_cparams("arbitrary"),
        name="s5_scan",
    )(s_loc, s0, a_pow)


def _s5_out_kernel(yi_ref, xp_ref, w_ref, y_hbm, yf_scr, buf, sem, *, tr, h, nsteps):
    i = pl.program_id(0)
    npair = yi_ref.shape[0]
    gw = S5_CHUNK * h
    gpb = LANES // h
    for p in range(npair):
        xcat = jnp.concatenate([xp_ref[k, pl.ds(p, tr, stride=npair), :] for k in range(4)], axis=1)
        yf_scr[p] = yi_ref[p] + _dot(xcat.astype(BF16), w_ref[p])

    @pl.when(i > 0)
    def _():
        for cp in _row_copies(y_hbm, buf, sem, i - 1, 0, tr, True):
            cp.wait()

    masks = _slot_masks((tr, LANES))
    for blk in range(npair * 2 * h // LANES):
        for t in range(S5_CHUNK // SUBLANES):
            src = []
            for gi in range(gpb):
                g = blk * gpb + gi
                c0 = (g % 2) * gw + t * LANES
                src.append(yf_scr[g // 2, :, c0:c0 + LANES])
            for l8 in range(SUBLANES):
                m = src[0]
                for gi in range(1, gpb):
                    m = jnp.where(masks[(gi + l8) % SUBLANES], src[gi], m)
                buf[0, t * SUBLANES + l8, :, blk * LANES:(blk + 1) * LANES] = (
                    pltpu.roll(m, LANES - l8 * h, 1) if l8 else m)
    copies = _row_copies(y_hbm, buf, sem, i, 0, tr, True)
    for cp in copies:
        cp.start()

    @pl.when(i == nsteps - 1)
    def _():
        for cp in copies:
            cp.wait()


def _s5_out(y_intra, x_prev, w_out_pair, l, h, tr):
    npair, r, width = y_intra.shape
    w5 = npair * 2 * h
    nsteps = r // tr
    return pl.pallas_call(
        functools.partial(_s5_out_kernel, tr=tr, h=h, nsteps=nsteps),
        grid=(nsteps,),
        in_specs=[
            pl.BlockSpec((npair, tr, width), lambda i: (0, i, 0)),
            pl.BlockSpec((4, tr * npair, LANES), lambda i: (0, i, 0)),
            _layer_spec(w_out_pair, l, 1, True),
        ],
        out_specs=pl.BlockSpec(memory_space=pl.ANY),
        out_shape=jax.ShapeDtypeStruct((r, S5_CHUNK, w5), F32),
        scratch_shapes=[pltpu.VMEM((npair, tr, width), F32), pltpu.VMEM((1, S5_CHUNK, tr, w5), F32),
                        pltpu.SemaphoreType.DMA((1, S5_CHUNK))],
        compiler_params=_cparams("arbitrary"),
        name="s5_out",
    )(y_intra, x_prev, w_out_pair)


def _rotate_positions(w, axis, gpb):
    g = w.shape[0]
    ell = w.shape[axis]
    shp = w.shape
    w = w.reshape((g // gpb, gpb) + shp[1:axis] + (ell // SUBLANES, SUBLANES) + shp[axis + 1:])
    parts = [jnp.roll(w[:, gm], gm, axis=axis + 1) for gm in range(gpb)]
    return jnp.stack(parts, axis=1).reshape(shp)


def _s5_filter_inputs(lam_re, lam_im, log_dt, b_re, b_im, c_re, c_im):
    ndir, g, p = lam_re.shape
    h = b_re.shape[-1]
    ell = S5_CHUNK
    npair = g // 2
    lr, li = lam_re.astype(F32), lam_im.astype(F32)
    dt = jnp.exp(log_dt.astype(F32))[..., None]
    mag, ang = jnp.exp(lr * dt), li * dt
    abar_re, abar_im = mag * jnp.cos(ang), mag * jnp.sin(ang)
    den = lr * lr + li * li
    nr, ni = abar_re - 1.0, abar_im
    coef_re = ((nr * lr + ni * li) / den)[..., None]
    coef_im = ((ni * lr - nr * li) / den)[..., None]
    bb_re = coef_re * b_re - coef_im * b_im
    bb_im = coef_re * b_im + coef_im * b_re
    tau = jnp.arange(ell + 1, dtype=F32)[:, None, None, None]
    pmag, pang = jnp.exp(tau * (lr * dt)), tau * (li * dt)
    pw_re, pw_im = pmag * jnp.cos(pang), pmag * jnp.sin(pang)
    ca_re = c_re[None] * pw_re[:, :, :, None, :] - c_im[None] * pw_im[:, :, :, None, :]
    ca_im = c_re[None] * pw_im[:, :, :, None, :] + c_im[None] * pw_re[:, :, :, None, :]
    bbt = jnp.concatenate([bb_re.transpose(0, 1, 3, 2), -bb_im.transpose(0, 1, 3, 2)], axis=-1)

    def lag_rows(d, descending):
        ca = jnp.concatenate([ca_re[:ell, d], ca_im[:ell, d]], axis=-1)
        ca = ca[::-1] if descending else ca
        return ca.transpose(1, 0, 2, 3).reshape(g, ell * h, 2 * p)

    w_ins, w_outs = [], []
    for d in range(2):
        pin_re, pin_im = [(pw[:ell, d][::-1] if d == 0 else pw[:ell, d])[..., None] for pw in (pw_re, pw_im)]
        wr = pin_re * bb_re[d][None] - pin_im * bb_im[d][None]
        wi = pin_re * bb_im[d][None] + pin_im * bb_re[d][None]
        w_ins.append([_rotate_positions(w.transpose(1, 0, 3, 2), 1, LANES // h).reshape(g, ell * h, p)
                      for w in (wr, wi)])
        orr, oii = [(ca[1:, d] if d == 0 else ca[1:, d][::-1]) for ca in (ca_re, -ca_im)]
        w_outs.append([_rotate_positions(w.transpose(1, 3, 0, 2), 2, LANES // h).reshape(g, p, ell * h)
                       for w in (orr, oii)])
    eye2 = jnp.eye(2, dtype=F32)

    def pair_cols(w):
        gg, rr, cc = w.shape
        w = w.reshape(npair, 2, rr, cc)
        return jnp.einsum('parc,ab->parbc', w, eye2).reshape(npair, 2 * rr, 2 * cc)

    w_in_pair = jnp.concatenate([pair_cols(w_ins[d][k]) for d in range(2) for k in range(2)], axis=-1)
    w_out_pair = jnp.concatenate([pair_cols(w_outs[d][k]) for d in range(2) for k in range(2)], axis=1)
    a_pow = jnp.stack([pw[ell, d].reshape(npair, 2 * p) for d in range(2) for pw in (pw_re, pw_im)])
    return bbt, lag_rows(0, False), lag_rows(1, True), w_in_pair.astype(BF16), w_out_pair.astype(BF16), a_pow


def _s5_mixer(u, filters, l, s0, with_out):
    toep, w_in_pair, w_out_pair, a_pow = filters
    bsz, n, w5 = u.shape
    npair = toep.shape[1]
    h = w5 // (2 * npair)
    nc = n // S5_CHUNK
    tr = min(bsz * nc, 128)
    y_intra, s_loc = _s5_in(u.reshape(bsz * nc, S5_CHUNK, w5), toep, w_in_pair, l, h, tr)
    x_prev, x_fin = _s5_scan(s_loc.reshape(4, bsz, nc, npair, LANES), s0, a_pow, l)
    if not with_out:
        return None, x_fin
    y = _s5_out(y_intra, x_prev.reshape(4, bsz * nc * npair, LANES), w_out_pair, l, h, tr)
    return y.reshape(bsz, n, w5), x_fin


def _gla_direction(fr, q, v, lb, s_ref, o_ref, *, blk, reverse, heads):
    n = blk // HG_CHUNK
    width = heads * LANES
    f = lb + (1.0 - lb) * jax.nn.sigmoid(fr)
    k = 1.0 - f
    hi, lo = _split(jnp.log(f), 2)
    row = lax.broadcasted_iota(jnp.int32, (blk, blk), 0)
    col = lax.broadcasted_iota(jnp.int32, (blk, blk), 1)
    tri = jnp.where((col >= row) if reverse else (col <= row), 1.0, 0.0).astype(BF16)
    c = _dot(tri, hi) + _dot(tri, lo)

    def rows(i):
        return slice(blk - (i + 1) * HG_CHUNK, blk - i * HG_CHUNK) if reverse else slice(i * HG_CHUNK, (i + 1) * HG_CHUNK)

    def mem_order(chunks):
        return sorted(chunks, reverse=reverse)

    r = [jnp.zeros((1, width), F32)]
    for i in range(n):
        edge = rows(i).start if reverse else rows(i).stop - 1
        r.append(c[edge:edge + 1])

    def per_chunk(vals):
        return jnp.concatenate([jnp.broadcast_to(vals[i], (HG_CHUNK, width)) for i in mem_order(range(n))], axis=0)

    qs = q * jnp.exp(c - per_chunk(r[:n]))
    kdl = k * jnp.exp(per_chunk(r[1:]) - c)
    qi = (qs * per_chunk([jnp.exp(r[i]) for i in range(n)])).astype(BF16)
    kd = (kdl * per_chunk([jnp.exp(r[n] - r[i + 1]) for i in range(n)])).astype(BF16)
    kdlb = kdl.astype(BF16)
    hop = {(i, j): jnp.exp(r[i] - r[j + 1]) for j in range(n) for i in range(j, n)}
    dparts = _split(jnp.concatenate([r[n], jnp.zeros((SUBLANES - 1, width), F32)], axis=0), 3)
    ones8 = jnp.ones((SUBLANES, LANES), BF16)
    lr = lax.broadcasted_iota(jnp.int32, (HG_CHUNK, HG_CHUNK), 0)
    lc = lax.broadcasted_iota(jnp.int32, (HG_CHUNK, HG_CHUNK), 1)
    causal = (lc >= lr) if reverse else (lc <= lr)

    for hd in range(heads):
        sl = slice(hd * LANES, (hd + 1) * LANES)
        state = s_ref[hd]
        dcol = sum(_dot_t0(p[:, sl], ones8) for p in dparts)
        o_inter = _dot(qi[:, sl], state.astype(BF16))
        s_ref[hd] = jnp.exp(dcol) * state + _dot_t0(kd[:, sl], v[:, sl])
        acc = {i: o_inter[rows(i)] for i in range(n)}
        for j in range(n):
            queries = mem_order(range(j, n))
            lhs = jnp.concatenate([qs[rows(i), sl] * hop[i, j][:, sl] for i in queries], axis=0).astype(BF16)
            att = _dot_t1(lhs, kdlb[rows(j), sl])
            pieces = [att[a * HG_CHUNK:(a + 1) * HG_CHUNK] for a in range(len(queries))]
            dpos = queries.index(j)
            pieces[dpos] = jnp.where(causal, pieces[dpos], 0.0)
            o_j = _dot(jnp.concatenate(pieces, axis=0).astype(BF16), v[rows(j), sl])
            for a, i in enumerate(queries):
                acc[i] = acc[i] + o_j[a * HG_CHUNK:(a + 1) * HG_CHUNK]
        o_ref[0, :, sl] = jnp.concatenate([acc[i] for i in mem_order(range(n))], axis=0)


def _gla_kernel(ff_ref, fb_ref, vf_ref, vb_ref, qf_ref, qb_ref, lb_ref, s0f_ref, s0b_ref,
                of_ref, ob_ref, sff_ref, sfb_ref, s_scr, *, blk, heads):
    j = pl.program_id(1)
    last = pl.num_programs(1) - 1

    @pl.when(j == 0)
    def _():
        s_scr[0] = s0f_ref[0]
        s_scr[1] = s0b_ref[0]

    _gla_direction(ff_ref[0], qf_ref[0].astype(F32), vf_ref[0], lb_ref[0:1], s_scr.at[0], of_ref,
                   blk=blk, reverse=False, heads=heads)
    _gla_direction(fb_ref[0], qb_ref[0].astype(F32), vb_ref[0], lb_ref[1:2], s_scr.at[1], ob_ref,
                   blk=blk, reverse=True, heads=heads)

    @pl.when(j == last)
    def _():
        sff_ref[0] = s_scr[0]
        sfb_ref[0] = s_scr[1]


def _gla(fraw, v, q, lb_all, l, s0f, s0b):
    bsz, n, hgw = v.shape
    heads = hgw // LANES
    blk = min(n, 256)
    nblk = n // blk
    fwd = lambda c: pl.BlockSpec((1, blk, hgw), lambda b, j: (b, j, c))
    bwd = lambda c: pl.BlockSpec((1, blk, hgw), lambda b, j: (b, nblk - 1 - j, c))
    st = pl.BlockSpec((1, heads, LANES, LANES), lambda b, j: (b, 0, 0, 0))
    return pl.pallas_call(
        functools.partial(_gla_kernel, blk=blk, heads=heads),
        grid=(bsz, nblk),
        in_specs=[fwd(0), bwd(1), fwd(0), bwd(0), fwd(0), bwd(0), _layer_spec(lb_all, l, 2), st, st],
        out_specs=[fwd(0), bwd(0), st, st],
        out_shape=[jax.ShapeDtypeStruct((bsz, n, hgw), F32)] * 2
        + [jax.ShapeDtypeStruct((bsz, heads, LANES, LANES), F32)] * 2,
        scratch_shapes=[pltpu.VMEM((2, heads, LANES, LANES), F32)],
        compiler_params=_cparams("arbitrary", "arbitrary"),
        name="gla",
    )(fraw, fraw, v, v, q, q, lb_all, s0f, s0b)


def _post_kernel(x_ref, u_ref, y_ref, of_ref, ob_ref, g_ref, gate_ref, d_ref, wg_ref, bg_ref, nw_ref,
                 wo_ref, lg_ref, lbias_ref, o_ref, *, alpha, w5, heads, ctx_row):
    s5_y = jax.nn.gelu(y_ref[0] + u_ref[0] * d_ref[...])
    s5_out = s5_y * jax.nn.sigmoid(_dot(s5_y.astype(BF16), wg_ref[...]) + bg_ref[...])
    proj = _dot(s5_out.astype(BF16), wo_ref[0:w5, :])
    o = of_ref[0] + ob_ref[0]
    gate = g_ref[0].astype(F32)
    nw = nw_ref[...]
    for hd in range(heads):
        sl = slice(hd * LANES, (hd + 1) * LANES)
        oh = o[:, sl]
        ms = jnp.mean(oh * oh, axis=-1, keepdims=True)
        hg = (oh * lax.rsqrt(ms + RMS_EPS) * nw * gate[:, sl]).astype(BF16)
        proj = proj + _dot(hg, wo_ref[w5 + hd * LANES:w5 + (hd + 1) * LANES, :])
    z = alpha * x_ref[0] + _mod_row(gate_ref, ctx_row) * proj
    o_ref[0] = _layer_norm(z, lg_ref[...], lbias_ref[...])


def _post(x, u, y5, o_f, o_b, g, mods, consts, l, alpha, ctx_row):
    bsz, n, d = x.shape
    w5 = u.shape[-1]
    hgw = g.shape[-1]
    tb = min(n, 512)
    tok = lambda c: pl.BlockSpec((1, tb, c), lambda b, j: (b, j, 0))
    return pl.pallas_call(
        functools.partial(_post_kernel, alpha=alpha, w5=w5, heads=hgw // LANES, ctx_row=ctx_row),
        grid=(bsz, n // tb),
        in_specs=[tok(d), tok(w5), tok(w5), tok(hgw), tok(hgw), tok(hgw), _mod_spec(mods, l, 2, d)]
        + [_layer_spec(a, l, 2) for a in consts],
        out_specs=tok(d),
        out_shape=jax.ShapeDtypeStruct((bsz, n, d), F32),
        compiler_params=_cparams("arbitrary", "arbitrary"),
        name="mixer_post",
    )(x, u, y5, o_f, o_b, g, mods, *consts)


def _ffn_kernel(x_ref, sh_ref, sc_ref, gate_ref, wu_ref, cw_ref, cb_ref, wd_ref, lg_ref, lbias_ref, o_ref, act_scr,
                *, alpha, row_w, tb, tf, dff, ctx_row):
    x = x_ref[0]
    h = (x * (1.0 + _mod_row(sc_ref, ctx_row)) + _mod_row(sh_ref, ctx_row)).astype(BF16)
    pos = lax.broadcasted_iota(jnp.int32, (tb, 1), 0) % row_w
    has_prev = pos != 0
    has_next = pos != row_w - 1

    def conv(up, c0):
        prev = jnp.where(has_prev, pltpu.roll(up, 1, 0), 0.0)
        nxt = jnp.where(has_next, pltpu.roll(up, tb - 1, 0), 0.0)
        cols = slice(c0, c0 + tf)
        return prev * cw_ref[0:1, cols] + up * cw_ref[1:2, cols] + nxt * cw_ref[2:3, cols] + cb_ref[:, cols]

    for t in range(dff // tf):
        a = conv(_dot(h, wu_ref[:, t * tf:(t + 1) * tf]), t * tf)
        g = conv(_dot(h, wu_ref[:, dff + t * tf:dff + (t + 1) * tf]), dff + t * tf)
        act_scr[:, t * tf:(t + 1) * tf] = (_silu(a) * g).astype(BF16)
    z = alpha * x + _mod_row(gate_ref, ctx_row) * _dot(act_scr[...], wd_ref[...])
    o_ref[0] = _layer_norm(z, lg_ref[...], lbias_ref[...])


def _conv_ffn(x, mods, consts, l, alpha, row_w, ctx_row):
    bsz, n, d = x.shape
    dff = consts[3].shape[1]
    tf = 256
    tb = min(n, 512)
    tok = pl.BlockSpec((1, tb, d), lambda b, i: (b, i, 0))
    return pl.pallas_call(
        functools.partial(_ffn_kernel, alpha=alpha, row_w=row_w, tb=tb, tf=tf, dff=dff, ctx_row=ctx_row),
        grid=(bsz, n // tb),
        in_specs=[tok] + [_mod_spec(mods, l, i, d) for i in (3, 4, 5)] + [_layer_spec(a, l, 2, True) for a in consts],
        out_specs=tok,
        out_shape=jax.ShapeDtypeStruct((bsz, n, d), F32),
        scratch_shapes=[pltpu.VMEM((tb, dff), BF16)],
        compiler_params=_cparams("arbitrary", "arbitrary"),
        name="conv_ffn",
    )(x, mods, mods, mods, *consts)


def _token_mixer(h_in, mods, w_in, filters, lb_all, l, init, with_out, w5, hg, ctx_row):
    u, fraw, v, q, g = _inproj(h_in, mods, w_in, l, w5, hg, ctx_row)
    s5_init, hg_init = init
    y5, s5_fin = _s5_mixer(u, filters, l, s5_init, with_out)
    o_f, o_b, hg_fin_f, hg_fin_b = _gla(fraw, v, q, lb_all, l, hg_init[0], hg_init[1])
    return (u, y5, o_f, o_b, g), (s5_fin, (hg_fin_f, hg_fin_b))


def kernel(x, c, ctx, c_ctx, w_mod, b_mod, w_in, s5_lam_re, s5_lam_im, s5_log_dt, s5_b_re, s5_b_im,
           s5_c_re, s5_c_im, s5_d, w_glu, b_glu, hg_lb, hg_norm_w, w_out, ln1_g, ln1_b,
           w_up, conv_w, conv_b, w_down, ln2_g, ln2_b):
    depth = w_mod.shape[0]
    bsz, n, d = x.shape
    n_ctx = ctx.shape[1]
    w5 = s5_d.shape[-1]
    hg = hg_lb.shape[-1]
    heads = hg // LANES
    npair = s5_lam_re.shape[2] // 2
    alpha = (2 * depth) ** 0.25

    lb_all = jnp.cumsum(jax.nn.softmax(hg_lb.astype(F32), axis=0), axis=0)
    lb_all = lb_all - lb_all[:1]

    rb = -(-(bsz + 1) // SUBLANES) * SUBLANES
    rows = jnp.concatenate([c, c_ctx[None], jnp.zeros((rb - bsz - 1, d), F32)], axis=0)
    mods = _mod_vectors(rows, w_mod, b_mod)

    vec = lambda a: a.reshape(depth, 1, a.shape[-1])
    w_in_b = w_in.astype(BF16)
    post_consts = [vec(s5_d), w_glu.astype(BF16), vec(b_glu), vec(hg_norm_w), w_out.astype(BF16), vec(ln1_g), vec(ln1_b)]
    ffn_consts = [w_up.astype(BF16), conv_w, vec(conv_b), w_down.astype(BF16), vec(ln2_g), vec(ln2_b)]

    bbt, caf, cab, w_in_pair, w_out_pair, a_pow = jax.vmap(_s5_filter_inputs)(
        s5_lam_re, s5_lam_im, s5_log_dt, s5_b_re, s5_b_im, s5_c_re, s5_c_im)
    filters = (_s5_toeplitz(bbt, caf, cab), w_in_pair, w_out_pair, a_pow)

    zero_init = (jnp.zeros((4, bsz, npair, LANES), F32),
                 (jnp.zeros((bsz, heads, LANES, LANES), F32),) * 2)

    for l in range(depth):
        last = l == depth - 1
        c_parts, ctx_states = _token_mixer(ctx, mods, w_in_b, filters, lb_all, l, zero_init, not last, w5, hg, bsz)
        x_parts, _ = _token_mixer(x, mods, w_in_b, filters, lb_all, l, ctx_states, True, w5, hg, None)
        x = _post(x, *x_parts, mods, post_consts, l, alpha, None)
        x = _conv_ffn(x, mods, ffn_consts, l, alpha, GRID_W, None)
        if not last:
            ctx = _post(ctx, *c_parts, mods, post_consts, l, alpha, bsz)
            ctx = _conv_ffn(ctx, mods, ffn_consts, l, alpha, n_ctx, bsz)
    return x
```

```python
import functools

import jax
import jax.numpy as jnp
from jax import lax
from jax.experimental import pallas as pl
from jax.experimental.pallas import tpu as pltpu

F32 = jnp.float32
BF16 = jnp.bfloat16

GRID_W = 64
HG_CHUNK = 32
S5_CHUNK = 16
LN_EPS = 1e-5
RMS_EPS = 1e-6
LANES = 128
SUBLANES = 8
GRAN = LANES // SUBLANES
VMEM_LIMIT = 56 * 1024 * 1024


def _cparams(*sem):
    return pltpu.CompilerParams(dimension_semantics=sem, vmem_limit_bytes=VMEM_LIMIT)


def _silu(x):
    return x * jax.nn.sigmoid(x)


def _dot(a, b):
    return jnp.dot(a, b, preferred_element_type=F32)


def _dot_t0(a, b):
    return lax.dot_general(a, b, (((0,), (0,)), ((), ())), preferred_element_type=F32)


def _dot_t1(a, b):
    return lax.dot_general(a, b, (((1,), (1,)), ((), ())), preferred_element_type=F32)


def _split(x, parts):
    out = []
    for _ in range(parts - 1):
        piece = x.astype(BF16)
        out.append(piece)
        x = x - piece.astype(F32)
    out.append(x.astype(BF16))
    return out


def _hp_dot_t1(a, b):
    ah, al = _split(a, 2)
    bh, bl = _split(b, 2)
    return _dot_t1(ah, bh) + _dot_t1(ah, bl) + _dot_t1(al, bh)


def _layer_norm(z, g, b):
    mu = jnp.mean(z, axis=-1, keepdims=True)
    zc = z - mu
    var = jnp.mean(zc * zc, axis=-1, keepdims=True)
    return zc * lax.rsqrt(var + LN_EPS) * g + b


def _layer_spec(a, l, ngrid, single=False):
    mode = dict(pipeline_mode=pl.Buffered(1)) if single else {}
    return pl.BlockSpec((None,) + a.shape[1:], lambda *_: (l,) + (0,) * (a.ndim - 1), **mode)


def _mod_spec(mods, l, chunk, d):
    return pl.BlockSpec((None, mods.shape[1], d), lambda *_: (l, 0, chunk))


def _mod_row(m_ref, ctx_row):
    row = pl.program_id(0) if ctx_row is None else ctx_row
    return m_ref[pl.ds(row, 1), :]


def _mod_kernel(c_ref, w_ref, b_ref, o_ref):
    s = _silu(c_ref[...]).astype(BF16)
    o_ref[0] = _dot(s, w_ref[0].astype(BF16)) + b_ref[0]


def _mod_vectors(rows, w_mod, b_mod):
    depth, d, d6 = w_mod.shape
    rb = rows.shape[0]
    tn = 1536 if d6 % 1536 == 0 else d6
    return pl.pallas_call(
        _mod_kernel,
        grid=(depth, d6 // tn),
        in_specs=[
            pl.BlockSpec((rb, d), lambda l, j: (0, 0)),
            pl.BlockSpec((1, d, tn), lambda l, j: (l, 0, j)),
            pl.BlockSpec((1, 1, tn), lambda l, j: (l, 0, j)),
        ],
        out_specs=pl.BlockSpec((1, rb, tn), lambda l, j: (l, 0, j)),
        out_shape=jax.ShapeDtypeStruct((depth, rb, d6), F32),
        compiler_params=_cparams("arbitrary", "arbitrary"),
        name="mod_vectors",
    )(rows, w_mod, b_mod.reshape(depth, 1, d6))


def _inproj_kernel(x_ref, sh_ref, sc_ref, w_ref, u_ref, f_ref, v_ref, q_ref, g_ref, *, w5, hg, ctx_row):
    h = (x_ref[0] * (1.0 + _mod_row(sc_ref, ctx_row)) + _mod_row(sh_ref, ctx_row)).astype(BF16)
    o = 0
    u_ref[0] = _dot(h, w_ref[:, o:o + w5]); o += w5
    f_ref[0] = _dot(h, w_ref[:, o:o + 2 * hg]); o += 2 * hg
    v_ref[0] = _dot(h, w_ref[:, o:o + hg]).astype(BF16); o += hg
    q_ref[0] = _silu(_dot(h, w_ref[:, o:o + hg])).astype(BF16); o += hg
    g_ref[0] = _silu(_dot(h, w_ref[:, o:o + hg])).astype(BF16)


def _inproj(x, mods, w_in, l, w5, hg, ctx_row):
    bsz, n, d = x.shape
    tb = min(n, 512)
    tok = lambda c: pl.BlockSpec((1, tb, c), lambda b, j: (b, j, 0))
    return pl.pallas_call(
        functools.partial(_inproj_kernel, w5=w5, hg=hg, ctx_row=ctx_row),
        grid=(bsz, n // tb),
        in_specs=[tok(d), _mod_spec(mods, l, 0, d), _mod_spec(mods, l, 1, d), _layer_spec(w_in, l, 2)],
        out_specs=[tok(w5), tok(2 * hg), tok(hg), tok(hg), tok(hg)],
        out_shape=[jax.ShapeDtypeStruct((bsz, n, c), t)
                   for c, t in ((w5, F32), (2 * hg, F32), (hg, BF16), (hg, BF16), (hg, BF16))],
        compiler_params=_cparams("arbitrary", "arbitrary"),
        name="inproj",
    )(x, mods, mods, w_in)


def _slot_masks(shape):
    lane = lax.broadcasted_iota(jnp.int32, shape, len(shape) - 1)
    return [(lane // GRAN) == s for s in range(SUBLANES)]


def _toeplitz_kernel(bbt_ref, caf_ref, cab_ref, t_ref, *, h, gpb):
    gw = S5_CHUNK * h
    lane = lax.broadcasted_iota(jnp.int32, (h, gw), 1)
    for gm in range(gpb):
        rf = _hp_dot_t1(bbt_ref[0, gm], caf_ref[gm])
        rb = _hp_dot_t1(bbt_ref[1, gm], cab_ref[gm])
        for lp in range(S5_CHUNK):
            sf, sb = lp * h, (S5_CHUNK - 1 - lp) * h
            fwd = rf if sf == 0 else jnp.where(lane >= sf, pltpu.roll(rf, sf, 1), 0.0)
            bwd = rb if sb == 0 else jnp.where(lane < gw - sb, pltpu.roll(rb, gw - sb, 1), 0.0)
            row = fwd + bwd
            if gm:
                row = jnp.concatenate([pltpu.roll(row[:, t * LANES:(t + 1) * LANES], gm * h, 1)
                                       for t in range(gw // LANES)], axis=1)
            r0 = (lp // SUBLANES) * LANES + ((gm + lp) % SUBLANES) * h
            t_ref[gm // 2, gm % 2, r0:r0 + h, :] = row.astype(BF16)


def _s5_toeplitz(bbt, caf, cab):
    depth, _, g, h, p2 = bbt.shape
    gw = S5_CHUNK * h
    gpb = LANES // h
    npair = g // 2
    return pl.pallas_call(
        functools.partial(_toeplitz_kernel, h=h, gpb=gpb),
        grid=(depth, g // gpb),
        in_specs=[
            pl.BlockSpec((None, 2, gpb, h, p2), lambda l, b: (l, 0, b, 0, 0)),
            pl.BlockSpec((None, gpb, gw, p2), lambda l, b: (l, b, 0, 0)),
            pl.BlockSpec((None, gpb, gw, p2), lambda l, b: (l, b, 0, 0)),
        ],
        out_specs=pl.BlockSpec((None, gpb // 2, 2, gw, gw), lambda l, b: (l, b, 0, 0, 0)),
        out_shape=jax.ShapeDtypeStruct((depth, npair, 2, gw, gw), BF16),
        compiler_params=_cparams("arbitrary", "arbitrary"),
        name="s5_toeplitz",
    )(bbt, caf, cab)


def _row_copies(hbm_ref, buf_ref, sem_ref, step, slot, tr, to_hbm):
    out = []
    for l in range(S5_CHUNK):
        hbm = hbm_ref.at[pl.ds(step * tr, tr), l, :]
        vm = buf_ref.at[slot, l]
        out.append(pltpu.make_async_copy(vm, hbm, sem_ref.at[slot, l]) if to_hbm
                   else pltpu.make_async_copy(hbm, vm, sem_ref.at[slot, l]))
    return out


def _s5_in_kernel(u_hbm, t_ref, w_ref, yi_ref, s_ref, buf, sem, uf_scr, *, tr, h, nsteps):
    i = pl.program_id(0)
    slot = i % 2
    npair = uf_scr.shape[0]
    gw = S5_CHUNK * h
    gpb = LANES // h

    @pl.when(i == 0)
    def _():
        for cp in _row_copies(u_hbm, buf, sem, 0, 0, tr, False):
            cp.start()

    for cp in _row_copies(u_hbm, buf, sem, i, slot, tr, False):
        cp.wait()

    @pl.when(i + 1 < nsteps)
    def _():
        for cp in _row_copies(u_hbm, buf, sem, i + 1, 1 - slot, tr, False):
            cp.start()

    masks = _slot_masks((tr, LANES))
    for blk in range(buf.shape[-1] // LANES):
        for t in range(S5_CHUNK // SUBLANES):
            rot = []
            for l8 in range(SUBLANES):
                x = buf[slot, t * SUBLANES + l8, :, blk * LANES:(blk + 1) * LANES]
                rot.append(pltpu.roll(x, l8 * h, 1) if l8 else x)
            for gi in range(gpb):
                g = blk * gpb + gi
                dest = rot[0]
                for l8 in range(1, SUBLANES):
                    dest = jnp.where(masks[(gi + l8) % SUBLANES], rot[l8], dest)
                c0 = (g % 2) * gw + t * LANES
                uf_scr[g // 2, :, c0:c0 + LANES] = dest
    for p in range(npair):
        ub = uf_scr[p].astype(BF16)
        s = _dot(ub, w_ref[p])
        for k in range(4):
            s_ref[k, pl.ds(p, tr, stride=npair), :] = s[:, k * LANES:(k + 1) * LANES]
        yi_ref[p] = jnp.concatenate([_dot(ub[:, a * gw:(a + 1) * gw], t_ref[p, a]) for a in range(2)], axis=1)


def _s5_in(u3, toep, w_in_pair, l, h, tr):
    r, ell, w5 = u3.shape
    _, npair, _, gw, _ = toep.shape
    nsteps = r // tr
    return pl.pallas_call(
        functools.partial(_s5_in_kernel, tr=tr, h=h, nsteps=nsteps),
        grid=(nsteps,),
        in_specs=[pl.BlockSpec(memory_space=pl.ANY), _layer_spec(toep, l, 1, True), _layer_spec(w_in_pair, l, 1, True)],
        out_specs=[
            pl.BlockSpec((npair, tr, 2 * gw), lambda i: (0, i, 0)),
            pl.BlockSpec((4, tr * npair, LANES), lambda i: (0, i, 0)),
        ],
        out_shape=[
            jax.ShapeDtypeStruct((npair, r, 2 * gw), F32),
            jax.ShapeDtypeStruct((4, r * npair, LANES), F32),
        ],
        scratch_shapes=[pltpu.VMEM((2, ell, tr, w5), F32), pltpu.SemaphoreType.DMA((2, ell)),
                        pltpu.VMEM((npair, tr, 2 * gw), F32)],
        compiler_params=_cparams("arbitrary"),
        name="s5_in",
    )(u3, toep, w_in_pair)


def _s5_scan_kernel(s_ref, s0_ref, a_ref, xp_ref, xf_ref, *, nc):
    far, fai, bar, bai = a_ref[0], a_ref[1], a_ref[2], a_ref[3]

    def body(i, carry):
        fr, fi, br, bi = carry
        ib = nc - 1 - i
        xp_ref[0, 0, i] = fr
        xp_ref[1, 0, i] = fi
        xp_ref[2, 0, ib] = br
        xp_ref[3, 0, ib] = bi
        nfr = far * fr - fai * fi + s_ref[0, 0, i]
        nfi = far * fi + fai * fr + s_ref[1, 0, i]
        nbr = bar * br - bai * bi + s_ref[2, 0, ib]
        nbi = bar * bi + bai * br + s_ref[3, 0, ib]
        return nfr, nfi, nbr, nbi

    init = (s0_ref[0, 0], s0_ref[1, 0], s0_ref[2, 0], s0_ref[3, 0])
    fr, fi, br, bi = lax.fori_loop(0, nc, body, init)
    xf_ref[0, 0] = fr
    xf_ref[1, 0] = fi
    xf_ref[2, 0] = br
    xf_ref[3, 0] = bi


def _s5_scan(s_loc, s0, a_pow, l):
    _, bsz, nc, npair, _ = s_loc.shape
    return pl.pallas_call(
        functools.partial(_s5_scan_kernel, nc=nc),
        grid=(bsz,),
        in_specs=[
            pl.BlockSpec((4, 1, nc, npair, LANES), lambda b: (0, b, 0, 0, 0)),
            pl.BlockSpec((4, 1, npair, LANES), lambda b: (0, b, 0, 0)),
            _layer_spec(a_pow, l, 1),
        ],
        out_specs=[
            pl.BlockSpec((4, 1, nc, npair, LANES), lambda b: (0, b, 0, 0, 0)),
            pl.BlockSpec((4, 1, npair, LANES), lambda b: (0, b, 0, 0)),
        ],
        out_shape=[
            jax.ShapeDtypeStruct(s_loc.shape, F32),
            jax.ShapeDtypeStruct((4, bsz, npair, LANES), F32),
        ],
        compiler_params=_cparams("arbitrary"),
        name="s5_scan",
    )(s_loc, s0, a_pow)


def _s5_out_kernel(yi_ref, xp_ref, w_ref, y_hbm, yf_scr, buf, sem, *, tr, h, nsteps):
    i = pl.program_id(0)
    npair = yi_ref.shape[0]
    gw = S5_CHUNK * h
    gpb = LANES // h
    for p in range(npair):
        xcat = jnp.concatenate([xp_ref[k, pl.ds(p, tr, stride=npair), :] for k in range(4)], axis=1)
        yf_scr[p] = yi_ref[p] + _dot(xcat.astype(BF16), w_ref[p])

    @pl.when(i > 0)
    def _():
        for cp in _row_copies(y_hbm, buf, sem, i - 1, 0, tr, True):
            cp.wait()

    masks = _slot_masks((tr, LANES))
    for blk in range(npair * 2 * h // LANES):
        for t in range(S5_CHUNK // SUBLANES):
            src = []
            for gi in range(gpb):
                g = blk * gpb + gi
                c0 = (g % 2) * gw + t * LANES
                src.append(yf_scr[g // 2, :, c0:c0 + LANES])
            for l8 in range(SUBLANES):
                m = src[0]
                for gi in range(1, gpb):
                    m = jnp.where(masks[(gi + l8) % SUBLANES], src[gi], m)
                buf[0, t * SUBLANES + l8, :, blk * LANES:(blk + 1) * LANES] = (
                    pltpu.roll(m, LANES - l8 * h, 1) if l8 else m)
    copies = _row_copies(y_hbm, buf, sem, i, 0, tr, True)
    for cp in copies:
        cp.start()

    @pl.when(i == nsteps - 1)
    def _():
        for cp in copies:
            cp.wait()


def _s5_out(y_intra, x_prev, w_out_pair, l, h, tr):
    npair, r, width = y_intra.shape
    w5 = npair * 2 * h
    nsteps = r // tr
    return pl.pallas_call(
        functools.partial(_s5_out_kernel, tr=tr, h=h, nsteps=nsteps),
        grid=(nsteps,),
        in_specs=[
            pl.BlockSpec((npair, tr, width), lambda i: (0, i, 0)),
            pl.BlockSpec((4, tr * npair, LANES), lambda i: (0, i, 0)),
            _layer_spec(w_out_pair, l, 1, True),
        ],
        out_specs=pl.BlockSpec(memory_space=pl.ANY),
        out_shape=jax.ShapeDtypeStruct((r, S5_CHUNK, w5), F32),
        scratch_shapes=[pltpu.VMEM((npair, tr, width), F32), pltpu.VMEM((1, S5_CHUNK, tr, w5), F32),
                        pltpu.SemaphoreType.DMA((1, S5_CHUNK))],
        compiler_params=_cparams("arbitrary"),
        name="s5_out",
    )(y_intra, x_prev, w_out_pair)


def _state_mats_kernel(bb_ref, cc_ref, pw_ref, win_ref, wout_ref, t_scr, *, h, ppb):
    ell = S5_CHUNK
    gw = ell * h
    for pi in range(ppb):
        for a in range(2):
            gm = 2 * pi + a
            for d in range(2):
                for l in range(ell):
                    r0 = a * gw + (l // SUBLANES) * LANES + ((gm + l) % SUBLANES) * h
                    t_in = ell - 1 - l if d == 0 else l
                    t_out = l + 1 if d == 0 else ell - l
                    pr, pim = pw_ref[0, d, pi, t_in:t_in + 1, :], pw_ref[1, d, pi, t_in:t_in + 1, :]
                    br, bi = bb_ref[0, d, pi, a], bb_ref[1, d, pi, a]
                    win_ref[pi, r0:r0 + h, (2 * d) * LANES:(2 * d + 1) * LANES] = (pr * br - pim * bi).astype(BF16)
                    win_ref[pi, r0:r0 + h, (2 * d + 1) * LANES:(2 * d + 2) * LANES] = (pr * bi + pim * br).astype(BF16)
                    pr, pim = pw_ref[0, d, pi, t_out:t_out + 1, :], pw_ref[1, d, pi, t_out:t_out + 1, :]
                    cr, ci = cc_ref[0, d, pi, a], cc_ref[1, d, pi, a]
                    t_scr[r0:r0 + h, (2 * d) * LANES:(2 * d + 1) * LANES] = cr * pr - ci * pim
                    t_scr[r0:r0 + h, (2 * d + 1) * LANES:(2 * d + 2) * LANES] = -(cr * pim + ci * pr)
        wout_ref[pi] = t_scr[...].T.astype(BF16)


def _s5_state_mats(bbp, ccp, pwp):
    depth, _, _, npair, _, h, _ = bbp.shape
    ell1 = pwp.shape[-2]
    ppb = LANES // h // 2
    width = 2 * S5_CHUNK * h
    return pl.pallas_call(
        functools.partial(_state_mats_kernel, h=h, ppb=ppb),
        grid=(depth, npair // ppb),
        in_specs=[
            pl.BlockSpec((None, 2, 2, ppb, 2, h, LANES), lambda l, b: (l, 0, 0, b, 0, 0, 0)),
            pl.BlockSpec((None, 2, 2, ppb, 2, h, LANES), lambda l, b: (l, 0, 0, b, 0, 0, 0)),
            pl.BlockSpec((None, 2, 2, ppb, ell1, LANES), lambda l, b: (l, 0, 0, b, 0, 0)),
        ],
        out_specs=[pl.BlockSpec((None, ppb, width, 4 * LANES), lambda l, b: (l, b, 0, 0)),
                   pl.BlockSpec((None, ppb, 4 * LANES, width), lambda l, b: (l, b, 0, 0))],
        out_shape=[jax.ShapeDtypeStruct((depth, npair, width, 4 * LANES), BF16),
                   jax.ShapeDtypeStruct((depth, npair, 4 * LANES, width), BF16)],
        scratch_shapes=[pltpu.VMEM((width, 4 * LANES), F32)],
        compiler_params=_cparams("arbitrary", "arbitrary"),
        name="s5_state_mats",
    )(bbp, ccp, pwp)


def _pair_halves(w):
    *lead, g, h, p = w.shape
    w = w.reshape(*lead, g // 2, 2, h, p)
    z = jnp.zeros_like(w[..., 0, :, :])
    return jnp.stack([jnp.concatenate([w[..., 0, :, :], z], axis=-1),
                      jnp.concatenate([z, w[..., 1, :, :]], axis=-1)], axis=-3)


def _s5_filter_inputs(lam_re, lam_im, log_dt, b_re, b_im, c_re, c_im):
    ndir, g, p = lam_re.shape
    h = b_re.shape[-1]
    ell = S5_CHUNK
    npair = g // 2
    lr, li = lam_re.astype(F32), lam_im.astype(F32)
    dt = jnp.exp(log_dt.astype(F32))[..., None]
    mag, ang = jnp.exp(lr * dt), li * dt
    abar_re, abar_im = mag * jnp.cos(ang), mag * jnp.sin(ang)
    den = lr * lr + li * li
    nr, ni = abar_re - 1.0, abar_im
    coef_re = ((nr * lr + ni * li) / den)[..., None]
    coef_im = ((ni * lr - nr * li) / den)[..., None]
    bb_re = coef_re * b_re - coef_im * b_im
    bb_im = coef_re * b_im + coef_im * b_re
    tau = jnp.arange(ell + 1, dtype=F32)[:, None, None, None]
    pmag, pang = jnp.exp(tau * (lr * dt)), tau * (li * dt)
    pw_re, pw_im = pmag * jnp.cos(pang), pmag * jnp.sin(pang)
    ca_re = c_re[None] * pw_re[:, :, :, None, :] - c_im[None] * pw_im[:, :, :, None, :]
    ca_im = c_re[None] * pw_im[:, :, :, None, :] + c_im[None] * pw_re[:, :, :, None, :]
    bbt = jnp.concatenate([bb_re.transpose(0, 1, 3, 2), -bb_im.transpose(0, 1, 3, 2)], axis=-1)

    def lag_rows(d, descending):
        ca = jnp.concatenate([ca_re[:ell, d], ca_im[:ell, d]], axis=-1)
        ca = ca[::-1] if descending else ca
        return ca.transpose(1, 0, 2, 3).reshape(g, ell * h, 2 * p)

    bbp = jnp.stack([_pair_halves(b.transpose(0, 1, 3, 2)) for b in (bb_re, bb_im)])
    ccp = jnp.stack([_pair_halves(c) for c in (c_re.astype(F32), c_im.astype(F32))])
    pwp = jnp.stack([pw.reshape(ell + 1, ndir, npair, 2 * p).transpose(1, 2, 0, 3) for pw in (pw_re, pw_im)])
    a_pow = jnp.stack([pw[ell, d].reshape(npair, 2 * p) for d in range(2) for pw in (pw_re, pw_im)])
    return bbt, lag_rows(0, False), lag_rows(1, True), bbp, ccp, pwp, a_pow


def _s5_mixer(u, filters, l, s0, with_out):
    toep, w_in_pair, w_out_pair, a_pow = filters
    bsz, n, w5 = u.shape
    npair = toep.shape[1]
    h = w5 // (2 * npair)
    nc = n // S5_CHUNK
    tr = min(bsz * nc, 128)
    y_intra, s_loc = _s5_in(u.reshape(bsz * nc, S5_CHUNK, w5), toep, w_in_pair, l, h, tr)
    x_prev, x_fin = _s5_scan(s_loc.reshape(4, bsz, nc, npair, LANES), s0, a_pow, l)
    if not with_out:
        return None, x_fin
    y = _s5_out(y_intra, x_prev.reshape(4, bsz * nc * npair, LANES), w_out_pair, l, h, tr)
    return y.reshape(bsz, n, w5), x_fin


def _gla_direction(fr, q, v, lb, s_ref, o_ref, *, blk, reverse, heads):
    n = blk // HG_CHUNK
    width = heads * LANES
    f = lb + (1.0 - lb) * jax.nn.sigmoid(fr)
    k = 1.0 - f
    hi, lo = _split(jnp.log(f), 2)
    row = lax.broadcasted_iota(jnp.int32, (blk, blk), 0)
    col = lax.broadcasted_iota(jnp.int32, (blk, blk), 1)
    tri = jnp.where((col >= row) if reverse else (col <= row), 1.0, 0.0).astype(BF16)
    c = _dot(tri, hi) + _dot(tri, lo)

    def rows(i):
        return slice(blk - (i + 1) * HG_CHUNK, blk - i * HG_CHUNK) if reverse else slice(i * HG_CHUNK, (i + 1) * HG_CHUNK)

    def mem_order(chunks):
        return sorted(chunks, reverse=reverse)

    r = [jnp.zeros((1, width), F32)]
    for i in range(n):
        edge = rows(i).start if reverse else rows(i).stop - 1
        r.append(c[edge:edge + 1])

    def per_chunk(vals):
        return jnp.concatenate([jnp.broadcast_to(vals[i], (HG_CHUNK, width)) for i in mem_order(range(n))], axis=0)

    qs = q * jnp.exp(c - per_chunk(r[:n]))
    kdl = k * jnp.exp(per_chunk(r[1:]) - c)
    qi = (qs * per_chunk([jnp.exp(r[i]) for i in range(n)])).astype(BF16)
    kd = (kdl * per_chunk([jnp.exp(r[n] - r[i + 1]) for i in range(n)])).astype(BF16)
    kdlb = kdl.astype(BF16)
    hop = {(i, j): jnp.exp(r[i] - r[j + 1]) for j in range(n) for i in range(j, n)}
    dparts = _split(jnp.concatenate([r[n], jnp.zeros((SUBLANES - 1, width), F32)], axis=0), 3)
    ones8 = jnp.ones((SUBLANES, LANES), BF16)
    lr = lax.broadcasted_iota(jnp.int32, (HG_CHUNK, HG_CHUNK), 0)
    lc = lax.broadcasted_iota(jnp.int32, (HG_CHUNK, HG_CHUNK), 1)
    causal = (lc >= lr) if reverse else (lc <= lr)

    for hd in range(heads):
        sl = slice(hd * LANES, (hd + 1) * LANES)
        state = s_ref[hd]
        dcol = sum(_dot_t0(p[:, sl], ones8) for p in dparts)
        o_inter = _dot(qi[:, sl], state.astype(BF16))
        s_ref[hd] = jnp.exp(dcol) * state + _dot_t0(kd[:, sl], v[:, sl])
        acc = {i: o_inter[rows(i)] for i in range(n)}
        for j in range(n):
            queries = mem_order(range(j, n))
            lhs = jnp.concatenate([qs[rows(i), sl] * hop[i, j][:, sl] for i in queries], axis=0).astype(BF16)
            att = _dot_t1(lhs, kdlb[rows(j), sl])
            pieces = [att[a * HG_CHUNK:(a + 1) * HG_CHUNK] for a in range(len(queries))]
            dpos = queries.index(j)
            pieces[dpos] = jnp.where(causal, pieces[dpos], 0.0)
            o_j = _dot(jnp.concatenate(pieces, axis=0).astype(BF16), v[rows(j), sl])
            for a, i in enumerate(queries):
                acc[i] = acc[i] + o_j[a * HG_CHUNK:(a + 1) * HG_CHUNK]
        o_ref[0, :, sl] = jnp.concatenate([acc[i] for i in mem_order(range(n))], axis=0)


def _gla_kernel(ff_ref, fb_ref, vf_ref, vb_ref, qf_ref, qb_ref, lb_ref, s0f_ref, s0b_ref,
                of_ref, ob_ref, sff_ref, sfb_ref, s_scr, *, blk, heads):
    j = pl.program_id(1)
    last = pl.num_programs(1) - 1

    @pl.when(j == 0)
    def _():
        s_scr[0] = s0f_ref[0]
        s_scr[1] = s0b_ref[0]

    _gla_direction(ff_ref[0], qf_ref[0].astype(F32), vf_ref[0], lb_ref[0:1], s_scr.at[0], of_ref,
                   blk=blk, reverse=False, heads=heads)
    _gla_direction(fb_ref[0], qb_ref[0].astype(F32), vb_ref[0], lb_ref[1:2], s_scr.at[1], ob_ref,
                   blk=blk, reverse=True, heads=heads)

    @pl.when(j == last)
    def _():
        sff_ref[0] = s_scr[0]
        sfb_ref[0] = s_scr[1]


def _gla(fraw, v, q, lb_all, l, s0f, s0b):
    bsz, n, hgw = v.shape
    heads = hgw // LANES
    blk = min(n, 256)
    nblk = n // blk
    fwd = lambda c: pl.BlockSpec((1, blk, hgw), lambda b, j: (b, j, c))
    bwd = lambda c: pl.BlockSpec((1, blk, hgw), lambda b, j: (b, nblk - 1 - j, c))
    st = pl.BlockSpec((1, heads, LANES, LANES), lambda b, j: (b, 0, 0, 0))
    return pl.pallas_call(
        functools.partial(_gla_kernel, blk=blk, heads=heads),
        grid=(bsz, nblk),
        in_specs=[fwd(0), bwd(1), fwd(0), bwd(0), fwd(0), bwd(0), _layer_spec(lb_all, l, 2), st, st],
        out_specs=[fwd(0), bwd(0), st, st],
        out_shape=[jax.ShapeDtypeStruct((bsz, n, hgw), F32)] * 2
        + [jax.ShapeDtypeStruct((bsz, heads, LANES, LANES), F32)] * 2,
        scratch_shapes=[pltpu.VMEM((2, heads, LANES, LANES), F32)],
        compiler_params=_cparams("arbitrary", "arbitrary"),
        name="gla",
    )(fraw, fraw, v, v, q, q, lb_all, s0f, s0b)


def _post_kernel(x_ref, u_ref, y_ref, of_ref, ob_ref, g_ref, gate_ref, d_ref, wg_ref, bg_ref, nw_ref,
                 wo_ref, lg_ref, lbias_ref, o_ref, *, alpha, w5, heads, ctx_row):
    s5_y = jax.nn.gelu(y_ref[0] + u_ref[0] * d_ref[...])
    s5_out = s5_y * jax.nn.sigmoid(_dot(s5_y.astype(BF16), wg_ref[...]) + bg_ref[...])
    o = of_ref[0] + ob_ref[0]
    gate = g_ref[0].astype(F32)
    nw = nw_ref[...]
    mixed = [s5_out.astype(BF16)]
    for hd in range(heads):
        sl = slice(hd * LANES, (hd + 1) * LANES)
        oh = o[:, sl]
        ms = jnp.mean(oh * oh, axis=-1, keepdims=True)
        mixed.append((oh * lax.rsqrt(ms + RMS_EPS) * nw * gate[:, sl]).astype(BF16))
    proj = _dot(jnp.concatenate(mixed, axis=1), wo_ref[...])
    z = alpha * x_ref[0] + _mod_row(gate_ref, ctx_row) * proj
    o_ref[0] = _layer_norm(z, lg_ref[...], lbias_ref[...])


def _post(x, u, y5, o_f, o_b, g, mods, consts, l, alpha, ctx_row):
    bsz, n, d = x.shape
    w5 = u.shape[-1]
    hgw = g.shape[-1]
    tb = min(n, 512)
    tok = lambda c: pl.BlockSpec((1, tb, c), lambda b, j: (b, j, 0))
    return pl.pallas_call(
        functools.partial(_post_kernel, alpha=alpha, w5=w5, heads=hgw // LANES, ctx_row=ctx_row),
        grid=(bsz, n // tb),
        in_specs=[tok(d), tok(w5), tok(w5), tok(hgw), tok(hgw), tok(hgw), _mod_spec(mods, l, 2, d)]
        + [_layer_spec(a, l, 2) for a in consts],
        out_specs=tok(d),
        out_shape=jax.ShapeDtypeStruct((bsz, n, d), F32),
        compiler_params=_cparams("arbitrary", "arbitrary"),
        name="mixer_post",
    )(x, u, y5, o_f, o_b, g, mods, *consts)


def _ffn_kernel(x_ref, sh_ref, sc_ref, gate_ref, wu_ref, cw_ref, cb_ref, wd_ref, lg_ref, lbias_ref, o_ref, act_scr,
                *, alpha, row_w, tb, tf, dff, ctx_row):
    x = x_ref[0]
    h = (x * (1.0 + _mod_row(sc_ref, ctx_row)) + _mod_row(sh_ref, ctx_row)).astype(BF16)
    pos = lax.broadcasted_iota(jnp.int32, (tb, 1), 0) % row_w
    has_prev = pos != 0
    has_next = pos != row_w - 1

    def conv(up, c0):
        prev = jnp.where(has_prev, pltpu.roll(up, 1, 0), 0.0)
        nxt = jnp.where(has_next, pltpu.roll(up, tb - 1, 0), 0.0)
        cols = slice(c0, c0 + tf)
        return prev * cw_ref[0:1, cols] + up * cw_ref[1:2, cols] + nxt * cw_ref[2:3, cols] + cb_ref[:, cols]

    for t in range(dff // tf):
        a = conv(_dot(h, wu_ref[:, t * tf:(t + 1) * tf]), t * tf)
        g = conv(_dot(h, wu_ref[:, dff + t * tf:dff + (t + 1) * tf]), dff + t * tf)
        act_scr[:, t * tf:(t + 1) * tf] = (_silu(a) * g).astype(BF16)
    z = alpha * x + _mod_row(gate_ref, ctx_row) * _dot(act_scr[...], wd_ref[...])
    o_ref[0] = _layer_norm(z, lg_ref[...], lbias_ref[...])


def _conv_ffn(x, mods, consts, l, alpha, row_w, ctx_row):
    bsz, n, d = x.shape
    dff = consts[3].shape[1]
    tf = 256
    tb = min(n, 512)
    tok = pl.BlockSpec((1, tb, d), lambda b, i: (b, i, 0))
    return pl.pallas_call(
        functools.partial(_ffn_kernel, alpha=alpha, row_w=row_w, tb=tb, tf=tf, dff=dff, ctx_row=ctx_row),
        grid=(bsz, n // tb),
        in_specs=[tok] + [_mod_spec(mods, l, i, d) for i in (3, 4, 5)] + [_layer_spec(a, l, 2, True) for a in consts],
        out_specs=tok,
        out_shape=jax.ShapeDtypeStruct((bsz, n, d), F32),
        scratch_shapes=[pltpu.VMEM((tb, dff), BF16)],
        compiler_params=_cparams("arbitrary", "arbitrary"),
        name="conv_ffn",
    )(x, mods, mods, mods, *consts)


def _token_mixer(h_in, mods, w_in, filters, lb_all, l, init, with_out, w5, hg, ctx_row):
    u, fraw, v, q, g = _inproj(h_in, mods, w_in, l, w5, hg, ctx_row)
    s5_init, hg_init = init
    y5, s5_fin = _s5_mixer(u, filters, l, s5_init, with_out)
    o_f, o_b, hg_fin_f, hg_fin_b = _gla(fraw, v, q, lb_all, l, hg_init[0], hg_init[1])
    return (u, y5, o_f, o_b, g), (s5_fin, (hg_fin_f, hg_fin_b))


def kernel(x, c, ctx, c_ctx, w_mod, b_mod, w_in, s5_lam_re, s5_lam_im, s5_log_dt, s5_b_re, s5_b_im,
           s5_c_re, s5_c_im, s5_d, w_glu, b_glu, hg_lb, hg_norm_w, w_out, ln1_g, ln1_b,
           w_up, conv_w, conv_b, w_down, ln2_g, ln2_b):
    depth = w_mod.shape[0]
    bsz, n, d = x.shape
    n_ctx = ctx.shape[1]
    w5 = s5_d.shape[-1]
    hg = hg_lb.shape[-1]
    heads = hg // LANES
    npair = s5_lam_re.shape[2] // 2
    alpha = (2 * depth) ** 0.25

    lb_all = jnp.cumsum(jax.nn.softmax(hg_lb.astype(F32), axis=0), axis=0)
    lb_all = lb_all - lb_all[:1]

    rb = -(-(bsz + 1) // SUBLANES) * SUBLANES
    rows = jnp.concatenate([c, c_ctx[None], jnp.zeros((rb - bsz - 1, d), F32)], axis=0)
    mods = _mod_vectors(rows, w_mod, b_mod)

    vec = lambda a: a.reshape(depth, 1, a.shape[-1])
    w_in_b = w_in.astype(BF16)
    post_consts = [vec(s5_d), w_glu.astype(BF16), vec(b_glu), vec(hg_norm_w), w_out.astype(BF16), vec(ln1_g), vec(ln1_b)]
    ffn_consts = [w_up.astype(BF16), conv_w, vec(conv_b), w_down.astype(BF16), vec(ln2_g), vec(ln2_b)]

    bbt, caf, cab, bbp, ccp, pwp, a_pow = jax.vmap(_s5_filter_inputs)(
        s5_lam_re, s5_lam_im, s5_log_dt, s5_b_re, s5_b_im, s5_c_re, s5_c_im)
    filters = (_s5_toeplitz(bbt, caf, cab),) + tuple(_s5_state_mats(bbp, ccp, pwp)) + (a_pow,)

    zero_init = (jnp.zeros((4, bsz, npair, LANES), F32),
                 (jnp.zeros((bsz, heads, LANES, LANES), F32),) * 2)

    for l in range(depth):
        last = l == depth - 1
        c_parts, ctx_states = _token_mixer(ctx, mods, w_in_b, filters, lb_all, l, zero_init, not last, w5, hg, bsz)
        x_parts, _ = _token_mixer(x, mods, w_in_b, filters, lb_all, l, ctx_states, True, w5, hg, None)
        x = _post(x, *x_parts, mods, post_consts, l, alpha, None)
        x = _conv_ffn(x, mods, ffn_consts, l, alpha, GRID_W, None)
        if not last:
            ctx = _post(ctx, *c_parts, mods, post_consts, l, alpha, bsz)
            ctx = _conv_ffn(ctx, mods, ffn_consts, l, alpha, n_ctx, bsz)
    return x
```

```python
import functools

import jax
import jax.numpy as jnp
from jax import lax
from jax.experimental import pallas as pl
from jax.experimental.pallas import tpu as pltpu

F32 = jnp.float32
BF16 = jnp.bfloat16

GRID_W = 64
HG_CHUNK = 32
S5_CHUNK = 16
LN_EPS = 1e-5
RMS_EPS = 1e-6
LANES = 128
SUBLANES = 8
GRAN = LANES // SUBLANES
VMEM_LIMIT = 56 * 1024 * 1024


def _cparams(*sem):
    return pltpu.CompilerParams(dimension_semantics=sem, vmem_limit_bytes=VMEM_LIMIT)


def _silu(x):
    return x * jax.nn.sigmoid(x)


def _dot(a, b):
    return jnp.dot(a, b, preferred_element_type=F32)


def _dot_t0(a, b):
    return lax.dot_general(a, b, (((0,), (0,)), ((), ())), preferred_element_type=F32)


def _dot_t1(a, b):
    return lax.dot_general(a, b, (((1,), (1,)), ((), ())), preferred_element_type=F32)


def _split(x, parts):
    out = []
    for _ in range(parts - 1):
        piece = x.astype(BF16)
        out.append(piece)
        x = x - piece.astype(F32)
    out.append(x.astype(BF16))
    return out


def _hp_dot_t1(a, b):
    ah, al = _split(a, 2)
    bh, bl = _split(b, 2)
    return _dot_t1(ah, bh) + _dot_t1(ah, bl) + _dot_t1(al, bh)


def _layer_norm(z, g, b):
    mu = jnp.mean(z, axis=-1, keepdims=True)
    zc = z - mu
    var = jnp.mean(zc * zc, axis=-1, keepdims=True)
    return zc * lax.rsqrt(var + LN_EPS) * g + b


def _layer_spec(a, l, ngrid, single=False):
    mode = dict(pipeline_mode=pl.Buffered(1)) if single else {}
    return pl.BlockSpec((None,) + a.shape[1:], lambda *_: (l,) + (0,) * (a.ndim - 1), **mode)


def _mod_spec(mods, l, chunk, d):
    return pl.BlockSpec((None, mods.shape[1], d), lambda *_: (l, 0, chunk))


def _mod_row(m_ref, ctx_row):
    row = pl.program_id(0) if ctx_row is None else ctx_row
    return m_ref[pl.ds(row, 1), :]


def _mod_kernel(c_ref, w_ref, b_ref, o_ref):
    s = _silu(c_ref[...]).astype(BF16)
    o_ref[0] = _dot(s, w_ref[0].astype(BF16)) + b_ref[0]


def _mod_vectors(rows, w_mod, b_mod):
    depth, d, d6 = w_mod.shape
    rb = rows.shape[0]
    tn = 1536 if d6 % 1536 == 0 else d6
    return pl.pallas_call(
        _mod_kernel,
        grid=(depth, d6 // tn),
        in_specs=[
            pl.BlockSpec((rb, d), lambda l, j: (0, 0)),
            pl.BlockSpec((1, d, tn), lambda l, j: (l, 0, j)),
            pl.BlockSpec((1, 1, tn), lambda l, j: (l, 0, j)),
        ],
        out_specs=pl.BlockSpec((1, rb, tn), lambda l, j: (l, 0, j)),
        out_shape=jax.ShapeDtypeStruct((depth, rb, d6), F32),
        compiler_params=_cparams("arbitrary", "arbitrary"),
        name="mod_vectors",
    )(rows, w_mod, b_mod.reshape(depth, 1, d6))


def _inproj_kernel(x_ref, sh_ref, sc_ref, w_ref, u_ref, f_ref, v_ref, q_ref, g_ref, *, w5, hg, ctx_row):
    h = (x_ref[0] * (1.0 + _mod_row(sc_ref, ctx_row)) + _mod_row(sh_ref, ctx_row)).astype(BF16)
    o = 0
    u_ref[0] = _dot(h, w_ref[:, o:o + w5]); o += w5
    f_ref[0] = _dot(h, w_ref[:, o:o + 2 * hg]); o += 2 * hg
    v_ref[0] = _dot(h, w_ref[:, o:o + hg]).astype(BF16); o += hg
    q_ref[0] = _silu(_dot(h, w_ref[:, o:o + hg])).astype(BF16); o += hg
    g_ref[0] = _silu(_dot(h, w_ref[:, o:o + hg])).astype(BF16)


def _inproj(x, mods, w_in, l, w5, hg, ctx_row):
    bsz, n, d = x.shape
    tb = min(n, 512)
    tok = lambda c: pl.BlockSpec((1, tb, c), lambda b, j: (b, j, 0))
    return pl.pallas_call(
        functools.partial(_inproj_kernel, w5=w5, hg=hg, ctx_row=ctx_row),
        grid=(bsz, n // tb),
        in_specs=[tok(d), _mod_spec(mods, l, 0, d), _mod_spec(mods, l, 1, d), _layer_spec(w_in, l, 2)],
        out_specs=[tok(w5), tok(2 * hg), tok(hg), tok(hg), tok(hg)],
        out_shape=[jax.ShapeDtypeStruct((bsz, n, c), t)
                   for c, t in ((w5, F32), (2 * hg, F32), (hg, BF16), (hg, BF16), (hg, BF16))],
        compiler_params=_cparams("arbitrary", "arbitrary"),
        name="inproj",
    )(x, mods, mods, w_in)


def _slot_masks(shape):
    lane = lax.broadcasted_iota(jnp.int32, shape, len(shape) - 1)
    return [(lane // GRAN) == s for s in range(SUBLANES)]


def _toeplitz_kernel(bbt_ref, caf_ref, cab_ref, t_ref, *, h, gpb):
    gw = S5_CHUNK * h
    lane = lax.broadcasted_iota(jnp.int32, (h, gw), 1)
    for gm in range(gpb):
        rf = _hp_dot_t1(bbt_ref[0, gm], caf_ref[gm])
        rb = _hp_dot_t1(bbt_ref[1, gm], cab_ref[gm])
        for lp in range(S5_CHUNK):
            sf, sb = lp * h, (S5_CHUNK - 1 - lp) * h
            fwd = rf if sf == 0 else jnp.where(lane >= sf, pltpu.roll(rf, sf, 1), 0.0)
            bwd = rb if sb == 0 else jnp.where(lane < gw - sb, pltpu.roll(rb, gw - sb, 1), 0.0)
            row = fwd + bwd
            if gm:
                row = jnp.concatenate([pltpu.roll(row[:, t * LANES:(t + 1) * LANES], gm * h, 1)
                                       for t in range(gw // LANES)], axis=1)
            r0 = (lp // SUBLANES) * LANES + ((gm + lp) % SUBLANES) * h
            t_ref[gm // 2, gm % 2, r0:r0 + h, :] = row.astype(BF16)


def _s5_toeplitz(bbt, caf, cab):
    depth, _, g, h, p2 = bbt.shape
    gw = S5_CHUNK * h
    gpb = LANES // h
    npair = g // 2
    return pl.pallas_call(
        functools.partial(_toeplitz_kernel, h=h, gpb=gpb),
        grid=(depth, g // gpb),
        in_specs=[
            pl.BlockSpec((None, 2, gpb, h, p2), lambda l, b: (l, 0, b, 0, 0)),
            pl.BlockSpec((None, gpb, gw, p2), lambda l, b: (l, b, 0, 0)),
            pl.BlockSpec((None, gpb, gw, p2), lambda l, b: (l, b, 0, 0)),
        ],
        out_specs=pl.BlockSpec((None, gpb // 2, 2, gw, gw), lambda l, b: (l, b, 0, 0, 0)),
        out_shape=jax.ShapeDtypeStruct((depth, npair, 2, gw, gw), BF16),
        compiler_params=_cparams("arbitrary", "arbitrary"),
        name="s5_toeplitz",
    )(bbt, caf, cab)


def _row_copies(hbm_ref, buf_ref, sem_ref, step, slot, tr, to_hbm):
    out = []
    for l in range(S5_CHUNK):
        hbm = hbm_ref.at[pl.ds(step * tr, tr), l, :]
        vm = buf_ref.at[slot, l]
        out.append(pltpu.make_async_copy(vm, hbm, sem_ref.at[slot, l]) if to_hbm
                   else pltpu.make_async_copy(hbm, vm, sem_ref.at[slot, l]))
    return out


def _s5_in_kernel(u_hbm, t_ref, w_ref, yi_ref, s_ref, buf, sem, uf_scr, *, tr, h, nsteps):
    i = pl.program_id(0)
    slot = i % 2
    npair = uf_scr.shape[0]
    gw = S5_CHUNK * h
    gpb = LANES // h

    @pl.when(i == 0)
    def _():
        for cp in _row_copies(u_hbm, buf, sem, 0, 0, tr, False):
            cp.start()

    for cp in _row_copies(u_hbm, buf, sem, i, slot, tr, False):
        cp.wait()

    @pl.when(i + 1 < nsteps)
    def _():
        for cp in _row_copies(u_hbm, buf, sem, i + 1, 1 - slot, tr, False):
            cp.start()

    masks = _slot_masks((tr, LANES))
    for blk in range(buf.shape[-1] // LANES):
        for t in range(S5_CHUNK // SUBLANES):
            rot = []
            for l8 in range(SUBLANES):
                x = buf[slot, t * SUBLANES + l8, :, blk * LANES:(blk + 1) * LANES]
                rot.append(pltpu.roll(x, l8 * h, 1) if l8 else x)
            for gi in range(gpb):
                g = blk * gpb + gi
                dest = rot[0]
                for l8 in range(1, SUBLANES):
                    dest = jnp.where(masks[(gi + l8) % SUBLANES], rot[l8], dest)
                c0 = (g % 2) * gw + t * LANES
                uf_scr[g // 2, :, c0:c0 + LANES] = dest
    for p in range(npair):
        ub = uf_scr[p].astype(BF16)
        s = _dot(ub, w_ref[p])
        for k in range(4):
            s_ref[k, pl.ds(p, tr, stride=npair), :] = s[:, k * LANES:(k + 1) * LANES]
        yi_ref[p] = jnp.concatenate([_dot(ub[:, a * gw:(a + 1) * gw], t_ref[p, a]) for a in range(2)],
                                    axis=1).astype(BF16)


def _s5_in(u3, toep, w_in_pair, l, h, tr):
    r, ell, w5 = u3.shape
    _, npair, _, gw, _ = toep.shape
    nsteps = r // tr
    return pl.pallas_call(
        functools.partial(_s5_in_kernel, tr=tr, h=h, nsteps=nsteps),
        grid=(nsteps,),
        in_specs=[pl.BlockSpec(memory_space=pl.ANY), _layer_spec(toep, l, 1, True), _layer_spec(w_in_pair, l, 1, True)],
        out_specs=[
            pl.BlockSpec((npair, tr, 2 * gw), lambda i: (0, i, 0)),
            pl.BlockSpec((4, tr * npair, LANES), lambda i: (0, i, 0)),
        ],
        out_shape=[
            jax.ShapeDtypeStruct((npair, r, 2 * gw), BF16),
            jax.ShapeDtypeStruct((4, r * npair, LANES), F32),
        ],
        scratch_shapes=[pltpu.VMEM((2, ell, tr, w5), F32), pltpu.SemaphoreType.DMA((2, ell)),
                        pltpu.VMEM((npair, tr, 2 * gw), F32)],
        compiler_params=_cparams("arbitrary"),
        name="s5_in",
    )(u3, toep, w_in_pair)


def _s5_scan_kernel(s_ref, s0_ref, a_ref, xp_ref, xf_ref, *, nc):
    far, fai, bar, bai = a_ref[0], a_ref[1], a_ref[2], a_ref[3]

    def body(i, carry):
        fr, fi, br, bi = carry
        ib = nc - 1 - i
        xp_ref[0, 0, i] = fr
        xp_ref[1, 0, i] = fi
        xp_ref[2, 0, ib] = br
        xp_ref[3, 0, ib] = bi
        nfr = far * fr - fai * fi + s_ref[0, 0, i]
        nfi = far * fi + fai * fr + s_ref[1, 0, i]
        nbr = bar * br - bai * bi + s_ref[2, 0, ib]
        nbi = bar * bi + bai * br + s_ref[3, 0, ib]
        return nfr, nfi, nbr, nbi

    init = (s0_ref[0, 0], s0_ref[1, 0], s0_ref[2, 0], s0_ref[3, 0])
    fr, fi, br, bi = lax.fori_loop(0, nc, body, init)
    xf_ref[0, 0] = fr
    xf_ref[1, 0] = fi
    xf_ref[2, 0] = br
    xf_ref[3, 0] = bi


def _s5_scan(s_loc, s0, a_pow, l):
    _, bsz, nc, npair, _ = s_loc.shape
    return pl.pallas_call(
        functools.partial(_s5_scan_kernel, nc=nc),
        grid=(bsz,),
        in_specs=[
            pl.BlockSpec((4, 1, nc, npair, LANES), lambda b: (0, b, 0, 0, 0)),
            pl.BlockSpec((4, 1, npair, LANES), lambda b: (0, b, 0, 0)),
            _layer_spec(a_pow, l, 1),
        ],
        out_specs=[
            pl.BlockSpec((4, 1, nc, npair, LANES), lambda b: (0, b, 0, 0, 0)),
            pl.BlockSpec((4, 1, npair, LANES), lambda b: (0, b, 0, 0)),
        ],
        out_shape=[
            jax.ShapeDtypeStruct(s_loc.shape, F32),
            jax.ShapeDtypeStruct((4, bsz, npair, LANES), F32),
        ],
        compiler_params=_cparams("arbitrary"),
        name="s5_scan",
    )(s_loc, s0, a_pow)


def _s5_out_kernel(yi_ref, xp_ref, w_ref, y_hbm, yf_scr, buf, sem, *, tr, h, nsteps):
    i = pl.program_id(0)
    npair = yi_ref.shape[0]
    gw = S5_CHUNK * h
    gpb = LANES // h
    for p in range(npair):
        xcat = jnp.concatenate([xp_ref[k, pl.ds(p, tr, stride=npair), :] for k in range(4)], axis=1)
        yf_scr[p] = yi_ref[p].astype(F32) + _dot(xcat.astype(BF16), w_ref[p])

    @pl.when(i > 0)
    def _():
        for cp in _row_copies(y_hbm, buf, sem, i - 1, 0, tr, True):
            cp.wait()

    masks = _slot_masks((tr, LANES))
    for blk in range(npair * 2 * h // LANES):
        for t in range(S5_CHUNK // SUBLANES):
            src = []
            for gi in range(gpb):
                g = blk * gpb + gi
                c0 = (g % 2) * gw + t * LANES
                src.append(yf_scr[g // 2, :, c0:c0 + LANES])
            for l8 in range(SUBLANES):
                m = src[0]
                for gi in range(1, gpb):
                    m = jnp.where(masks[(gi + l8) % SUBLANES], src[gi], m)
                buf[0, t * SUBLANES + l8, :, blk * LANES:(blk + 1) * LANES] = (
                    pltpu.roll(m, LANES - l8 * h, 1) if l8 else m)
    copies = _row_copies(y_hbm, buf, sem, i, 0, tr, True)
    for cp in copies:
        cp.start()

    @pl.when(i == nsteps - 1)
    def _():
        for cp in copies:
            cp.wait()


def _s5_out(y_intra, x_prev, w_out_pair, l, h, tr):
    npair, r, width = y_intra.shape
    w5 = npair * 2 * h
    nsteps = r // tr
    return pl.pallas_call(
        functools.partial(_s5_out_kernel, tr=tr, h=h, nsteps=nsteps),
        grid=(nsteps,),
        in_specs=[
            pl.BlockSpec((npair, tr, width), lambda i: (0, i, 0)),
            pl.BlockSpec((4, tr * npair, LANES), lambda i: (0, i, 0)),
            _layer_spec(w_out_pair, l, 1, True),
        ],
        out_specs=pl.BlockSpec(memory_space=pl.ANY),
        out_shape=jax.ShapeDtypeStruct((r, S5_CHUNK, w5), F32),
        scratch_shapes=[pltpu.VMEM((npair, tr, width), F32), pltpu.VMEM((1, S5_CHUNK, tr, w5), F32),
                        pltpu.SemaphoreType.DMA((1, S5_CHUNK))],
        compiler_params=_cparams("arbitrary"),
        name="s5_out",
    )(y_intra, x_prev, w_out_pair)


def _state_mats_kernel(bb_ref, cc_ref, pw_ref, win_ref, wout_ref, t_scr, *, h, ppb):
    ell = S5_CHUNK
    gw = ell * h
    for pi in range(ppb):
        for a in range(2):
            gm = 2 * pi + a
            for d in range(2):
                for l in range(ell):
                    r0 = a * gw + (l // SUBLANES) * LANES + ((gm + l) % SUBLANES) * h
                    t_in = ell - 1 - l if d == 0 else l
                    t_out = l + 1 if d == 0 else ell - l
                    pr, pim = pw_ref[0, d, pi, t_in:t_in + 1, :], pw_ref[1, d, pi, t_in:t_in + 1, :]
                    br, bi = bb_ref[0, d, pi, a], bb_ref[1, d, pi, a]
                    win_ref[pi, r0:r0 + h, (2 * d) * LANES:(2 * d + 1) * LANES] = (pr * br - pim * bi).astype(BF16)
                    win_ref[pi, r0:r0 + h, (2 * d + 1) * LANES:(2 * d + 2) * LANES] = (pr * bi + pim * br).astype(BF16)
                    pr, pim = pw_ref[0, d, pi, t_out:t_out + 1, :], pw_ref[1, d, pi, t_out:t_out + 1, :]
                    cr, ci = cc_ref[0, d, pi, a], cc_ref[1, d, pi, a]
                    t_scr[r0:r0 + h, (2 * d) * LANES:(2 * d + 1) * LANES] = cr * pr - ci * pim
                    t_scr[r0:r0 + h, (2 * d + 1) * LANES:(2 * d + 2) * LANES] = -(cr * pim + ci * pr)
        wout_ref[pi] = t_scr[...].T.astype(BF16)


def _s5_state_mats(bbp, ccp, pwp):
    depth, _, _, npair, _, h, _ = bbp.shape
    ell1 = pwp.shape[-2]
    ppb = LANES // h // 2
    width = 2 * S5_CHUNK * h
    return pl.pallas_call(
        functools.partial(_state_mats_kernel, h=h, ppb=ppb),
        grid=(depth, npair // ppb),
        in_specs=[
            pl.BlockSpec((None, 2, 2, ppb, 2, h, LANES), lambda l, b: (l, 0, 0, b, 0, 0, 0)),
            pl.BlockSpec((None, 2, 2, ppb, 2, h, LANES), lambda l, b: (l, 0, 0, b, 0, 0, 0)),
            pl.BlockSpec((None, 2, 2, ppb, ell1, LANES), lambda l, b: (l, 0, 0, b, 0, 0)),
        ],
        out_specs=[pl.BlockSpec((None, ppb, width, 4 * LANES), lambda l, b: (l, b, 0, 0)),
                   pl.BlockSpec((None, ppb, 4 * LANES, width), lambda l, b: (l, b, 0, 0))],
        out_shape=[jax.ShapeDtypeStruct((depth, npair, width, 4 * LANES), BF16),
                   jax.ShapeDtypeStruct((depth, npair, 4 * LANES, width), BF16)],
        scratch_shapes=[pltpu.VMEM((width, 4 * LANES), F32)],
        compiler_params=_cparams("arbitrary", "arbitrary"),
        name="s5_state_mats",
    )(bbp, ccp, pwp)


def _pair_halves(w):
    *lead, g, h, p = w.shape
    w = w.reshape(*lead, g // 2, 2, h, p)
    z = jnp.zeros_like(w[..., 0, :, :])
    return jnp.stack([jnp.concatenate([w[..., 0, :, :], z], axis=-1),
                      jnp.concatenate([z, w[..., 1, :, :]], axis=-1)], axis=-3)


def _s5_filter_inputs(lam_re, lam_im, log_dt, b_re, b_im, c_re, c_im):
    ndir, g, p = lam_re.shape
    h = b_re.shape[-1]
    ell = S5_CHUNK
    npair = g // 2
    lr, li = lam_re.astype(F32), lam_im.astype(F32)
    dt = jnp.exp(log_dt.astype(F32))[..., None]
    mag, ang = jnp.exp(lr * dt), li * dt
    abar_re, abar_im = mag * jnp.cos(ang), mag * jnp.sin(ang)
    den = lr * lr + li * li
    nr, ni = abar_re - 1.0, abar_im
    coef_re = ((nr * lr + ni * li) / den)[..., None]
    coef_im = ((ni * lr - nr * li) / den)[..., None]
    bb_re = coef_re * b_re - coef_im * b_im
    bb_im = coef_re * b_im + coef_im * b_re
    tau = jnp.arange(ell + 1, dtype=F32)[:, None, None, None]
    pmag, pang = jnp.exp(tau * (lr * dt)), tau * (li * dt)
    pw_re, pw_im = pmag * jnp.cos(pang), pmag * jnp.sin(pang)
    ca_re = c_re[None] * pw_re[:, :, :, None, :] - c_im[None] * pw_im[:, :, :, None, :]
    ca_im = c_re[None] * pw_im[:, :, :, None, :] + c_im[None] * pw_re[:, :, :, None, :]
    bbt = jnp.concatenate([bb_re.transpose(0, 1, 3, 2), -bb_im.transpose(0, 1, 3, 2)], axis=-1)

    def lag_rows(d, descending):
        ca = jnp.concatenate([ca_re[:ell, d], ca_im[:ell, d]], axis=-1)
        ca = ca[::-1] if descending else ca
        return ca.transpose(1, 0, 2, 3).reshape(g, ell * h, 2 * p)

    bbp = jnp.stack([_pair_halves(b.transpose(0, 1, 3, 2)) for b in (bb_re, bb_im)])
    ccp = jnp.stack([_pair_halves(c) for c in (c_re.astype(F32), c_im.astype(F32))])
    pwp = jnp.stack([pw.reshape(ell + 1, ndir, npair, 2 * p).transpose(1, 2, 0, 3) for pw in (pw_re, pw_im)])
    a_pow = jnp.stack([pw[ell, d].reshape(npair, 2 * p) for d in range(2) for pw in (pw_re, pw_im)])
    return bbt, lag_rows(0, False), lag_rows(1, True), bbp, ccp, pwp, a_pow


def _s5_mixer(u, filters, l, s0, with_out):
    toep, w_in_pair, w_out_pair, a_pow = filters
    bsz, n, w5 = u.shape
    npair = toep.shape[1]
    h = w5 // (2 * npair)
    nc = n // S5_CHUNK
    tr = min(bsz * nc, 128)
    y_intra, s_loc = _s5_in(u.reshape(bsz * nc, S5_CHUNK, w5), toep, w_in_pair, l, h, tr)
    x_prev, x_fin = _s5_scan(s_loc.reshape(4, bsz, nc, npair, LANES), s0, a_pow, l)
    if not with_out:
        return None, x_fin
    y = _s5_out(y_intra, x_prev.reshape(4, bsz * nc * npair, LANES), w_out_pair, l, h, tr)
    return y.reshape(bsz, n, w5), x_fin


def _gla_direction(fr, q, v, lb, s_ref, o_ref, *, blk, reverse, heads):
    n = blk // HG_CHUNK
    width = heads * LANES
    f = lb + (1.0 - lb) * jax.nn.sigmoid(fr)
    k = 1.0 - f
    hi, lo = _split(jnp.log(f), 2)
    row = lax.broadcasted_iota(jnp.int32, (blk, blk), 0)
    col = lax.broadcasted_iota(jnp.int32, (blk, blk), 1)
    tri = jnp.where((col >= row) if reverse else (col <= row), 1.0, 0.0).astype(BF16)
    c = _dot(tri, hi) + _dot(tri, lo)

    def rows(i):
        return slice(blk - (i + 1) * HG_CHUNK, blk - i * HG_CHUNK) if reverse else slice(i * HG_CHUNK, (i + 1) * HG_CHUNK)

    def mem_order(chunks):
        return sorted(chunks, reverse=reverse)

    r = [jnp.zeros((1, width), F32)]
    for i in range(n):
        edge = rows(i).start if reverse else rows(i).stop - 1
        r.append(c[edge:edge + 1])

    def per_chunk(vals):
        return jnp.concatenate([jnp.broadcast_to(vals[i], (HG_CHUNK, width)) for i in mem_order(range(n))], axis=0)

    qs = q * jnp.exp(c - per_chunk(r[:n]))
    kdl = k * jnp.exp(per_chunk(r[1:]) - c)
    qi = (qs * per_chunk([jnp.exp(r[i]) for i in range(n)])).astype(BF16)
    kd = (kdl * per_chunk([jnp.exp(r[n] - r[i + 1]) for i in range(n)])).astype(BF16)
    kdlb = kdl.astype(BF16)
    hop = {(i, j): jnp.exp(r[i] - r[j + 1]) for j in range(n) for i in range(j, n)}
    dparts = _split(jnp.concatenate([r[n], jnp.zeros((SUBLANES - 1, width), F32)], axis=0), 3)
    ones8 = jnp.ones((SUBLANES, LANES), BF16)
    lr = lax.broadcasted_iota(jnp.int32, (HG_CHUNK, HG_CHUNK), 0)
    lc = lax.broadcasted_iota(jnp.int32, (HG_CHUNK, HG_CHUNK), 1)
    causal = (lc >= lr) if reverse else (lc <= lr)

    for hd in range(heads):
        sl = slice(hd * LANES, (hd + 1) * LANES)
        state = s_ref[hd]
        dcol = sum(_dot_t0(p[:, sl], ones8) for p in dparts)
        o_inter = _dot(qi[:, sl], state.astype(BF16))
        s_ref[hd] = jnp.exp(dcol) * state + _dot_t0(kd[:, sl], v[:, sl])
        acc = {i: o_inter[rows(i)] for i in range(n)}
        for j in range(n):
            queries = mem_order(range(j, n))
            lhs = jnp.concatenate([qs[rows(i), sl] * hop[i, j][:, sl] for i in queries], axis=0).astype(BF16)
            att = _dot_t1(lhs, kdlb[rows(j), sl])
            pieces = [att[a * HG_CHUNK:(a + 1) * HG_CHUNK] for a in range(len(queries))]
            dpos = queries.index(j)
            pieces[dpos] = jnp.where(causal, pieces[dpos], 0.0)
            o_j = _dot(jnp.concatenate(pieces, axis=0).astype(BF16), v[rows(j), sl])
            for a, i in enumerate(queries):
                acc[i] = acc[i] + o_j[a * HG_CHUNK:(a + 1) * HG_CHUNK]
        o_ref[0, :, sl] = jnp.concatenate([acc[i] for i in mem_order(range(n))], axis=0).astype(BF16)


def _gla_kernel(ff_ref, fb_ref, vf_ref, vb_ref, qf_ref, qb_ref, lb_ref, s0f_ref, s0b_ref,
                of_ref, ob_ref, sff_ref, sfb_ref, s_scr, *, blk, heads):
    j = pl.program_id(1)
    last = pl.num_programs(1) - 1

    @pl.when(j == 0)
    def _():
        s_scr[0] = s0f_ref[0]
        s_scr[1] = s0b_ref[0]

    _gla_direction(ff_ref[0], qf_ref[0].astype(F32), vf_ref[0], lb_ref[0:1], s_scr.at[0], of_ref,
                   blk=blk, reverse=False, heads=heads)
    _gla_direction(fb_ref[0], qb_ref[0].astype(F32), vb_ref[0], lb_ref[1:2], s_scr.at[1], ob_ref,
                   blk=blk, reverse=True, heads=heads)

    @pl.when(j == last)
    def _():
        sff_ref[0] = s_scr[0]
        sfb_ref[0] = s_scr[1]


def _gla(fraw, v, q, lb_all, l, s0f, s0b):
    bsz, n, hgw = v.shape
    heads = hgw // LANES
    blk = min(n, 256)
    nblk = n // blk
    fwd = lambda c: pl.BlockSpec((1, blk, hgw), lambda b, j: (b, j, c))
    bwd = lambda c: pl.BlockSpec((1, blk, hgw), lambda b, j: (b, nblk - 1 - j, c))
    st = pl.BlockSpec((1, heads, LANES, LANES), lambda b, j: (b, 0, 0, 0))
    return pl.pallas_call(
        functools.partial(_gla_kernel, blk=blk, heads=heads),
        grid=(bsz, nblk),
        in_specs=[fwd(0), bwd(1), fwd(0), bwd(0), fwd(0), bwd(0), _layer_spec(lb_all, l, 2), st, st],
        out_specs=[fwd(0), bwd(0), st, st],
        out_shape=[jax.ShapeDtypeStruct((bsz, n, hgw), BF16)] * 2
        + [jax.ShapeDtypeStruct((bsz, heads, LANES, LANES), F32)] * 2,
        scratch_shapes=[pltpu.VMEM((2, heads, LANES, LANES), F32)],
        compiler_params=_cparams("arbitrary", "arbitrary"),
        name="gla",
    )(fraw, fraw, v, v, q, q, lb_all, s0f, s0b)


def _tail_kernel(x_ref, u_ref, y_ref, of_ref, ob_ref, g_ref, g1_ref, sh_ref, sc_ref, g2_ref,
                 d_ref, wg_ref, bg_ref, nw_ref, wo_ref, l1g_ref, l1b_ref,
                 wu_ref, cw_ref, cb_ref, wd_ref, l2g_ref, l2b_ref, o_ref, act_scr,
                 *, alpha, heads, row_w, tb, tf, dff, ctx_row):
    s5_y = jax.nn.gelu(y_ref[0] + u_ref[0] * d_ref[...])
    s5_out = s5_y * jax.nn.sigmoid(_dot(s5_y.astype(BF16), wg_ref[...]) + bg_ref[...])
    o = of_ref[0].astype(F32) + ob_ref[0].astype(F32)
    gate = g_ref[0].astype(F32)
    nw = nw_ref[...]
    mixed = [s5_out.astype(BF16)]
    for hd in range(heads):
        sl = slice(hd * LANES, (hd + 1) * LANES)
        oh = o[:, sl]
        ms = jnp.mean(oh * oh, axis=-1, keepdims=True)
        mixed.append((oh * lax.rsqrt(ms + RMS_EPS) * nw * gate[:, sl]).astype(BF16))
    proj = _dot(jnp.concatenate(mixed, axis=1), wo_ref[...])
    x = _layer_norm(alpha * x_ref[0] + _mod_row(g1_ref, ctx_row) * proj, l1g_ref[...], l1b_ref[...])

    h = (x * (1.0 + _mod_row(sc_ref, ctx_row)) + _mod_row(sh_ref, ctx_row)).astype(BF16)
    pos = lax.broadcasted_iota(jnp.int32, (tb, 1), 0) % row_w
    has_prev = pos != 0
    has_next = pos != row_w - 1

    def conv(up, c0):
        prev = jnp.where(has_prev, pltpu.roll(up, 1, 0), 0.0)
        nxt = jnp.where(has_next, pltpu.roll(up, tb - 1, 0), 0.0)
        cols = slice(c0, c0 + tf)
        return prev * cw_ref[0:1, cols] + up * cw_ref[1:2, cols] + nxt * cw_ref[2:3, cols] + cb_ref[:, cols]

    for t in range(dff // tf):
        a = conv(_dot(h, wu_ref[:, t * tf:(t + 1) * tf]), t * tf)
        g = conv(_dot(h, wu_ref[:, dff + t * tf:dff + (t + 1) * tf]), dff + t * tf)
        act_scr[:, t * tf:(t + 1) * tf] = (_silu(a) * g).astype(BF16)
    z = alpha * x + _mod_row(g2_ref, ctx_row) * _dot(act_scr[...], wd_ref[...])
    o_ref[0] = _layer_norm(z, l2g_ref[...], l2b_ref[...])


def _layer_tail(x, u, y5, o_f, o_b, g, mods, post_consts, ffn_consts, l, alpha, row_w, ctx_row):
    bsz, n, d = x.shape
    w5 = u.shape[-1]
    hgw = g.shape[-1]
    dff = ffn_consts[3].shape[1]
    tf = 256
    tb = min(n, 512)
    tok = lambda c: pl.BlockSpec((1, tb, c), lambda b, i: (b, i, 0))
    return pl.pallas_call(
        functools.partial(_tail_kernel, alpha=alpha, heads=hgw // LANES, row_w=row_w, tb=tb, tf=tf, dff=dff,
                          ctx_row=ctx_row),
        grid=(bsz, n // tb),
        in_specs=[tok(d), tok(w5), tok(w5), tok(hgw), tok(hgw), tok(hgw)]
        + [_mod_spec(mods, l, i, d) for i in (2, 3, 4, 5)]
        + [_layer_spec(a, l, 2, True) for a in post_consts + ffn_consts],
        out_specs=tok(d),
        out_shape=jax.ShapeDtypeStruct((bsz, n, d), F32),
        scratch_shapes=[pltpu.VMEM((tb, dff), BF16)],
        compiler_params=_cparams("arbitrary", "arbitrary"),
        name="layer_tail",
    )(x, u, y5, o_f, o_b, g, mods, mods, mods, mods, *post_consts, *ffn_consts)


def _token_mixer(h_in, mods, w_in, filters, lb_all, l, init, with_out, w5, hg, ctx_row):
    u, fraw, v, q, g = _inproj(h_in, mods, w_in, l, w5, hg, ctx_row)
    s5_init, hg_init = init
    y5, s5_fin = _s5_mixer(u, filters, l, s5_init, with_out)
    o_f, o_b, hg_fin_f, hg_fin_b = _gla(fraw, v, q, lb_all, l, hg_init[0], hg_init[1])
    return (u, y5, o_f, o_b, g), (s5_fin, (hg_fin_f, hg_fin_b))


def kernel(x, c, ctx, c_ctx, w_mod, b_mod, w_in, s5_lam_re, s5_lam_im, s5_log_dt, s5_b_re, s5_b_im,
           s5_c_re, s5_c_im, s5_d, w_glu, b_glu, hg_lb, hg_norm_w, w_out, ln1_g, ln1_b,
           w_up, conv_w, conv_b, w_down, ln2_g, ln2_b):
    depth = w_mod.shape[0]
    bsz, n, d = x.shape
    n_ctx = ctx.shape[1]
    w5 = s5_d.shape[-1]
    hg = hg_lb.shape[-1]
    heads = hg // LANES
    npair = s5_lam_re.shape[2] // 2
    alpha = (2 * depth) ** 0.25

    lb_all = jnp.cumsum(jax.nn.softmax(hg_lb.astype(F32), axis=0), axis=0)
    lb_all = lb_all - lb_all[:1]

    rb = -(-(bsz + 1) // SUBLANES) * SUBLANES
    rows = jnp.concatenate([c, c_ctx[None], jnp.zeros((rb - bsz - 1, d), F32)], axis=0)
    mods = _mod_vectors(rows, w_mod, b_mod)

    vec = lambda a: a.reshape(depth, 1, a.shape[-1])
    w_in_b = w_in.astype(BF16)
    post_consts = [vec(s5_d), w_glu.astype(BF16), vec(b_glu), vec(hg_norm_w), w_out.astype(BF16), vec(ln1_g), vec(ln1_b)]
    ffn_consts = [w_up.astype(BF16), conv_w, vec(conv_b), w_down.astype(BF16), vec(ln2_g), vec(ln2_b)]

    bbt, caf, cab, bbp, ccp, pwp, a_pow = jax.vmap(_s5_filter_inputs)(
        s5_lam_re, s5_lam_im, s5_log_dt, s5_b_re, s5_b_im, s5_c_re, s5_c_im)
    filters = (_s5_toeplitz(bbt, caf, cab),) + tuple(_s5_state_mats(bbp, ccp, pwp)) + (a_pow,)

    zero_init = (jnp.zeros((4, bsz, npair, LANES), F32),
                 (jnp.zeros((bsz, heads, LANES, LANES), F32),) * 2)

    for l in range(depth):
        last = l == depth - 1
        c_parts, ctx_states = _token_mixer(ctx, mods, w_in_b, filters, lb_all, l, zero_init, not last, w5, hg, bsz)
        x_parts, _ = _token_mixer(x, mods, w_in_b, filters, lb_all, l, ctx_states, True, w5, hg, None)
        x = _layer_tail(x, *x_parts, mods, post_consts, ffn_consts, l, alpha, GRID_W, None)
        if not last:
            ctx = _layer_tail(ctx, *c_parts, mods, post_consts, ffn_consts, l, alpha, n_ctx, bsz)
    return x
```

```python
import functools

import jax
import jax.numpy as jnp
from jax import lax
from jax.experimental import pallas as pl
from jax.experimental.pallas import tpu as pltpu

F32 = jnp.float32
BF16 = jnp.bfloat16

GRID_W = 64
HG_CHUNK = 32
S5_CHUNK = 16
LN_EPS = 1e-5
RMS_EPS = 1e-6
LANES = 128
SUBLANES = 8
GRAN = LANES // SUBLANES
VMEM_LIMIT = 56 * 1024 * 1024


def _cparams(*sem):
    return pltpu.CompilerParams(dimension_semantics=sem, vmem_limit_bytes=VMEM_LIMIT)


def _silu(x):
    return x * jax.nn.sigmoid(x)


def _dot(a, b):
    return jnp.dot(a, b, preferred_element_type=F32)


def _dot_t0(a, b):
    return lax.dot_general(a, b, (((0,), (0,)), ((), ())), preferred_element_type=F32)


def _dot_t1(a, b):
    return lax.dot_general(a, b, (((1,), (1,)), ((), ())), preferred_element_type=F32)


def _split(x, parts):
    out = []
    for _ in range(parts - 1):
        piece = x.astype(BF16)
        out.append(piece)
        x = x - piece.astype(F32)
    out.append(x.astype(BF16))
    return out


def _hp_dot_t1(a, b):
    ah, al = _split(a, 2)
    bh, bl = _split(b, 2)
    return _dot_t1(ah, bh) + _dot_t1(ah, bl) + _dot_t1(al, bh)


def _layer_norm(z, g, b):
    mu = jnp.mean(z, axis=-1, keepdims=True)
    zc = z - mu
    var = jnp.mean(zc * zc, axis=-1, keepdims=True)
    return zc * lax.rsqrt(var + LN_EPS) * g + b


def _layer_spec(a, l, ngrid, single=False):
    mode = dict(pipeline_mode=pl.Buffered(1)) if single else {}
    return pl.BlockSpec((None,) + a.shape[1:], lambda *_: (l,) + (0,) * (a.ndim - 1), **mode)


def _mod_spec(mods, l, chunk, d):
    return pl.BlockSpec((None, mods.shape[1], d), lambda *_: (l, 0, chunk))


def _mod_row(m_ref, ctx_row):
    row = pl.program_id(0) if ctx_row is None else ctx_row
    return m_ref[pl.ds(row, 1), :]


def _mod_kernel(c_ref, w_ref, b_ref, o_ref):
    s = _silu(c_ref[...]).astype(BF16)
    o_ref[0] = _dot(s, w_ref[0].astype(BF16)) + b_ref[0]


def _mod_vectors(rows, w_mod, b_mod):
    depth, d, d6 = w_mod.shape
    rb = rows.shape[0]
    tn = 1536 if d6 % 1536 == 0 else d6
    return pl.pallas_call(
        _mod_kernel,
        grid=(depth, d6 // tn),
        in_specs=[
            pl.BlockSpec((rb, d), lambda l, j: (0, 0)),
            pl.BlockSpec((1, d, tn), lambda l, j: (l, 0, j)),
            pl.BlockSpec((1, 1, tn), lambda l, j: (l, 0, j)),
        ],
        out_specs=pl.BlockSpec((1, rb, tn), lambda l, j: (l, 0, j)),
        out_shape=jax.ShapeDtypeStruct((depth, rb, d6), F32),
        compiler_params=_cparams("arbitrary", "arbitrary"),
        name="mod_vectors",
    )(rows, w_mod, b_mod.reshape(depth, 1, d6))


def _inproj_kernel(x_ref, sh_ref, sc_ref, w_ref, u_ref, f_ref, v_ref, q_ref, g_ref, *, w5, hg, ctx_row):
    h = (x_ref[0] * (1.0 + _mod_row(sc_ref, ctx_row)) + _mod_row(sh_ref, ctx_row)).astype(BF16)
    o = 0
    u_ref[0] = _dot(h, w_ref[:, o:o + w5]); o += w5
    f_ref[0] = _dot(h, w_ref[:, o:o + 2 * hg]); o += 2 * hg
    v_ref[0] = _dot(h, w_ref[:, o:o + hg]).astype(BF16); o += hg
    q_ref[0] = _silu(_dot(h, w_ref[:, o:o + hg])).astype(BF16); o += hg
    g_ref[0] = _silu(_dot(h, w_ref[:, o:o + hg])).astype(BF16)


def _inproj(x, mods, w_in, l, w5, hg, ctx_row):
    bsz, n, d = x.shape
    tb = min(n, 1024)
    tok = lambda c: pl.BlockSpec((1, tb, c), lambda b, j: (b, j, 0))
    return pl.pallas_call(
        functools.partial(_inproj_kernel, w5=w5, hg=hg, ctx_row=ctx_row),
        grid=(bsz, n // tb),
        in_specs=[tok(d), _mod_spec(mods, l, 0, d), _mod_spec(mods, l, 1, d), _layer_spec(w_in, l, 2, True)],
        out_specs=[tok(w5), tok(2 * hg), tok(hg), tok(hg), tok(hg)],
        out_shape=[jax.ShapeDtypeStruct((bsz, n, c), t)
                   for c, t in ((w5, F32), (2 * hg, F32), (hg, BF16), (hg, BF16), (hg, BF16))],
        compiler_params=_cparams("arbitrary", "arbitrary"),
        name="inproj",
    )(x, mods, mods, w_in)


def _slot_masks(shape):
    lane = lax.broadcasted_iota(jnp.int32, shape, len(shape) - 1)
    return [(lane // GRAN) == s for s in range(SUBLANES)]


def _toeplitz_kernel(bbt_ref, caf_ref, cab_ref, t_ref, *, h, gpb):
    gw = S5_CHUNK * h
    lane = lax.broadcasted_iota(jnp.int32, (h, gw), 1)
    for gm in range(gpb):
        rf = _hp_dot_t1(bbt_ref[0, gm], caf_ref[gm])
        rb = _hp_dot_t1(bbt_ref[1, gm], cab_ref[gm])
        for lp in range(S5_CHUNK):
            sf, sb = lp * h, (S5_CHUNK - 1 - lp) * h
            fwd = rf if sf == 0 else jnp.where(lane >= sf, pltpu.roll(rf, sf, 1), 0.0)
            bwd = rb if sb == 0 else jnp.where(lane < gw - sb, pltpu.roll(rb, gw - sb, 1), 0.0)
            row = fwd + bwd
            if gm:
                row = jnp.concatenate([pltpu.roll(row[:, t * LANES:(t + 1) * LANES], gm * h, 1)
                                       for t in range(gw // LANES)], axis=1)
            r0 = (lp // SUBLANES) * LANES + ((gm + lp) % SUBLANES) * h
            t_ref[gm // 2, gm % 2, r0:r0 + h, :] = row.astype(BF16)


def _s5_toeplitz(bbt, caf, cab):
    depth, _, g, h, p2 = bbt.shape
    gw = S5_CHUNK * h
    gpb = LANES // h
    npair = g // 2
    return pl.pallas_call(
        functools.partial(_toeplitz_kernel, h=h, gpb=gpb),
        grid=(depth, g // gpb),
        in_specs=[
            pl.BlockSpec((None, 2, gpb, h, p2), lambda l, b: (l, 0, b, 0, 0)),
            pl.BlockSpec((None, gpb, gw, p2), lambda l, b: (l, b, 0, 0)),
            pl.BlockSpec((None, gpb, gw, p2), lambda l, b: (l, b, 0, 0)),
        ],
        out_specs=pl.BlockSpec((None, gpb // 2, 2, gw, gw), lambda l, b: (l, b, 0, 0, 0)),
        out_shape=jax.ShapeDtypeStruct((depth, npair, 2, gw, gw), BF16),
        compiler_params=_cparams("arbitrary", "arbitrary"),
        name="s5_toeplitz",
    )(bbt, caf, cab)


def _row_copies(hbm_ref, buf_ref, sem_ref, step, slot, tr, to_hbm):
    out = []
    for l in range(S5_CHUNK):
        hbm = hbm_ref.at[pl.ds(step * tr, tr), l, :]
        vm = buf_ref.at[slot, l]
        out.append(pltpu.make_async_copy(vm, hbm, sem_ref.at[slot, l]) if to_hbm
                   else pltpu.make_async_copy(hbm, vm, sem_ref.at[slot, l]))
    return out


def _s5_in_kernel(u_hbm, t_ref, w_ref, yi_ref, s_ref, buf, sem, uf_scr, *, tr, h, nsteps):
    i = pl.program_id(0)
    slot = i % 2
    npair = uf_scr.shape[0]
    gw = S5_CHUNK * h
    gpb = LANES // h

    @pl.when(i == 0)
    def _():
        for cp in _row_copies(u_hbm, buf, sem, 0, 0, tr, False):
            cp.start()

    for cp in _row_copies(u_hbm, buf, sem, i, slot, tr, False):
        cp.wait()

    @pl.when(i + 1 < nsteps)
    def _():
        for cp in _row_copies(u_hbm, buf, sem, i + 1, 1 - slot, tr, False):
            cp.start()

    masks = _slot_masks((tr, LANES))
    for blk in range(buf.shape[-1] // LANES):
        for t in range(S5_CHUNK // SUBLANES):
            rot = []
            for l8 in range(SUBLANES):
                x = buf[slot, t * SUBLANES + l8, :, blk * LANES:(blk + 1) * LANES]
                rot.append(pltpu.roll(x, l8 * h, 1) if l8 else x)
            for gi in range(gpb):
                g = blk * gpb + gi
                dest = rot[0]
                for l8 in range(1, SUBLANES):
                    dest = jnp.where(masks[(gi + l8) % SUBLANES], rot[l8], dest)
                c0 = (g % 2) * gw + t * LANES
                uf_scr[g // 2, :, c0:c0 + LANES] = dest
    for p in range(npair):
        ub = uf_scr[p].astype(BF16)
        s = _dot(ub, w_ref[p])
        for k in range(4):
            s_ref[k, pl.ds(p, tr, stride=npair), :] = s[:, k * LANES:(k + 1) * LANES]
        yi_ref[p] = jnp.concatenate([_dot(ub[:, a * gw:(a + 1) * gw], t_ref[p, a]) for a in range(2)],
                                    axis=1).astype(BF16)


def _s5_in(u3, toep, w_in_pair, l, h, tr):
    r, ell, w5 = u3.shape
    _, npair, _, gw, _ = toep.shape
    nsteps = r // tr
    return pl.pallas_call(
        functools.partial(_s5_in_kernel, tr=tr, h=h, nsteps=nsteps),
        grid=(nsteps,),
        in_specs=[pl.BlockSpec(memory_space=pl.ANY), _layer_spec(toep, l, 1, True), _layer_spec(w_in_pair, l, 1, True)],
        out_specs=[
            pl.BlockSpec((npair, tr, 2 * gw), lambda i: (0, i, 0)),
            pl.BlockSpec((4, tr * npair, LANES), lambda i: (0, i, 0)),
        ],
        out_shape=[
            jax.ShapeDtypeStruct((npair, r, 2 * gw), BF16),
            jax.ShapeDtypeStruct((4, r * npair, LANES), F32),
        ],
        scratch_shapes=[pltpu.VMEM((2, ell, tr, w5), F32), pltpu.SemaphoreType.DMA((2, ell)),
                        pltpu.VMEM((npair, tr, 2 * gw), F32)],
        compiler_params=_cparams("arbitrary"),
        name="s5_in",
    )(u3, toep, w_in_pair)


def _s5_scan_kernel(s_ref, s0_ref, a_ref, xp_ref, xf_ref, *, nc):
    far, fai, bar, bai = a_ref[0], a_ref[1], a_ref[2], a_ref[3]

    def body(i, carry):
        fr, fi, br, bi = carry
        ib = nc - 1 - i
        xp_ref[0, 0, i] = fr
        xp_ref[1, 0, i] = fi
        xp_ref[2, 0, ib] = br
        xp_ref[3, 0, ib] = bi
        nfr = far * fr - fai * fi + s_ref[0, 0, i]
        nfi = far * fi + fai * fr + s_ref[1, 0, i]
        nbr = bar * br - bai * bi + s_ref[2, 0, ib]
        nbi = bar * bi + bai * br + s_ref[3, 0, ib]
        return nfr, nfi, nbr, nbi

    init = (s0_ref[0, 0], s0_ref[1, 0], s0_ref[2, 0], s0_ref[3, 0])
    fr, fi, br, bi = lax.fori_loop(0, nc, body, init, unroll=4)
    xf_ref[0, 0] = fr
    xf_ref[1, 0] = fi
    xf_ref[2, 0] = br
    xf_ref[3, 0] = bi


def _s5_scan(s_loc, s0, a_pow, l):
    _, bsz, nc, npair, _ = s_loc.shape
    return pl.pallas_call(
        functools.partial(_s5_scan_kernel, nc=nc),
        grid=(bsz,),
        in_specs=[
            pl.BlockSpec((4, 1, nc, npair, LANES), lambda b: (0, b, 0, 0, 0)),
            pl.BlockSpec((4, 1, npair, LANES), lambda b: (0, b, 0, 0)),
            _layer_spec(a_pow, l, 1),
        ],
        out_specs=[
            pl.BlockSpec((4, 1, nc, npair, LANES), lambda b: (0, b, 0, 0, 0)),
            pl.BlockSpec((4, 1, npair, LANES), lambda b: (0, b, 0, 0)),
        ],
        out_shape=[
            jax.ShapeDtypeStruct(s_loc.shape, F32),
            jax.ShapeDtypeStruct((4, bsz, npair, LANES), F32),
        ],
        compiler_params=_cparams("arbitrary"),
        name="s5_scan",
    )(s_loc, s0, a_pow)


def _s5_out_kernel(yi_ref, xp_ref, w_ref, y_hbm, yf_scr, buf, sem, *, tr, h, nsteps):
    i = pl.program_id(0)
    npair = yi_ref.shape[0]
    gw = S5_CHUNK * h
    gpb = LANES // h
    for p in range(npair):
        xcat = jnp.concatenate([xp_ref[k, pl.ds(p, tr, stride=npair), :] for k in range(4)], axis=1)
        yf_scr[p] = yi_ref[p].astype(F32) + _dot(xcat.astype(BF16), w_ref[p])

    @pl.when(i > 0)
    def _():
        for cp in _row_copies(y_hbm, buf, sem, i - 1, 0, tr, True):
            cp.wait()

    masks = _slot_masks((tr, LANES))
    for blk in range(npair * 2 * h // LANES):
        for t in range(S5_CHUNK // SUBLANES):
            src = []
            for gi in range(gpb):
                g = blk * gpb + gi
                c0 = (g % 2) * gw + t * LANES
                src.append(yf_scr[g // 2, :, c0:c0 + LANES])
            for l8 in range(SUBLANES):
                m = src[0]
                for gi in range(1, gpb):
                    m = jnp.where(masks[(gi + l8) % SUBLANES], src[gi], m)
                buf[0, t * SUBLANES + l8, :, blk * LANES:(blk + 1) * LANES] = (
                    pltpu.roll(m, LANES - l8 * h, 1) if l8 else m)
    copies = _row_copies(y_hbm, buf, sem, i, 0, tr, True)
    for cp in copies:
        cp.start()

    @pl.when(i == nsteps - 1)
    def _():
        for cp in copies:
            cp.wait()


def _s5_out(y_intra, x_prev, w_out_pair, l, h, tr):
    npair, r, width = y_intra.shape
    w5 = npair * 2 * h
    nsteps = r // tr
    return pl.pallas_call(
        functools.partial(_s5_out_kernel, tr=tr, h=h, nsteps=nsteps),
        grid=(nsteps,),
        in_specs=[
            pl.BlockSpec((npair, tr, width), lambda i: (0, i, 0)),
            pl.BlockSpec((4, tr * npair, LANES), lambda i: (0, i, 0)),
            _layer_spec(w_out_pair, l, 1, True),
        ],
        out_specs=pl.BlockSpec(memory_space=pl.ANY),
        out_shape=jax.ShapeDtypeStruct((r, S5_CHUNK, w5), F32),
        scratch_shapes=[pltpu.VMEM((npair, tr, width), F32), pltpu.VMEM((1, S5_CHUNK, tr, w5), F32),
                        pltpu.SemaphoreType.DMA((1, S5_CHUNK))],
        compiler_params=_cparams("arbitrary"),
        name="s5_out",
    )(y_intra, x_prev, w_out_pair)


def _state_mats_kernel(bb_ref, cc_ref, pw_ref, win_ref, wout_ref, t_scr, *, h, ppb):
    ell = S5_CHUNK
    gw = ell * h
    for pi in range(ppb):
        for a in range(2):
            gm = 2 * pi + a
            for d in range(2):
                for l in range(ell):
                    r0 = a * gw + (l // SUBLANES) * LANES + ((gm + l) % SUBLANES) * h
                    t_in = ell - 1 - l if d == 0 else l
                    t_out = l + 1 if d == 0 else ell - l
                    pr, pim = pw_ref[0, d, pi, t_in:t_in + 1, :], pw_ref[1, d, pi, t_in:t_in + 1, :]
                    br, bi = bb_ref[0, d, pi, a], bb_ref[1, d, pi, a]
                    win_ref[pi, r0:r0 + h, (2 * d) * LANES:(2 * d + 1) * LANES] = (pr * br - pim * bi).astype(BF16)
                    win_ref[pi, r0:r0 + h, (2 * d + 1) * LANES:(2 * d + 2) * LANES] = (pr * bi + pim * br).astype(BF16)
                    pr, pim = pw_ref[0, d, pi, t_out:t_out + 1, :], pw_ref[1, d, pi, t_out:t_out + 1, :]
                    cr, ci = cc_ref[0, d, pi, a], cc_ref[1, d, pi, a]
                    t_scr[r0:r0 + h, (2 * d) * LANES:(2 * d + 1) * LANES] = cr * pr - ci * pim
                    t_scr[r0:r0 + h, (2 * d + 1) * LANES:(2 * d + 2) * LANES] = -(cr * pim + ci * pr)
        wout_ref[pi] = t_scr[...].T.astype(BF16)


def _s5_state_mats(bbp, ccp, pwp):
    depth, _, _, npair, _, h, _ = bbp.shape
    ell1 = pwp.shape[-2]
    ppb = LANES // h // 2
    width = 2 * S5_CHUNK * h
    return pl.pallas_call(
        functools.partial(_state_mats_kernel, h=h, ppb=ppb),
        grid=(depth, npair // ppb),
        in_specs=[
            pl.BlockSpec((None, 2, 2, ppb, 2, h, LANES), lambda l, b: (l, 0, 0, b, 0, 0, 0)),
            pl.BlockSpec((None, 2, 2, ppb, 2, h, LANES), lambda l, b: (l, 0, 0, b, 0, 0, 0)),
            pl.BlockSpec((None, 2, 2, ppb, ell1, LANES), lambda l, b: (l, 0, 0, b, 0, 0)),
        ],
        out_specs=[pl.BlockSpec((None, ppb, width, 4 * LANES), lambda l, b: (l, b, 0, 0)),
                   pl.BlockSpec((None, ppb, 4 * LANES, width), lambda l, b: (l, b, 0, 0))],
        out_shape=[jax.ShapeDtypeStruct((depth, npair, width, 4 * LANES), BF16),
                   jax.ShapeDtypeStruct((depth, npair, 4 * LANES, width), BF16)],
        scratch_shapes=[pltpu.VMEM((width, 4 * LANES), F32)],
        compiler_params=_cparams("arbitrary", "arbitrary"),
        name="s5_state_mats",
    )(bbp, ccp, pwp)


def _pair_halves(w):
    *lead, g, h, p = w.shape
    w = w.reshape(*lead, g // 2, 2, h, p)
    z = jnp.zeros_like(w[..., 0, :, :])
    return jnp.stack([jnp.concatenate([w[..., 0, :, :], z], axis=-1),
                      jnp.concatenate([z, w[..., 1, :, :]], axis=-1)], axis=-3)


def _s5_filter_inputs(lam_re, lam_im, log_dt, b_re, b_im, c_re, c_im):
    ndir, g, p = lam_re.shape
    h = b_re.shape[-1]
    ell = S5_CHUNK
    npair = g // 2
    lr, li = lam_re.astype(F32), lam_im.astype(F32)
    dt = jnp.exp(log_dt.astype(F32))[..., None]
    mag, ang = jnp.exp(lr * dt), li * dt
    abar_re, abar_im = mag * jnp.cos(ang), mag * jnp.sin(ang)
    den = lr * lr + li * li
    nr, ni = abar_re - 1.0, abar_im
    coef_re = ((nr * lr + ni * li) / den)[..., None]
    coef_im = ((ni * lr - nr * li) / den)[..., None]
    bb_re = coef_re * b_re - coef_im * b_im
    bb_im = coef_re * b_im + coef_im * b_re
    tau = jnp.arange(ell + 1, dtype=F32)[:, None, None, None]
    pmag, pang = jnp.exp(tau * (lr * dt)), tau * (li * dt)
    pw_re, pw_im = pmag * jnp.cos(pang), pmag * jnp.sin(pang)
    ca_re = c_re[None] * pw_re[:, :, :, None, :] - c_im[None] * pw_im[:, :, :, None, :]
    ca_im = c_re[None] * pw_im[:, :, :, None, :] + c_im[None] * pw_re[:, :, :, None, :]
    bbt = jnp.concatenate([bb_re.transpose(0, 1, 3, 2), -bb_im.transpose(0, 1, 3, 2)], axis=-1)

    def lag_rows(d, descending):
        ca = jnp.concatenate([ca_re[:ell, d], ca_im[:ell, d]], axis=-1)
        ca = ca[::-1] if descending else ca
        return ca.transpose(1, 0, 2, 3).reshape(g, ell * h, 2 * p)

    bbp = jnp.stack([_pair_halves(b.transpose(0, 1, 3, 2)) for b in (bb_re, bb_im)])
    ccp = jnp.stack([_pair_halves(c) for c in (c_re.astype(F32), c_im.astype(F32))])
    pwp = jnp.stack([pw.reshape(ell + 1, ndir, npair, 2 * p).transpose(1, 2, 0, 3) for pw in (pw_re, pw_im)])
    a_pow = jnp.stack([pw[ell, d].reshape(npair, 2 * p) for d in range(2) for pw in (pw_re, pw_im)])
    return bbt, lag_rows(0, False), lag_rows(1, True), bbp, ccp, pwp, a_pow


def _s5_mixer(u, filters, l, s0, with_out):
    toep, w_in_pair, w_out_pair, a_pow = filters
    bsz, n, w5 = u.shape
    npair = toep.shape[1]
    h = w5 // (2 * npair)
    nc = n // S5_CHUNK
    tr = min(bsz * nc, 128)
    y_intra, s_loc = _s5_in(u.reshape(bsz * nc, S5_CHUNK, w5), toep, w_in_pair, l, h, tr)
    x_prev, x_fin = _s5_scan(s_loc.reshape(4, bsz, nc, npair, LANES), s0, a_pow, l)
    if not with_out:
        return None, x_fin
    y = _s5_out(y_intra, x_prev.reshape(4, bsz * nc * npair, LANES), w_out_pair, l, h, tr)
    return y.reshape(bsz, n, w5), x_fin


def _gla_direction(fr, q, v, lb, s_ref, o_ref, row0, *, blk, reverse, heads):
    n = blk // HG_CHUNK
    width = heads * LANES
    f = lb + (1.0 - lb) * jax.nn.sigmoid(fr)
    k = 1.0 - f
    hi, lo = _split(jnp.log(f), 2)
    row = lax.broadcasted_iota(jnp.int32, (blk, blk), 0)
    col = lax.broadcasted_iota(jnp.int32, (blk, blk), 1)
    tri = jnp.where((col >= row) if reverse else (col <= row), 1.0, 0.0).astype(BF16)
    c = _dot(tri, hi) + _dot(tri, lo)

    def rows(i):
        return slice(blk - (i + 1) * HG_CHUNK, blk - i * HG_CHUNK) if reverse else slice(i * HG_CHUNK, (i + 1) * HG_CHUNK)

    def mem_order(chunks):
        return sorted(chunks, reverse=reverse)

    r = [jnp.zeros((1, width), F32)]
    for i in range(n):
        edge = rows(i).start if reverse else rows(i).stop - 1
        r.append(c[edge:edge + 1])

    def per_chunk(vals):
        return jnp.concatenate([jnp.broadcast_to(vals[i], (HG_CHUNK, width)) for i in mem_order(range(n))], axis=0)

    qs = q * jnp.exp(c - per_chunk(r[:n]))
    kdl = k * jnp.exp(per_chunk(r[1:]) - c)
    qi = (qs * per_chunk([jnp.exp(r[i]) for i in range(n)])).astype(BF16)
    kd = (kdl * per_chunk([jnp.exp(r[n] - r[i + 1]) for i in range(n)])).astype(BF16)
    kdlb = kdl.astype(BF16)
    hop = {(i, j): jnp.exp(r[i] - r[j + 1]) for j in range(n) for i in range(j, n)}
    dparts = _split(jnp.concatenate([r[n], jnp.zeros((SUBLANES - 1, width), F32)], axis=0), 3)
    ones8 = jnp.ones((SUBLANES, LANES), BF16)
    lr = lax.broadcasted_iota(jnp.int32, (HG_CHUNK, HG_CHUNK), 0)
    lc = lax.broadcasted_iota(jnp.int32, (HG_CHUNK, HG_CHUNK), 1)
    causal = (lc >= lr) if reverse else (lc <= lr)

    for hd in range(heads):
        sl = slice(hd * LANES, (hd + 1) * LANES)
        state = s_ref[hd]
        dcol = sum(_dot_t0(p[:, sl], ones8) for p in dparts)
        o_inter = _dot(qi[:, sl], state.astype(BF16))
        s_ref[hd] = jnp.exp(dcol) * state + _dot_t0(kd[:, sl], v[:, sl])
        acc = {i: o_inter[rows(i)] for i in range(n)}
        for j in range(n):
            queries = mem_order(range(j, n))
            lhs = jnp.concatenate([qs[rows(i), sl] * hop[i, j][:, sl] for i in queries], axis=0).astype(BF16)
            att = _dot_t1(lhs, kdlb[rows(j), sl])
            pieces = [att[a * HG_CHUNK:(a + 1) * HG_CHUNK] for a in range(len(queries))]
            dpos = queries.index(j)
            pieces[dpos] = jnp.where(causal, pieces[dpos], 0.0)
            o_j = _dot(jnp.concatenate(pieces, axis=0).astype(BF16), v[rows(j), sl])
            for a, i in enumerate(queries):
                acc[i] = acc[i] + o_j[a * HG_CHUNK:(a + 1) * HG_CHUNK]
        o_ref[0, row0:row0 + blk, sl] = jnp.concatenate([acc[i] for i in mem_order(range(n))], axis=0).astype(BF16)


def _gla_kernel(ff_ref, fb_ref, vf_ref, vb_ref, qf_ref, qb_ref, lb_ref, s0f_ref, s0b_ref,
                of_ref, ob_ref, sff_ref, sfb_ref, s_scr, *, blk, heads, nsub):
    j = pl.program_id(1)
    last = pl.num_programs(1) - 1

    @pl.when(j == 0)
    def _():
        s_scr[0] = s0f_ref[0]
        s_scr[1] = s0b_ref[0]

    for sb in range(nsub):
        rf, rb = sb * blk, (nsub - 1 - sb) * blk
        _gla_direction(ff_ref[0, rf:rf + blk], qf_ref[0, rf:rf + blk].astype(F32), vf_ref[0, rf:rf + blk], lb_ref[0:1],
                       s_scr.at[0], of_ref, rf, blk=blk, reverse=False, heads=heads)
        _gla_direction(fb_ref[0, rb:rb + blk], qb_ref[0, rb:rb + blk].astype(F32), vb_ref[0, rb:rb + blk], lb_ref[1:2],
                       s_scr.at[1], ob_ref, rb, blk=blk, reverse=True, heads=heads)

    @pl.when(j == last)
    def _():
        sff_ref[0] = s_scr[0]
        sfb_ref[0] = s_scr[1]


def _gla(fraw, v, q, lb_all, l, s0f, s0b):
    bsz, n, hgw = v.shape
    heads = hgw // LANES
    blk = min(n, 256)
    nsub = 2 if n % (2 * blk) == 0 else 1
    step = blk * nsub
    nstep = n // step
    fwd = lambda c: pl.BlockSpec((1, step, hgw), lambda b, j: (b, j, c))
    bwd = lambda c: pl.BlockSpec((1, step, hgw), lambda b, j: (b, nstep - 1 - j, c))
    st = pl.BlockSpec((1, heads, LANES, LANES), lambda b, j: (b, 0, 0, 0))
    return pl.pallas_call(
        functools.partial(_gla_kernel, blk=blk, heads=heads, nsub=nsub),
        grid=(bsz, nstep),
        in_specs=[fwd(0), bwd(1), fwd(0), bwd(0), fwd(0), bwd(0), _layer_spec(lb_all, l, 2), st, st],
        out_specs=[fwd(0), bwd(0), st, st],
        out_shape=[jax.ShapeDtypeStruct((bsz, n, hgw), BF16)] * 2
        + [jax.ShapeDtypeStruct((bsz, heads, LANES, LANES), F32)] * 2,
        scratch_shapes=[pltpu.VMEM((2, heads, LANES, LANES), F32)],
        compiler_params=_cparams("arbitrary", "arbitrary"),
        name="gla",
    )(fraw, fraw, v, v, q, q, lb_all, s0f, s0b)


def _tail_kernel(x_ref, u_ref, y_ref, of_ref, ob_ref, g_ref, g1_ref, sh_ref, sc_ref, g2_ref,
                 d_ref, wg_ref, bg_ref, nw_ref, wo_ref, l1g_ref, l1b_ref,
                 wu_ref, cw_ref, cb_ref, wd_ref, l2g_ref, l2b_ref, o_ref, act_scr,
                 *, alpha, heads, row_w, tb, tf, dff, ctx_row):
    s5_y = jax.nn.gelu(y_ref[0] + u_ref[0] * d_ref[...])
    s5_out = s5_y * jax.nn.sigmoid(_dot(s5_y.astype(BF16), wg_ref[...]) + bg_ref[...])
    o = of_ref[0].astype(F32) + ob_ref[0].astype(F32)
    gate = g_ref[0].astype(F32)
    nw = nw_ref[...]
    mixed = [s5_out.astype(BF16)]
    for hd in range(heads):
        sl = slice(hd * LANES, (hd + 1) * LANES)
        oh = o[:, sl]
        ms = jnp.mean(oh * oh, axis=-1, keepdims=True)
        mixed.append((oh * lax.rsqrt(ms + RMS_EPS) * nw * gate[:, sl]).astype(BF16))
    proj = _dot(jnp.concatenate(mixed, axis=1), wo_ref[...])
    x = _layer_norm(alpha * x_ref[0] + _mod_row(g1_ref, ctx_row) * proj, l1g_ref[...], l1b_ref[...])

    h = (x * (1.0 + _mod_row(sc_ref, ctx_row)) + _mod_row(sh_ref, ctx_row)).astype(BF16)
    pos = lax.broadcasted_iota(jnp.int32, (tb, 1), 0) % row_w
    has_prev = pos != 0
    has_next = pos != row_w - 1

    def conv(up, c0):
        prev = jnp.where(has_prev, pltpu.roll(up, 1, 0), 0.0)
        nxt = jnp.where(has_next, pltpu.roll(up, tb - 1, 0), 0.0)
        cols = slice(c0, c0 + tf)
        return prev * cw_ref[0:1, cols] + up * cw_ref[1:2, cols] + nxt * cw_ref[2:3, cols] + cb_ref[:, cols]

    for t in range(dff // tf):
        a = conv(_dot(h, wu_ref[:, t * tf:(t + 1) * tf]), t * tf)
        g = conv(_dot(h, wu_ref[:, dff + t * tf:dff + (t + 1) * tf]), dff + t * tf)
        act_scr[:, t * tf:(t + 1) * tf] = (_silu(a) * g).astype(BF16)
    z = alpha * x + _mod_row(g2_ref, ctx_row) * _dot(act_scr[...], wd_ref[...])
    o_ref[0] = _layer_norm(z, l2g_ref[...], l2b_ref[...])


def _layer_tail(x, u, y5, o_f, o_b, g, mods, post_consts, ffn_consts, l, alpha, row_w, ctx_row):
    bsz, n, d = x.shape
    w5 = u.shape[-1]
    hgw = g.shape[-1]
    dff = ffn_consts[3].shape[1]
    tf = 256
    tb = min(n, 512)
    tok = lambda c: pl.BlockSpec((1, tb, c), lambda b, i: (b, i, 0))
    return pl.pallas_call(
        functools.partial(_tail_kernel, alpha=alpha, heads=hgw // LANES, row_w=row_w, tb=tb, tf=tf, dff=dff,
                          ctx_row=ctx_row),
        grid=(bsz, n // tb),
        in_specs=[tok(d), tok(w5), tok(w5), tok(hgw), tok(hgw), tok(hgw)]
        + [_mod_spec(mods, l, i, d) for i in (2, 3, 4, 5)]
        + [_layer_spec(a, l, 2, True) for a in post_consts + ffn_consts],
        out_specs=tok(d),
        out_shape=jax.ShapeDtypeStruct((bsz, n, d), F32),
        scratch_shapes=[pltpu.VMEM((tb, dff), BF16)],
        compiler_params=_cparams("arbitrary", "arbitrary"),
        name="layer_tail",
    )(x, u, y5, o_f, o_b, g, mods, mods, mods, mods, *post_consts, *ffn_consts)


def _token_mixer(h_in, mods, w_in, filters, lb_all, l, init, with_out, w5, hg, ctx_row):
    u, fraw, v, q, g = _inproj(h_in, mods, w_in, l, w5, hg, ctx_row)
    s5_init, hg_init = init
    y5, s5_fin = _s5_mixer(u, filters, l, s5_init, with_out)
    o_f, o_b, hg_fin_f, hg_fin_b = _gla(fraw, v, q, lb_all, l, hg_init[0], hg_init[1])
    return (u, y5, o_f, o_b, g), (s5_fin, (hg_fin_f, hg_fin_b))


def kernel(x, c, ctx, c_ctx, w_mod, b_mod, w_in, s5_lam_re, s5_lam_im, s5_log_dt, s5_b_re, s5_b_im,
           s5_c_re, s5_c_im, s5_d, w_glu, b_glu, hg_lb, hg_norm_w, w_out, ln1_g, ln1_b,
           w_up, conv_w, conv_b, w_down, ln2_g, ln2_b):
    depth = w_mod.shape[0]
    bsz, n, d = x.shape
    n_ctx = ctx.shape[1]
    w5 = s5_d.shape[-1]
    hg = hg_lb.shape[-1]
    heads = hg // LANES
    npair = s5_lam_re.shape[2] // 2
    alpha = (2 * depth) ** 0.25

    lb_all = jnp.cumsum(jax.nn.softmax(hg_lb.astype(F32), axis=0), axis=0)
    lb_all = lb_all - lb_all[:1]

    rb = -(-(bsz + 1) // SUBLANES) * SUBLANES
    rows = jnp.concatenate([c, c_ctx[None], jnp.zeros((rb - bsz - 1, d), F32)], axis=0)
    mods = _mod_vectors(rows, w_mod, b_mod)

    vec = lambda a: a.reshape(depth, 1, a.shape[-1])
    w_in_b = w_in.astype(BF16)
    post_consts = [vec(s5_d), w_glu.astype(BF16), vec(b_glu), vec(hg_norm_w), w_out.astype(BF16), vec(ln1_g), vec(ln1_b)]
    ffn_consts = [w_up.astype(BF16), conv_w, vec(conv_b), w_down.astype(BF16), vec(ln2_g), vec(ln2_b)]

    bbt, caf, cab, bbp, ccp, pwp, a_pow = jax.vmap(_s5_filter_inputs)(
        s5_lam_re, s5_lam_im, s5_log_dt, s5_b_re, s5_b_im, s5_c_re, s5_c_im)
    filters = (_s5_toeplitz(bbt, caf, cab),) + tuple(_s5_state_mats(bbp, ccp, pwp)) + (a_pow,)

    zero_init = (jnp.zeros((4, bsz, npair, LANES), F32),
                 (jnp.zeros((bsz, heads, LANES, LANES), F32),) * 2)

    for l in range(depth):
        last = l == depth - 1
        c_parts, ctx_states = _token_mixer(ctx, mods, w_in_b, filters, lb_all, l, zero_init, not last, w5, hg, bsz)
        x_parts, _ = _token_mixer(x, mods, w_in_b, filters, lb_all, l, ctx_states, True, w5, hg, None)
        x = _layer_tail(x, *x_parts, mods, post_consts, ffn_consts, l, alpha, GRID_W, None)
        if not last:
            ctx = _layer_tail(ctx, *c_parts, mods, post_consts, ffn_consts, l, alpha, n_ctx, bsz)
    return x
```

```python
import functools

import jax
import jax.numpy as jnp
from jax import lax
from jax.experimental import pallas as pl
from jax.experimental.pallas import tpu as pltpu

F32 = jnp.float32
BF16 = jnp.bfloat16

GRID_W = 64
HG_CHUNK = 32
S5_CHUNK = 16
LN_EPS = 1e-5
RMS_EPS = 1e-6
LANES = 128
SUBLANES = 8
GRAN = LANES // SUBLANES
VMEM_LIMIT = 56 * 1024 * 1024


def _cparams(*sem):
    return pltpu.CompilerParams(dimension_semantics=sem, vmem_limit_bytes=VMEM_LIMIT)


def _silu(x):
    return x * jax.nn.sigmoid(x)


def _dot(a, b):
    return jnp.dot(a, b, preferred_element_type=F32)


def _dot_t0(a, b):
    return lax.dot_general(a, b, (((0,), (0,)), ((), ())), preferred_element_type=F32)


def _dot_t1(a, b):
    return lax.dot_general(a, b, (((1,), (1,)), ((), ())), preferred_element_type=F32)


def _split(x, parts):
    out = []
    for _ in range(parts - 1):
        piece = x.astype(BF16)
        out.append(piece)
        x = x - piece.astype(F32)
    out.append(x.astype(BF16))
    return out


def _hp_dot_t1(a, b):
    ah, al = _split(a, 2)
    bh, bl = _split(b, 2)
    return _dot_t1(ah, bh) + _dot_t1(ah, bl) + _dot_t1(al, bh)


def _layer_norm(z, g, b):
    mu = jnp.mean(z, axis=-1, keepdims=True)
    zc = z - mu
    var = jnp.mean(zc * zc, axis=-1, keepdims=True)
    return zc * lax.rsqrt(var + LN_EPS) * g + b


def _layer_spec(a, l, ngrid, single=False):
    mode = dict(pipeline_mode=pl.Buffered(1)) if single else {}
    return pl.BlockSpec((None,) + a.shape[1:], lambda *_: (l,) + (0,) * (a.ndim - 1), **mode)


def _mod_spec(mods, l, chunk, d):
    return pl.BlockSpec((None, mods.shape[1], d), lambda *_: (l, 0, chunk))


def _mod_row(m_ref, ctx_row):
    row = pl.program_id(0) if ctx_row is None else ctx_row
    return m_ref[pl.ds(row, 1), :]


def _mod_kernel(c_ref, w_ref, b_ref, o_ref):
    s = _silu(c_ref[...]).astype(BF16)
    o_ref[0] = _dot(s, w_ref[0].astype(BF16)) + b_ref[0]


def _mod_vectors(rows, w_mod, b_mod):
    depth, d, d6 = w_mod.shape
    rb = rows.shape[0]
    tn = 1536 if d6 % 1536 == 0 else d6
    return pl.pallas_call(
        _mod_kernel,
        grid=(depth, d6 // tn),
        in_specs=[
            pl.BlockSpec((rb, d), lambda l, j: (0, 0)),
            pl.BlockSpec((1, d, tn), lambda l, j: (l, 0, j)),
            pl.BlockSpec((1, 1, tn), lambda l, j: (l, 0, j)),
        ],
        out_specs=pl.BlockSpec((1, rb, tn), lambda l, j: (l, 0, j)),
        out_shape=jax.ShapeDtypeStruct((depth, rb, d6), F32),
        compiler_params=_cparams("arbitrary", "arbitrary"),
        name="mod_vectors",
    )(rows, w_mod, b_mod.reshape(depth, 1, d6))


def _inproj_kernel(x_ref, sh_ref, sc_ref, w_ref, u_ref, f_ref, v_ref, q_ref, g_ref, *, w5, hg, ctx_row):
    h = (x_ref[0] * (1.0 + _mod_row(sc_ref, ctx_row)) + _mod_row(sh_ref, ctx_row)).astype(BF16)
    o = 0
    u_ref[0] = _dot(h, w_ref[:, o:o + w5]); o += w5
    f_ref[0] = _dot(h, w_ref[:, o:o + 2 * hg]); o += 2 * hg
    v_ref[0] = _dot(h, w_ref[:, o:o + hg]).astype(BF16); o += hg
    q_ref[0] = _silu(_dot(h, w_ref[:, o:o + hg])).astype(BF16); o += hg
    g_ref[0] = _silu(_dot(h, w_ref[:, o:o + hg])).astype(BF16)


def _inproj(x, mods, w_in, l, w5, hg, ctx_row):
    bsz, n, d = x.shape
    tb = min(n, 1024)
    tok = lambda c: pl.BlockSpec((1, tb, c), lambda b, j: (b, j, 0))
    return pl.pallas_call(
        functools.partial(_inproj_kernel, w5=w5, hg=hg, ctx_row=ctx_row),
        grid=(bsz, n // tb),
        in_specs=[tok(d), _mod_spec(mods, l, 0, d), _mod_spec(mods, l, 1, d), _layer_spec(w_in, l, 2, True)],
        out_specs=[tok(w5), tok(2 * hg), tok(hg), tok(hg), tok(hg)],
        out_shape=[jax.ShapeDtypeStruct((bsz, n, c), t)
                   for c, t in ((w5, F32), (2 * hg, F32), (hg, BF16), (hg, BF16), (hg, BF16))],
        compiler_params=_cparams("arbitrary", "arbitrary"),
        name="inproj",
    )(x, mods, mods, w_in)


def _slot_masks(shape):
    lane = lax.broadcasted_iota(jnp.int32, shape, len(shape) - 1)
    return [(lane // GRAN) == s for s in range(SUBLANES)]


def _row_copies(hbm_ref, buf_ref, sem_ref, step, slot, tr, to_hbm):
    out = []
    for l in range(S5_CHUNK):
        hbm = hbm_ref.at[pl.ds(step * tr, tr), l, :]
        vm = buf_ref.at[slot, l]
        out.append(pltpu.make_async_copy(vm, hbm, sem_ref.at[slot, l]) if to_hbm
                   else pltpu.make_async_copy(hbm, vm, sem_ref.at[slot, l]))
    return out


def _s5_in_kernel(u_hbm, t_ref, w_ref, yi_ref, s_ref, buf, sem, uf_scr, *, tr, h, nsteps):
    i = pl.program_id(0)
    slot = i % 2
    npair = uf_scr.shape[0]
    gw = S5_CHUNK * h
    gpb = LANES // h

    @pl.when(i == 0)
    def _():
        for cp in _row_copies(u_hbm, buf, sem, 0, 0, tr, False):
            cp.start()

    for cp in _row_copies(u_hbm, buf, sem, i, slot, tr, False):
        cp.wait()

    @pl.when(i + 1 < nsteps)
    def _():
        for cp in _row_copies(u_hbm, buf, sem, i + 1, 1 - slot, tr, False):
            cp.start()

    masks = _slot_masks((tr, LANES))
    for blk in range(buf.shape[-1] // LANES):
        for t in range(S5_CHUNK // SUBLANES):
            rot = []
            for l8 in range(SUBLANES):
                x = buf[slot, t * SUBLANES + l8, :, blk * LANES:(blk + 1) * LANES]
                rot.append(pltpu.roll(x, l8 * h, 1) if l8 else x)
            for gi in range(gpb):
                g = blk * gpb + gi
                dest = rot[0]
                for l8 in range(1, SUBLANES):
                    dest = jnp.where(masks[(gi + l8) % SUBLANES], rot[l8], dest)
                c0 = (g % 2) * gw + t * LANES
                uf_scr[g // 2, :, c0:c0 + LANES] = dest
    for p in range(npair):
        ub = uf_scr[p].astype(BF16)
        s = _dot(ub, w_ref[p])
        for k in range(4):
            s_ref[k, pl.ds(p, tr, stride=npair), :] = s[:, k * LANES:(k + 1) * LANES]
        yi_ref[p] = jnp.concatenate([_dot(ub[:, a * gw:(a + 1) * gw], t_ref[p, a]) for a in range(2)],
                                    axis=1).astype(BF16)


def _s5_in(u3, toep, w_in_pair, l, h, tr):
    r, ell, w5 = u3.shape
    _, npair, _, gw, _ = toep.shape
    nsteps = r // tr
    return pl.pallas_call(
        functools.partial(_s5_in_kernel, tr=tr, h=h, nsteps=nsteps),
        grid=(nsteps,),
        in_specs=[pl.BlockSpec(memory_space=pl.ANY), _layer_spec(toep, l, 1, True), _layer_spec(w_in_pair, l, 1, True)],
        out_specs=[
            pl.BlockSpec((npair, tr, 2 * gw), lambda i: (0, i, 0)),
            pl.BlockSpec((4, tr * npair, LANES), lambda i: (0, i, 0)),
        ],
        out_shape=[
            jax.ShapeDtypeStruct((npair, r, 2 * gw), BF16),
            jax.ShapeDtypeStruct((4, r * npair, LANES), F32),
        ],
        scratch_shapes=[pltpu.VMEM((2, ell, tr, w5), F32), pltpu.SemaphoreType.DMA((2, ell)),
                        pltpu.VMEM((npair, tr, 2 * gw), F32)],
        compiler_params=_cparams("arbitrary"),
        name="s5_in",
    )(u3, toep, w_in_pair)


def _s5_scan_kernel(s_ref, s0_ref, a_ref, xp_ref, xf_ref, *, nc):
    far, fai, bar, bai = a_ref[0], a_ref[1], a_ref[2], a_ref[3]

    def body(i, carry):
        fr, fi, br, bi = carry
        ib = nc - 1 - i
        xp_ref[0, 0, i] = fr
        xp_ref[1, 0, i] = fi
        xp_ref[2, 0, ib] = br
        xp_ref[3, 0, ib] = bi
        nfr = far * fr - fai * fi + s_ref[0, 0, i]
        nfi = far * fi + fai * fr + s_ref[1, 0, i]
        nbr = bar * br - bai * bi + s_ref[2, 0, ib]
        nbi = bar * bi + bai * br + s_ref[3, 0, ib]
        return nfr, nfi, nbr, nbi

    init = (s0_ref[0, 0], s0_ref[1, 0], s0_ref[2, 0], s0_ref[3, 0])
    fr, fi, br, bi = lax.fori_loop(0, nc, body, init, unroll=4)
    xf_ref[0, 0] = fr
    xf_ref[1, 0] = fi
    xf_ref[2, 0] = br
    xf_ref[3, 0] = bi


def _s5_scan(s_loc, s0, a_pow, l):
    _, bsz, nc, npair, _ = s_loc.shape
    return pl.pallas_call(
        functools.partial(_s5_scan_kernel, nc=nc),
        grid=(bsz,),
        in_specs=[
            pl.BlockSpec((4, 1, nc, npair, LANES), lambda b: (0, b, 0, 0, 0)),
            pl.BlockSpec((4, 1, npair, LANES), lambda b: (0, b, 0, 0)),
            _layer_spec(a_pow, l, 1),
        ],
        out_specs=[
            pl.BlockSpec((4, 1, nc, npair, LANES), lambda b: (0, b, 0, 0, 0)),
            pl.BlockSpec((4, 1, npair, LANES), lambda b: (0, b, 0, 0)),
        ],
        out_shape=[
            jax.ShapeDtypeStruct(s_loc.shape, F32),
            jax.ShapeDtypeStruct((4, bsz, npair, LANES), F32),
        ],
        compiler_params=_cparams("arbitrary"),
        name="s5_scan",
    )(s_loc, s0, a_pow)


def _s5_out_kernel(yi_ref, xp_ref, w_ref, y_hbm, yf_scr, buf, sem, *, tr, h, nsteps):
    i = pl.program_id(0)
    npair = yi_ref.shape[0]
    gw = S5_CHUNK * h
    gpb = LANES // h
    for p in range(npair):
        xcat = jnp.concatenate([xp_ref[k, pl.ds(p, tr, stride=npair), :] for k in range(4)], axis=1)
        yf_scr[p] = yi_ref[p].astype(F32) + _dot(xcat.astype(BF16), w_ref[p])

    @pl.when(i > 0)
    def _():
        for cp in _row_copies(y_hbm, buf, sem, i - 1, 0, tr, True):
            cp.wait()

    masks = _slot_masks((tr, LANES))
    for blk in range(npair * 2 * h // LANES):
        for t in range(S5_CHUNK // SUBLANES):
            src = []
            for gi in range(gpb):
                g = blk * gpb + gi
                c0 = (g % 2) * gw + t * LANES
                src.append(yf_scr[g // 2, :, c0:c0 + LANES])
            for l8 in range(SUBLANES):
                m = src[0]
                for gi in range(1, gpb):
                    m = jnp.where(masks[(gi + l8) % SUBLANES], src[gi], m)
                buf[0, t * SUBLANES + l8, :, blk * LANES:(blk + 1) * LANES] = (
                    pltpu.roll(m, LANES - l8 * h, 1) if l8 else m)
    copies = _row_copies(y_hbm, buf, sem, i, 0, tr, True)
    for cp in copies:
        cp.start()

    @pl.when(i == nsteps - 1)
    def _():
        for cp in copies:
            cp.wait()


def _s5_out(y_intra, x_prev, w_out_pair, l, h, tr):
    npair, r, width = y_intra.shape
    w5 = npair * 2 * h
    nsteps = r // tr
    return pl.pallas_call(
        functools.partial(_s5_out_kernel, tr=tr, h=h, nsteps=nsteps),
        grid=(nsteps,),
        in_specs=[
            pl.BlockSpec((npair, tr, width), lambda i: (0, i, 0)),
            pl.BlockSpec((4, tr * npair, LANES), lambda i: (0, i, 0)),
            _layer_spec(w_out_pair, l, 1, True),
        ],
        out_specs=pl.BlockSpec(memory_space=pl.ANY),
        out_shape=jax.ShapeDtypeStruct((r, S5_CHUNK, w5), F32),
        scratch_shapes=[pltpu.VMEM((npair, tr, width), F32), pltpu.VMEM((1, S5_CHUNK, tr, w5), F32),
                        pltpu.SemaphoreType.DMA((1, S5_CHUNK))],
        compiler_params=_cparams("arbitrary"),
        name="s5_out",
    )(y_intra, x_prev, w_out_pair)


def _filters_kernel(bb_ref, cc_ref, pw_ref, t_ref, win_ref, wout_ref, ca_scr, t_scr, *, h, ppb):
    ell = S5_CHUNK
    gw = ell * h
    lane = lax.broadcasted_iota(jnp.int32, (h, gw), 1)
    for pi in range(ppb):
        for a in range(2):
            gm = 2 * pi + a
            lag = []
            for d in range(2):
                cr, ci = cc_ref[0, d, pi, a], cc_ref[1, d, pi, a]
                for j in range(ell):
                    tau = j if d == 0 else ell - 1 - j
                    pr, pim = pw_ref[0, d, pi, tau:tau + 1, :], pw_ref[1, d, pi, tau:tau + 1, :]
                    ca_scr[0, j * h:(j + 1) * h, :] = cr * pr - ci * pim
                    ca_scr[1, j * h:(j + 1) * h, :] = cr * pim + ci * pr
                lag.append(_hp_dot_t1(bb_ref[0, d, pi, a], ca_scr[0]) - _hp_dot_t1(bb_ref[1, d, pi, a], ca_scr[1]))
            for lp in range(ell):
                sf, sb = lp * h, (ell - 1 - lp) * h
                fwd = lag[0] if sf == 0 else jnp.where(lane >= sf, pltpu.roll(lag[0], sf, 1), 0.0)
                bwd = lag[1] if sb == 0 else jnp.where(lane < gw - sb, pltpu.roll(lag[1], gw - sb, 1), 0.0)
                row = fwd + bwd
                if gm:
                    row = jnp.concatenate([pltpu.roll(row[:, t * LANES:(t + 1) * LANES], gm * h, 1)
                                           for t in range(gw // LANES)], axis=1)
                r0 = (lp // SUBLANES) * LANES + ((gm + lp) % SUBLANES) * h
                t_ref[pi, a, r0:r0 + h, :] = row.astype(BF16)
            for d in range(2):
                for l in range(ell):
                    r0 = a * gw + (l // SUBLANES) * LANES + ((gm + l) % SUBLANES) * h
                    t_in = ell - 1 - l if d == 0 else l
                    t_out = l + 1 if d == 0 else ell - l
                    pr, pim = pw_ref[0, d, pi, t_in:t_in + 1, :], pw_ref[1, d, pi, t_in:t_in + 1, :]
                    br, bi = bb_ref[0, d, pi, a], bb_ref[1, d, pi, a]
                    win_ref[pi, r0:r0 + h, (2 * d) * LANES:(2 * d + 1) * LANES] = (pr * br - pim * bi).astype(BF16)
                    win_ref[pi, r0:r0 + h, (2 * d + 1) * LANES:(2 * d + 2) * LANES] = (pr * bi + pim * br).astype(BF16)
                    pr, pim = pw_ref[0, d, pi, t_out:t_out + 1, :], pw_ref[1, d, pi, t_out:t_out + 1, :]
                    cr, ci = cc_ref[0, d, pi, a], cc_ref[1, d, pi, a]
                    t_scr[r0:r0 + h, (2 * d) * LANES:(2 * d + 1) * LANES] = cr * pr - ci * pim
                    t_scr[r0:r0 + h, (2 * d + 1) * LANES:(2 * d + 2) * LANES] = -(cr * pim + ci * pr)
        wout_ref[pi] = t_scr[...].T.astype(BF16)


def _s5_filters(bbp, ccp, pwp):
    depth, _, _, npair, _, h, _ = bbp.shape
    ell1 = pwp.shape[-2]
    ppb = LANES // h // 2
    gw = S5_CHUNK * h
    width = 2 * gw
    return pl.pallas_call(
        functools.partial(_filters_kernel, h=h, ppb=ppb),
        grid=(depth, npair // ppb),
        in_specs=[
            pl.BlockSpec((None, 2, 2, ppb, 2, h, LANES), lambda l, b: (l, 0, 0, b, 0, 0, 0)),
            pl.BlockSpec((None, 2, 2, ppb, 2, h, LANES), lambda l, b: (l, 0, 0, b, 0, 0, 0)),
            pl.BlockSpec((None, 2, 2, ppb, ell1, LANES), lambda l, b: (l, 0, 0, b, 0, 0)),
        ],
        out_specs=[pl.BlockSpec((None, ppb, 2, gw, gw), lambda l, b: (l, b, 0, 0, 0)),
                   pl.BlockSpec((None, ppb, width, 4 * LANES), lambda l, b: (l, b, 0, 0)),
                   pl.BlockSpec((None, ppb, 4 * LANES, width), lambda l, b: (l, b, 0, 0))],
        out_shape=[jax.ShapeDtypeStruct((depth, npair, 2, gw, gw), BF16),
                   jax.ShapeDtypeStruct((depth, npair, width, 4 * LANES), BF16),
                   jax.ShapeDtypeStruct((depth, npair, 4 * LANES, width), BF16)],
        scratch_shapes=[pltpu.VMEM((2, gw, LANES), F32), pltpu.VMEM((width, 4 * LANES), F32)],
        compiler_params=_cparams("arbitrary", "arbitrary"),
        name="s5_filters",
    )(bbp, ccp, pwp)


def _pair_halves(w):
    *lead, g, h, p = w.shape
    w = w.reshape(*lead, g // 2, 2, h, p)
    z = jnp.zeros_like(w[..., 0, :, :])
    return jnp.stack([jnp.concatenate([w[..., 0, :, :], z], axis=-1),
                      jnp.concatenate([z, w[..., 1, :, :]], axis=-1)], axis=-3)


def _s5_filter_inputs(lam_re, lam_im, log_dt, b_re, b_im, c_re, c_im):
    ndir, g, p = lam_re.shape
    h = b_re.shape[-1]
    ell = S5_CHUNK
    npair = g // 2
    lr, li = lam_re.astype(F32), lam_im.astype(F32)
    dt = jnp.exp(log_dt.astype(F32))[..., None]
    mag, ang = jnp.exp(lr * dt), li * dt
    abar_re, abar_im = mag * jnp.cos(ang), mag * jnp.sin(ang)
    den = lr * lr + li * li
    nr, ni = abar_re - 1.0, abar_im
    coef_re = ((nr * lr + ni * li) / den)[..., None]
    coef_im = ((ni * lr - nr * li) / den)[..., None]
    bb_re = coef_re * b_re - coef_im * b_im
    bb_im = coef_re * b_im + coef_im * b_re
    tau = jnp.arange(ell + 1, dtype=F32)[:, None, None, None]
    pmag, pang = jnp.exp(tau * (lr * dt)), tau * (li * dt)
    pw_re, pw_im = pmag * jnp.cos(pang), pmag * jnp.sin(pang)
    bbp = jnp.stack([_pair_halves(b.transpose(0, 1, 3, 2)) for b in (bb_re, bb_im)])
    ccp = jnp.stack([_pair_halves(c) for c in (c_re.astype(F32), c_im.astype(F32))])
    pwp = jnp.stack([pw.reshape(ell + 1, ndir, npair, 2 * p).transpose(1, 2, 0, 3) for pw in (pw_re, pw_im)])
    a_pow = jnp.stack([pw[ell, d].reshape(npair, 2 * p) for d in range(2) for pw in (pw_re, pw_im)])
    return bbp, ccp, pwp, a_pow


def _s5_mixer(u, filters, l, s0, with_out):
    toep, w_in_pair, w_out_pair, a_pow = filters
    bsz, n, w5 = u.shape
    npair = toep.shape[1]
    h = w5 // (2 * npair)
    nc = n // S5_CHUNK
    tr = min(bsz * nc, 128)
    y_intra, s_loc = _s5_in(u.reshape(bsz * nc, S5_CHUNK, w5), toep, w_in_pair, l, h, tr)
    x_prev, x_fin = _s5_scan(s_loc.reshape(4, bsz, nc, npair, LANES), s0, a_pow, l)
    if not with_out:
        return None, x_fin
    y = _s5_out(y_intra, x_prev.reshape(4, bsz * nc * npair, LANES), w_out_pair, l, h, tr)
    return y.reshape(bsz, n, w5), x_fin


def _gla_direction(fr, q, v, lb, s_ref, o_ref, row0, *, blk, reverse, heads):
    n = blk // HG_CHUNK
    width = heads * LANES
    f = lb + (1.0 - lb) * jax.nn.sigmoid(fr)
    k = 1.0 - f
    hi, lo = _split(jnp.log(f), 2)
    row = lax.broadcasted_iota(jnp.int32, (blk, blk), 0)
    col = lax.broadcasted_iota(jnp.int32, (blk, blk), 1)
    tri = jnp.where((col >= row) if reverse else (col <= row), 1.0, 0.0).astype(BF16)
    c = _dot(tri, hi) + _dot(tri, lo)

    def rows(i):
        return slice(blk - (i + 1) * HG_CHUNK, blk - i * HG_CHUNK) if reverse else slice(i * HG_CHUNK, (i + 1) * HG_CHUNK)

    def mem_order(chunks):
        return sorted(chunks, reverse=reverse)

    r = [jnp.zeros((1, width), F32)]
    for i in range(n):
        edge = rows(i).start if reverse else rows(i).stop - 1
        r.append(c[edge:edge + 1])

    def per_chunk(vals):
        return jnp.concatenate([jnp.broadcast_to(vals[i], (HG_CHUNK, width)) for i in mem_order(range(n))], axis=0)

    qs = q * jnp.exp(c - per_chunk(r[:n]))
    kdl = k * jnp.exp(per_chunk(r[1:]) - c)
    qi = (qs * per_chunk([jnp.exp(r[i]) for i in range(n)])).astype(BF16)
    kd = (kdl * per_chunk([jnp.exp(r[n] - r[i + 1]) for i in range(n)])).astype(BF16)
    kdlb = kdl.astype(BF16)
    hop = {(i, j): jnp.exp(r[i] - r[j + 1]) for j in range(n) for i in range(j, n)}
    dparts = _split(jnp.concatenate([r[n], jnp.zeros((SUBLANES - 1, width), F32)], axis=0), 3)
    ones8 = jnp.ones((SUBLANES, LANES), BF16)
    lr = lax.broadcasted_iota(jnp.int32, (HG_CHUNK, HG_CHUNK), 0)
    lc = lax.broadcasted_iota(jnp.int32, (HG_CHUNK, HG_CHUNK), 1)
    causal = (lc >= lr) if reverse else (lc <= lr)

    for hd in range(heads):
        sl = slice(hd * LANES, (hd + 1) * LANES)
        state = s_ref[hd]
        dcol = sum(_dot_t0(p[:, sl], ones8) for p in dparts)
        o_inter = _dot(qi[:, sl], state.astype(BF16))
        s_ref[hd] = jnp.exp(dcol) * state + _dot_t0(kd[:, sl], v[:, sl])
        acc = {i: o_inter[rows(i)] for i in range(n)}
        for j in range(n):
            queries = mem_order(range(j, n))
            lhs = jnp.concatenate([qs[rows(i), sl] * hop[i, j][:, sl] for i in queries], axis=0).astype(BF16)
            att = _dot_t1(lhs, kdlb[rows(j), sl])
            pieces = [att[a * HG_CHUNK:(a + 1) * HG_CHUNK] for a in range(len(queries))]
            dpos = queries.index(j)
            pieces[dpos] = jnp.where(causal, pieces[dpos], 0.0)
            o_j = _dot(jnp.concatenate(pieces, axis=0).astype(BF16), v[rows(j), sl])
            for a, i in enumerate(queries):
                acc[i] = acc[i] + o_j[a * HG_CHUNK:(a + 1) * HG_CHUNK]
        o_ref[0, row0:row0 + blk, sl] = jnp.concatenate([acc[i] for i in mem_order(range(n))], axis=0).astype(BF16)


def _gla_kernel(ff_ref, fb_ref, vf_ref, vb_ref, qf_ref, qb_ref, lb_ref, s0f_ref, s0b_ref,
                of_ref, ob_ref, sff_ref, sfb_ref, s_scr, *, blk, heads, nsub):
    j = pl.program_id(1)
    last = pl.num_programs(1) - 1

    @pl.when(j == 0)
    def _():
        s_scr[0] = s0f_ref[0]
        s_scr[1] = s0b_ref[0]

    for sb in range(nsub):
        rf, rb = sb * blk, (nsub - 1 - sb) * blk
        _gla_direction(ff_ref[0, rf:rf + blk], qf_ref[0, rf:rf + blk].astype(F32), vf_ref[0, rf:rf + blk], lb_ref[0:1],
                       s_scr.at[0], of_ref, rf, blk=blk, reverse=False, heads=heads)
        _gla_direction(fb_ref[0, rb:rb + blk], qb_ref[0, rb:rb + blk].astype(F32), vb_ref[0, rb:rb + blk], lb_ref[1:2],
                       s_scr.at[1], ob_ref, rb, blk=blk, reverse=True, heads=heads)

    @pl.when(j == last)
    def _():
        sff_ref[0] = s_scr[0]
        sfb_ref[0] = s_scr[1]


def _gla(fraw, v, q, lb_all, l, s0f, s0b):
    bsz, n, hgw = v.shape
    heads = hgw // LANES
    blk = min(n, 128)
    nsub = 4 if n % (4 * blk) == 0 else 1
    step = blk * nsub
    nstep = n // step
    fwd = lambda c: pl.BlockSpec((1, step, hgw), lambda b, j: (b, j, c))
    bwd = lambda c: pl.BlockSpec((1, step, hgw), lambda b, j: (b, nstep - 1 - j, c))
    st = pl.BlockSpec((1, heads, LANES, LANES), lambda b, j: (b, 0, 0, 0))
    return pl.pallas_call(
        functools.partial(_gla_kernel, blk=blk, heads=heads, nsub=nsub),
        grid=(bsz, nstep),
        in_specs=[fwd(0), bwd(1), fwd(0), bwd(0), fwd(0), bwd(0), _layer_spec(lb_all, l, 2), st, st],
        out_specs=[fwd(0), bwd(0), st, st],
        out_shape=[jax.ShapeDtypeStruct((bsz, n, hgw), BF16)] * 2
        + [jax.ShapeDtypeStruct((bsz, heads, LANES, LANES), F32)] * 2,
        scratch_shapes=[pltpu.VMEM((2, heads, LANES, LANES), F32)],
        compiler_params=_cparams("arbitrary", "arbitrary"),
        name="gla",
    )(fraw, fraw, v, v, q, q, lb_all, s0f, s0b)


def _tail_kernel(x_ref, u_ref, y_ref, of_ref, ob_ref, g_ref, g1_ref, sh_ref, sc_ref, g2_ref,
                 d_ref, wg_ref, bg_ref, nw_ref, wo_ref, l1g_ref, l1b_ref,
                 wu_ref, cw_ref, cb_ref, wd_ref, l2g_ref, l2b_ref, o_ref, act_scr,
                 *, alpha, heads, row_w, tb, tf, dff, ctx_row):
    s5_y = jax.nn.gelu(y_ref[0] + u_ref[0] * d_ref[...])
    s5_out = s5_y * jax.nn.sigmoid(_dot(s5_y.astype(BF16), wg_ref[...]) + bg_ref[...])
    o = of_ref[0].astype(F32) + ob_ref[0].astype(F32)
    gate = g_ref[0].astype(F32)
    nw = nw_ref[...]
    mixed = [s5_out.astype(BF16)]
    for hd in range(heads):
        sl = slice(hd * LANES, (hd + 1) * LANES)
        oh = o[:, sl]
        ms = jnp.mean(oh * oh, axis=-1, keepdims=True)
        mixed.append((oh * lax.rsqrt(ms + RMS_EPS) * nw * gate[:, sl]).astype(BF16))
    proj = _dot(jnp.concatenate(mixed, axis=1), wo_ref[...])
    x = _layer_norm(alpha * x_ref[0] + _mod_row(g1_ref, ctx_row) * proj, l1g_ref[...], l1b_ref[...])

    h = (x * (1.0 + _mod_row(sc_ref, ctx_row)) + _mod_row(sh_ref, ctx_row)).astype(BF16)
    pos = lax.broadcasted_iota(jnp.int32, (tb, 1), 0) % row_w
    has_prev = pos != 0
    has_next = pos != row_w - 1

    def conv(up, c0):
        prev = jnp.where(has_prev, pltpu.roll(up, 1, 0), 0.0)
        nxt = jnp.where(has_next, pltpu.roll(up, tb - 1, 0), 0.0)
        cols = slice(c0, c0 + tf)
        return prev * cw_ref[0:1, cols] + up * cw_ref[1:2, cols] + nxt * cw_ref[2:3, cols] + cb_ref[:, cols]

    for t in range(dff // tf):
        a = conv(_dot(h, wu_ref[:, t * tf:(t + 1) * tf]), t * tf)
        g = conv(_dot(h, wu_ref[:, dff + t * tf:dff + (t + 1) * tf]), dff + t * tf)
        act_scr[:, t * tf:(t + 1) * tf] = (_silu(a) * g).astype(BF16)
    z = alpha * x + _mod_row(g2_ref, ctx_row) * _dot(act_scr[...], wd_ref[...])
    o_ref[0] = _layer_norm(z, l2g_ref[...], l2b_ref[...])


def _layer_tail(x, u, y5, o_f, o_b, g, mods, post_consts, ffn_consts, l, alpha, row_w, ctx_row):
    bsz, n, d = x.shape
    w5 = u.shape[-1]
    hgw = g.shape[-1]
    dff = ffn_consts[3].shape[1]
    tf = 256
    tb = min(n, 512)
    tok = lambda c: pl.BlockSpec((1, tb, c), lambda b, i: (b, i, 0))
    return pl.pallas_call(
        functools.partial(_tail_kernel, alpha=alpha, heads=hgw // LANES, row_w=row_w, tb=tb, tf=tf, dff=dff,
                          ctx_row=ctx_row),
        grid=(bsz, n // tb),
        in_specs=[tok(d), tok(w5), tok(w5), tok(hgw), tok(hgw), tok(hgw)]
        + [_mod_spec(mods, l, i, d) for i in (2, 3, 4, 5)]
        + [_layer_spec(a, l, 2, True) for a in post_consts + ffn_consts],
        out_specs=tok(d),
        out_shape=jax.ShapeDtypeStruct((bsz, n, d), F32),
        scratch_shapes=[pltpu.VMEM((tb, dff), BF16)],
        compiler_params=_cparams("arbitrary", "arbitrary"),
        name="layer_tail",
    )(x, u, y5, o_f, o_b, g, mods, mods, mods, mods, *post_consts, *ffn_consts)


def _token_mixer(h_in, mods, w_in, filters, lb_all, l, init, with_out, w5, hg, ctx_row):
    u, fraw, v, q, g = _inproj(h_in, mods, w_in, l, w5, hg, ctx_row)
    s5_init, hg_init = init
    y5, s5_fin = _s5_mixer(u, filters, l, s5_init, with_out)
    o_f, o_b, hg_fin_f, hg_fin_b = _gla(fraw, v, q, lb_all, l, hg_init[0], hg_init[1])
    return (u, y5, o_f, o_b, g), (s5_fin, (hg_fin_f, hg_fin_b))


def kernel(x, c, ctx, c_ctx, w_mod, b_mod, w_in, s5_lam_re, s5_lam_im, s5_log_dt, s5_b_re, s5_b_im,
           s5_c_re, s5_c_im, s5_d, w_glu, b_glu, hg_lb, hg_norm_w, w_out, ln1_g, ln1_b,
           w_up, conv_w, conv_b, w_down, ln2_g, ln2_b):
    depth = w_mod.shape[0]
    bsz, n, d = x.shape
    n_ctx = ctx.shape[1]
    w5 = s5_d.shape[-1]
    hg = hg_lb.shape[-1]
    heads = hg // LANES
    npair = s5_lam_re.shape[2] // 2
    alpha = (2 * depth) ** 0.25

    lb_all = jnp.cumsum(jax.nn.softmax(hg_lb.astype(F32), axis=0), axis=0)
    lb_all = lb_all - lb_all[:1]

    rb = -(-(bsz + 1) // SUBLANES) * SUBLANES
    rows = jnp.concatenate([c, c_ctx[None], jnp.zeros((rb - bsz - 1, d), F32)], axis=0)
    mods = _mod_vectors(rows, w_mod, b_mod)

    vec = lambda a: a.reshape(depth, 1, a.shape[-1])
    w_in_b = w_in.astype(BF16)
    post_consts = [vec(s5_d), w_glu.astype(BF16), vec(b_glu), vec(hg_norm_w), w_out.astype(BF16), vec(ln1_g), vec(ln1_b)]
    ffn_consts = [w_up.astype(BF16), conv_w, vec(conv_b), w_down.astype(BF16), vec(ln2_g), vec(ln2_b)]

    bbp, ccp, pwp, a_pow = jax.vmap(_s5_filter_inputs)(
        s5_lam_re, s5_lam_im, s5_log_dt, s5_b_re, s5_b_im, s5_c_re, s5_c_im)
    filters = tuple(_s5_filters(bbp, ccp, pwp)) + (a_pow,)

    zero_init = (jnp.zeros((4, bsz, npair, LANES), F32),
                 (jnp.zeros((bsz, heads, LANES, LANES), F32),) * 2)

    for l in range(depth):
        last = l == depth - 1
        c_parts, ctx_states = _token_mixer(ctx, mods, w_in_b, filters, lb_all, l, zero_init, not last, w5, hg, bsz)
        x_parts, _ = _token_mixer(x, mods, w_in_b, filters, lb_all, l, ctx_states, True, w5, hg, None)
        x = _layer_tail(x, *x_parts, mods, post_consts, ffn_consts, l, alpha, GRID_W, None)
        if not last:
            ctx = _layer_tail(ctx, *c_parts, mods, post_consts, ffn_consts, l, alpha, n_ctx, bsz)
    return x
```

```python
import functools
import math

import jax
import jax.numpy as jnp
from jax import lax
from jax.experimental import pallas as pl
from jax.experimental.pallas import tpu as pltpu

F32 = jnp.float32
BF16 = jnp.bfloat16

GRID_W = 64
HG_CHUNK = 32
S5_CHUNK = 16
LN_EPS = 1e-5
RMS_EPS = 1e-6
LANES = 128
SUBLANES = 8
GRAN = LANES // SUBLANES
VMEM_LIMIT = 56 * 1024 * 1024
INPROJ_TOKENS = 1024
TAIL_TOKENS = 512
FFN_TILE = 256
S5_ROWS = 128
HG_BLOCK = 256
HG_BLOCKS_PER_STEP = 4


def _cparams(*sem):
    return pltpu.CompilerParams(dimension_semantics=sem, vmem_limit_bytes=VMEM_LIMIT)


def _silu(x):
    return x * jax.nn.sigmoid(x)


def _dot(a, b):
    return jnp.dot(a, b, preferred_element_type=F32)


def _dot_t0(a, b):
    return lax.dot_general(a, b, (((0,), (0,)), ((), ())), preferred_element_type=F32)


def _dot_t1(a, b):
    return lax.dot_general(a, b, (((1,), (1,)), ((), ())), preferred_element_type=F32)


def _split(x, parts):
    out = []
    for _ in range(parts - 1):
        piece = x.astype(BF16)
        out.append(piece)
        x = x - piece.astype(F32)
    out.append(x.astype(BF16))
    return out


def _hp_dot_t1(a, b):
    ah, al = _split(a, 2)
    bh, bl = _split(b, 2)
    return _dot_t1(ah, bh) + _dot_t1(ah, bl) + _dot_t1(al, bh)


def _layer_norm(z, g, b):
    mu = jnp.mean(z, axis=-1, keepdims=True)
    zc = z - mu
    var = jnp.mean(zc * zc, axis=-1, keepdims=True)
    return zc * lax.rsqrt(var + LN_EPS) * g + b


def _layer_spec(a, l, single=False):
    mode = dict(pipeline_mode=pl.Buffered(1)) if single else {}
    return pl.BlockSpec((None,) + a.shape[1:], lambda *_: (l,) + (0,) * (a.ndim - 1), **mode)


def _mod_spec(mods, l, chunk, d):
    return pl.BlockSpec((None, mods.shape[1], d), lambda *_: (l, 0, chunk))


def _mod_row(m_ref, ctx_row):
    row = pl.program_id(0) if ctx_row is None else ctx_row
    return m_ref[pl.ds(row, 1), :]


def _mod_kernel(c_ref, w_ref, b_ref, o_ref):
    s = _silu(c_ref[...]).astype(BF16)
    o_ref[0] = _dot(s, w_ref[0].astype(BF16)) + b_ref[0]


def _mod_vectors(rows, w_mod, b_mod):
    depth, d, d6 = w_mod.shape
    rb = rows.shape[0]
    tn = 1536 if d6 % 1536 == 0 else d6
    return pl.pallas_call(
        _mod_kernel,
        grid=(depth, d6 // tn),
        in_specs=[
            pl.BlockSpec((rb, d), lambda l, j: (0, 0)),
            pl.BlockSpec((1, d, tn), lambda l, j: (l, 0, j)),
            pl.BlockSpec((1, 1, tn), lambda l, j: (l, 0, j)),
        ],
        out_specs=pl.BlockSpec((1, rb, tn), lambda l, j: (l, 0, j)),
        out_shape=jax.ShapeDtypeStruct((depth, rb, d6), F32),
        compiler_params=_cparams("arbitrary", "arbitrary"),
        name="mod_vectors",
    )(rows, w_mod, b_mod.reshape(depth, 1, d6))


def _inproj_kernel(x_ref, sh_ref, sc_ref, w_ref, u_ref, f_ref, v_ref, q_ref, g_ref, *, w5, hg, ctx_row):
    h = (x_ref[0] * (1.0 + _mod_row(sc_ref, ctx_row)) + _mod_row(sh_ref, ctx_row)).astype(BF16)
    o = 0
    u_ref[0] = _dot(h, w_ref[:, o:o + w5]); o += w5
    f_ref[0] = _dot(h, w_ref[:, o:o + 2 * hg]); o += 2 * hg
    v_ref[0] = _dot(h, w_ref[:, o:o + hg]).astype(BF16); o += hg
    q_ref[0] = _silu(_dot(h, w_ref[:, o:o + hg])).astype(BF16); o += hg
    g_ref[0] = _silu(_dot(h, w_ref[:, o:o + hg])).astype(BF16)


def _inproj(x, mods, w_in, l, w5, hg, ctx_row):
    bsz, n, d = x.shape
    tb = min(n, INPROJ_TOKENS)
    tok = lambda c: pl.BlockSpec((1, tb, c), lambda b, j: (b, j, 0))
    return pl.pallas_call(
        functools.partial(_inproj_kernel, w5=w5, hg=hg, ctx_row=ctx_row),
        grid=(bsz, n // tb),
        in_specs=[tok(d), _mod_spec(mods, l, 0, d), _mod_spec(mods, l, 1, d), _layer_spec(w_in, l, True)],
        out_specs=[tok(w5), tok(2 * hg), tok(hg), tok(hg), tok(hg)],
        out_shape=[jax.ShapeDtypeStruct((bsz, n, c), t)
                   for c, t in ((w5, F32), (2 * hg, F32), (hg, BF16), (hg, BF16), (hg, BF16))],
        compiler_params=_cparams("arbitrary", "arbitrary"),
        name="inproj",
    )(x, mods, mods, w_in)


def _slot_masks(shape):
    lane = lax.broadcasted_iota(jnp.int32, shape, len(shape) - 1)
    return [(lane // GRAN) == s for s in range(SUBLANES)]


def _row_copies(hbm_ref, buf_ref, sem_ref, step, slot, tr, to_hbm):
    out = []
    for l in range(S5_CHUNK):
        hbm = hbm_ref.at[pl.ds(step * tr, tr), l, :]
        vm = buf_ref.at[slot, l]
        out.append(pltpu.make_async_copy(vm, hbm, sem_ref.at[slot, l]) if to_hbm
                   else pltpu.make_async_copy(hbm, vm, sem_ref.at[slot, l]))
    return out


def _s5_in_kernel(u_hbm, t_ref, w_ref, yi_ref, s_ref, buf, sem, uf_scr, *, tr, h, nsteps):
    i = pl.program_id(0)
    slot = i % 2
    npair = uf_scr.shape[0]
    gw = S5_CHUNK * h
    gpb = LANES // h

    @pl.when(i == 0)
    def _():
        for cp in _row_copies(u_hbm, buf, sem, 0, 0, tr, False):
            cp.start()

    for cp in _row_copies(u_hbm, buf, sem, i, slot, tr, False):
        cp.wait()

    @pl.when(i + 1 < nsteps)
    def _():
        for cp in _row_copies(u_hbm, buf, sem, i + 1, 1 - slot, tr, False):
            cp.start()

    masks = _slot_masks((tr, LANES))
    for blk in range(buf.shape[-1] // LANES):
        for t in range(S5_CHUNK // SUBLANES):
            rot = []
            for l8 in range(SUBLANES):
                x = buf[slot, t * SUBLANES + l8, :, blk * LANES:(blk + 1) * LANES]
                rot.append(pltpu.roll(x, l8 * h, 1) if l8 else x)
            for gi in range(gpb):
                g = blk * gpb + gi
                dest = rot[0]
                for l8 in range(1, SUBLANES):
                    dest = jnp.where(masks[(gi + l8) % SUBLANES], rot[l8], dest)
                c0 = (g % 2) * gw + t * LANES
                uf_scr[g // 2, :, c0:c0 + LANES] = dest
    for p in range(npair):
        ub = uf_scr[p].astype(BF16)
        s = _dot(ub, w_ref[p])
        for k in range(4):
            s_ref[k, pl.ds(p, tr, stride=npair), :] = s[:, k * LANES:(k + 1) * LANES]
        yi_ref[p] = jnp.concatenate([_dot(ub[:, a * gw:(a + 1) * gw], t_ref[p, a]) for a in range(2)],
                                    axis=1).astype(BF16)


def _s5_in(u3, toep, w_in_pair, l, h, tr):
    r, ell, w5 = u3.shape
    _, npair, _, gw, _ = toep.shape
    nsteps = r // tr
    return pl.pallas_call(
        functools.partial(_s5_in_kernel, tr=tr, h=h, nsteps=nsteps),
        grid=(nsteps,),
        in_specs=[pl.BlockSpec(memory_space=pl.ANY), _layer_spec(toep, l, True), _layer_spec(w_in_pair, l, True)],
        out_specs=[
            pl.BlockSpec((npair, tr, 2 * gw), lambda i: (0, i, 0)),
            pl.BlockSpec((4, tr * npair, LANES), lambda i: (0, i, 0)),
        ],
        out_shape=[
            jax.ShapeDtypeStruct((npair, r, 2 * gw), BF16),
            jax.ShapeDtypeStruct((4, r * npair, LANES), F32),
        ],
        scratch_shapes=[pltpu.VMEM((2, ell, tr, w5), F32), pltpu.SemaphoreType.DMA((2, ell)),
                        pltpu.VMEM((npair, tr, 2 * gw), F32)],
        compiler_params=_cparams("arbitrary"),
        name="s5_in",
    )(u3, toep, w_in_pair)


def _s5_scan_kernel(s_ref, s0_ref, a_ref, xp_ref, xf_ref, *, nc):
    far, fai, bar, bai = a_ref[0], a_ref[1], a_ref[2], a_ref[3]

    def body(i, carry):
        fr, fi, br, bi = carry
        ib = nc - 1 - i
        xp_ref[0, 0, i] = fr
        xp_ref[1, 0, i] = fi
        xp_ref[2, 0, ib] = br
        xp_ref[3, 0, ib] = bi
        nfr = far * fr - fai * fi + s_ref[0, 0, i]
        nfi = far * fi + fai * fr + s_ref[1, 0, i]
        nbr = bar * br - bai * bi + s_ref[2, 0, ib]
        nbi = bar * bi + bai * br + s_ref[3, 0, ib]
        return nfr, nfi, nbr, nbi

    init = (s0_ref[0, 0], s0_ref[1, 0], s0_ref[2, 0], s0_ref[3, 0])
    fr, fi, br, bi = lax.fori_loop(0, nc, body, init, unroll=4)
    xf_ref[0, 0] = fr
    xf_ref[1, 0] = fi
    xf_ref[2, 0] = br
    xf_ref[3, 0] = bi


def _s5_scan(s_loc, s0, a_pow, l):
    _, bsz, nc, npair, _ = s_loc.shape
    return pl.pallas_call(
        functools.partial(_s5_scan_kernel, nc=nc),
        grid=(bsz,),
        in_specs=[
            pl.BlockSpec((4, 1, nc, npair, LANES), lambda b: (0, b, 0, 0, 0)),
            pl.BlockSpec((4, 1, npair, LANES), lambda b: (0, b, 0, 0)),
            _layer_spec(a_pow, l),
        ],
        out_specs=[
            pl.BlockSpec((4, 1, nc, npair, LANES), lambda b: (0, b, 0, 0, 0)),
            pl.BlockSpec((4, 1, npair, LANES), lambda b: (0, b, 0, 0)),
        ],
        out_shape=[
            jax.ShapeDtypeStruct(s_loc.shape, F32),
            jax.ShapeDtypeStruct((4, bsz, npair, LANES), F32),
        ],
        compiler_params=_cparams("arbitrary"),
        name="s5_scan",
    )(s_loc, s0, a_pow)


def _s5_out_kernel(yi_ref, xp_ref, w_ref, y_hbm, yf_scr, buf, sem, *, tr, h, nsteps):
    i = pl.program_id(0)
    npair = yi_ref.shape[0]
    gw = S5_CHUNK * h
    gpb = LANES // h
    for p in range(npair):
        xcat = jnp.concatenate([xp_ref[k, pl.ds(p, tr, stride=npair), :] for k in range(4)], axis=1)
        yf_scr[p] = yi_ref[p].astype(F32) + _dot(xcat.astype(BF16), w_ref[p])

    @pl.when(i > 0)
    def _():
        for cp in _row_copies(y_hbm, buf, sem, i - 1, 0, tr, True):
            cp.wait()

    masks = _slot_masks((tr, LANES))
    for blk in range(npair * 2 * h // LANES):
        for t in range(S5_CHUNK // SUBLANES):
            src = []
            for gi in range(gpb):
                g = blk * gpb + gi
                c0 = (g % 2) * gw + t * LANES
                src.append(yf_scr[g // 2, :, c0:c0 + LANES])
            for l8 in range(SUBLANES):
                m = src[0]
                for gi in range(1, gpb):
                    m = jnp.where(masks[(gi + l8) % SUBLANES], src[gi], m)
                buf[0, t * SUBLANES + l8, :, blk * LANES:(blk + 1) * LANES] = (
                    pltpu.roll(m, LANES - l8 * h, 1) if l8 else m)
    copies = _row_copies(y_hbm, buf, sem, i, 0, tr, True)
    for cp in copies:
        cp.start()

    @pl.when(i == nsteps - 1)
    def _():
        for cp in copies:
            cp.wait()


def _s5_out(y_intra, x_prev, w_out_pair, l, h, tr):
    npair, r, width = y_intra.shape
    w5 = npair * 2 * h
    nsteps = r // tr
    return pl.pallas_call(
        functools.partial(_s5_out_kernel, tr=tr, h=h, nsteps=nsteps),
        grid=(nsteps,),
        in_specs=[
            pl.BlockSpec((npair, tr, width), lambda i: (0, i, 0)),
            pl.BlockSpec((4, tr * npair, LANES), lambda i: (0, i, 0)),
            _layer_spec(w_out_pair, l, True),
        ],
        out_specs=pl.BlockSpec(memory_space=pl.ANY),
        out_shape=jax.ShapeDtypeStruct((r, S5_CHUNK, w5), F32),
        scratch_shapes=[pltpu.VMEM((npair, tr, width), F32), pltpu.VMEM((1, S5_CHUNK, tr, w5), F32),
                        pltpu.SemaphoreType.DMA((1, S5_CHUNK))],
        compiler_params=_cparams("arbitrary"),
        name="s5_out",
    )(y_intra, x_prev, w_out_pair)


def _filters_kernel(bb_ref, cc_ref, pw_ref, t_ref, win_ref, wout_ref, ca_scr, t_scr, *, h, ppb):
    ell = S5_CHUNK
    gw = ell * h
    lane = lax.broadcasted_iota(jnp.int32, (h, gw), 1)
    for pi in range(ppb):
        for a in range(2):
            gm = 2 * pi + a
            lag = []
            for d in range(2):
                cr, ci = cc_ref[0, d, pi, a], cc_ref[1, d, pi, a]
                for j in range(ell):
                    tau = j if d == 0 else ell - 1 - j
                    pr, pim = pw_ref[0, d, pi, tau:tau + 1, :], pw_ref[1, d, pi, tau:tau + 1, :]
                    ca_scr[0, j * h:(j + 1) * h, :] = cr * pr - ci * pim
                    ca_scr[1, j * h:(j + 1) * h, :] = cr * pim + ci * pr
                lag.append(_hp_dot_t1(bb_ref[0, d, pi, a], ca_scr[0]) - _hp_dot_t1(bb_ref[1, d, pi, a], ca_scr[1]))
            for lp in range(ell):
                sf, sb = lp * h, (ell - 1 - lp) * h
                fwd = lag[0] if sf == 0 else jnp.where(lane >= sf, pltpu.roll(lag[0], sf, 1), 0.0)
                bwd = lag[1] if sb == 0 else jnp.where(lane < gw - sb, pltpu.roll(lag[1], gw - sb, 1), 0.0)
                row = fwd + bwd
                if gm:
                    row = jnp.concatenate([pltpu.roll(row[:, t * LANES:(t + 1) * LANES], gm * h, 1)
                                           for t in range(gw // LANES)], axis=1)
                r0 = (lp // SUBLANES) * LANES + ((gm + lp) % SUBLANES) * h
                t_ref[pi, a, r0:r0 + h, :] = row.astype(BF16)
            for d in range(2):
                for l in range(ell):
                    r0 = a * gw + (l // SUBLANES) * LANES + ((gm + l) % SUBLANES) * h
                    t_in = ell - 1 - l if d == 0 else l
                    t_out = l + 1 if d == 0 else ell - l
                    pr, pim = pw_ref[0, d, pi, t_in:t_in + 1, :], pw_ref[1, d, pi, t_in:t_in + 1, :]
                    br, bi = bb_ref[0, d, pi, a], bb_ref[1, d, pi, a]
                    win_ref[pi, r0:r0 + h, (2 * d) * LANES:(2 * d + 1) * LANES] = (pr * br - pim * bi).astype(BF16)
                    win_ref[pi, r0:r0 + h, (2 * d + 1) * LANES:(2 * d + 2) * LANES] = (pr * bi + pim * br).astype(BF16)
                    pr, pim = pw_ref[0, d, pi, t_out:t_out + 1, :], pw_ref[1, d, pi, t_out:t_out + 1, :]
                    cr, ci = cc_ref[0, d, pi, a], cc_ref[1, d, pi, a]
                    t_scr[r0:r0 + h, (2 * d) * LANES:(2 * d + 1) * LANES] = cr * pr - ci * pim
                    t_scr[r0:r0 + h, (2 * d + 1) * LANES:(2 * d + 2) * LANES] = -(cr * pim + ci * pr)
        wout_ref[pi] = t_scr[...].T.astype(BF16)


def _s5_filters(bbp, ccp, pwp):
    depth, _, _, npair, _, h, _ = bbp.shape
    ell1 = pwp.shape[-2]
    ppb = LANES // h // 2
    gw = S5_CHUNK * h
    width = 2 * gw
    return pl.pallas_call(
        functools.partial(_filters_kernel, h=h, ppb=ppb),
        grid=(depth, npair // ppb),
        in_specs=[
            pl.BlockSpec((None, 2, 2, ppb, 2, h, LANES), lambda l, b: (l, 0, 0, b, 0, 0, 0)),
            pl.BlockSpec((None, 2, 2, ppb, 2, h, LANES), lambda l, b: (l, 0, 0, b, 0, 0, 0)),
            pl.BlockSpec((None, 2, 2, ppb, ell1, LANES), lambda l, b: (l, 0, 0, b, 0, 0)),
        ],
        out_specs=[pl.BlockSpec((None, ppb, 2, gw, gw), lambda l, b: (l, b, 0, 0, 0)),
                   pl.BlockSpec((None, ppb, width, 4 * LANES), lambda l, b: (l, b, 0, 0)),
                   pl.BlockSpec((None, ppb, 4 * LANES, width), lambda l, b: (l, b, 0, 0))],
        out_shape=[jax.ShapeDtypeStruct((depth, npair, 2, gw, gw), BF16),
                   jax.ShapeDtypeStruct((depth, npair, width, 4 * LANES), BF16),
                   jax.ShapeDtypeStruct((depth, npair, 4 * LANES, width), BF16)],
        scratch_shapes=[pltpu.VMEM((2, gw, LANES), F32), pltpu.VMEM((width, 4 * LANES), F32)],
        compiler_params=_cparams("arbitrary", "arbitrary"),
        name="s5_filters",
    )(bbp, ccp, pwp)


def _pair_halves(w):
    *lead, g, h, p = w.shape
    w = w.reshape(*lead, g // 2, 2, h, p)
    z = jnp.zeros_like(w[..., 0, :, :])
    return jnp.stack([jnp.concatenate([w[..., 0, :, :], z], axis=-1),
                      jnp.concatenate([z, w[..., 1, :, :]], axis=-1)], axis=-3)


def _s5_filter_inputs(lam_re, lam_im, log_dt, b_re, b_im, c_re, c_im):
    ndir, g, p = lam_re.shape
    h = b_re.shape[-1]
    ell = S5_CHUNK
    npair = g // 2
    lr, li = lam_re.astype(F32), lam_im.astype(F32)
    dt = jnp.exp(log_dt.astype(F32))[..., None]
    mag, ang = jnp.exp(lr * dt), li * dt
    abar_re, abar_im = mag * jnp.cos(ang), mag * jnp.sin(ang)
    den = lr * lr + li * li
    nr, ni = abar_re - 1.0, abar_im
    coef_re = ((nr * lr + ni * li) / den)[..., None]
    coef_im = ((ni * lr - nr * li) / den)[..., None]
    bb_re = coef_re * b_re - coef_im * b_im
    bb_im = coef_re * b_im + coef_im * b_re
    tau = jnp.arange(ell + 1, dtype=F32)[:, None, None, None]
    pmag, pang = jnp.exp(tau * (lr * dt)), tau * (li * dt)
    pw_re, pw_im = pmag * jnp.cos(pang), pmag * jnp.sin(pang)
    bbp = jnp.stack([_pair_halves(b.transpose(0, 1, 3, 2)) for b in (bb_re, bb_im)])
    ccp = jnp.stack([_pair_halves(c) for c in (c_re.astype(F32), c_im.astype(F32))])
    pwp = jnp.stack([pw.reshape(ell + 1, ndir, npair, 2 * p).transpose(1, 2, 0, 3) for pw in (pw_re, pw_im)])
    a_pow = jnp.stack([pw[ell, d].reshape(npair, 2 * p) for d in range(2) for pw in (pw_re, pw_im)])
    return bbp, ccp, pwp, a_pow


def _s5_mixer(u, filters, l, s0, with_out):
    toep, w_in_pair, w_out_pair, a_pow = filters
    bsz, n, w5 = u.shape
    npair = toep.shape[1]
    h = w5 // (2 * npair)
    nc = n // S5_CHUNK
    tr = min(bsz * nc, S5_ROWS)
    y_intra, s_loc = _s5_in(u.reshape(bsz * nc, S5_CHUNK, w5), toep, w_in_pair, l, h, tr)
    x_prev, x_fin = _s5_scan(s_loc.reshape(4, bsz, nc, npair, LANES), s0, a_pow, l)
    if not with_out:
        return None, x_fin
    y = _s5_out(y_intra, x_prev.reshape(4, bsz * nc * npair, LANES), w_out_pair, l, h, tr)
    return y.reshape(bsz, n, w5), x_fin


def _gla_direction(fr, q, v, lb, s_ref, o_ref, row0, *, blk, reverse, heads):
    n = blk // HG_CHUNK
    width = heads * LANES
    f = lb + (1.0 - lb) * jax.nn.sigmoid(fr)
    k = 1.0 - f
    hi, lo = _split(jnp.log(f), 2)
    row = lax.broadcasted_iota(jnp.int32, (blk, blk), 0)
    col = lax.broadcasted_iota(jnp.int32, (blk, blk), 1)
    tri = jnp.where((col >= row) if reverse else (col <= row), 1.0, 0.0).astype(BF16)
    c = _dot(tri, hi) + _dot(tri, lo)

    def rows(i):
        return slice(blk - (i + 1) * HG_CHUNK, blk - i * HG_CHUNK) if reverse else slice(i * HG_CHUNK, (i + 1) * HG_CHUNK)

    def mem_order(chunks):
        return sorted(chunks, reverse=reverse)

    r = [jnp.zeros((1, width), F32)]
    for i in range(n):
        edge = rows(i).start if reverse else rows(i).stop - 1
        r.append(c[edge:edge + 1])

    def per_chunk(vals):
        return jnp.concatenate([jnp.broadcast_to(vals[i], (HG_CHUNK, width)) for i in mem_order(range(n))], axis=0)

    qs = q * jnp.exp(c - per_chunk(r[:n]))
    kdl = k * jnp.exp(per_chunk(r[1:]) - c)
    qi = (qs * per_chunk([jnp.exp(r[i]) for i in range(n)])).astype(BF16)
    kd = (kdl * per_chunk([jnp.exp(r[n] - r[i + 1]) for i in range(n)])).astype(BF16)
    kdlb = kdl.astype(BF16)
    hop = {(i, j): jnp.exp(r[i] - r[j + 1]) for j in range(n) for i in range(j, n)}
    dparts = _split(jnp.concatenate([r[n], jnp.zeros((SUBLANES - 1, width), F32)], axis=0), 3)
    ones8 = jnp.ones((SUBLANES, LANES), BF16)
    lr = lax.broadcasted_iota(jnp.int32, (HG_CHUNK, HG_CHUNK), 0)
    lc = lax.broadcasted_iota(jnp.int32, (HG_CHUNK, HG_CHUNK), 1)
    causal = (lc >= lr) if reverse else (lc <= lr)

    for hd in range(heads):
        sl = slice(hd * LANES, (hd + 1) * LANES)
        state = s_ref[hd]
        dcol = sum(_dot_t0(p[:, sl], ones8) for p in dparts)
        o_inter = _dot(qi[:, sl], state.astype(BF16))
        s_ref[hd] = jnp.exp(dcol) * state + _dot_t0(kd[:, sl], v[:, sl])
        acc = {i: o_inter[rows(i)] for i in range(n)}
        for j in range(n):
            queries = mem_order(range(j, n))
            lhs = jnp.concatenate([qs[rows(i), sl] * hop[i, j][:, sl] for i in queries], axis=0).astype(BF16)
            att = _dot_t1(lhs, kdlb[rows(j), sl])
            pieces = [att[a * HG_CHUNK:(a + 1) * HG_CHUNK] for a in range(len(queries))]
            dpos = queries.index(j)
            pieces[dpos] = jnp.where(causal, pieces[dpos], 0.0)
            o_j = _dot(jnp.concatenate(pieces, axis=0).astype(BF16), v[rows(j), sl])
            for a, i in enumerate(queries):
                acc[i] = acc[i] + o_j[a * HG_CHUNK:(a + 1) * HG_CHUNK]
        o_ref[0, row0:row0 + blk, sl] = jnp.concatenate([acc[i] for i in mem_order(range(n))], axis=0).astype(BF16)


def _gla_kernel(ff_ref, fb_ref, vf_ref, vb_ref, qf_ref, qb_ref, lb_ref, s0f_ref, s0b_ref,
                of_ref, ob_ref, sff_ref, sfb_ref, s_scr, *, blk, heads, nsub):
    j = pl.program_id(1)
    last = pl.num_programs(1) - 1

    @pl.when(j == 0)
    def _():
        s_scr[0] = s0f_ref[0]
        s_scr[1] = s0b_ref[0]

    for sb in range(nsub):
        rf, rb = sb * blk, (nsub - 1 - sb) * blk
        _gla_direction(ff_ref[0, rf:rf + blk], qf_ref[0, rf:rf + blk].astype(F32), vf_ref[0, rf:rf + blk], lb_ref[0:1],
                       s_scr.at[0], of_ref, rf, blk=blk, reverse=False, heads=heads)
        _gla_direction(fb_ref[0, rb:rb + blk], qb_ref[0, rb:rb + blk].astype(F32), vb_ref[0, rb:rb + blk], lb_ref[1:2],
                       s_scr.at[1], ob_ref, rb, blk=blk, reverse=True, heads=heads)

    @pl.when(j == last)
    def _():
        sff_ref[0] = s_scr[0]
        sfb_ref[0] = s_scr[1]


def _gla(fraw, v, q, lb_all, l, s0f, s0b):
    bsz, n, hgw = v.shape
    heads = hgw // LANES
    blk = min(n, HG_BLOCK)
    nsub = math.gcd(n // blk, HG_BLOCKS_PER_STEP)
    step = blk * nsub
    nstep = n // step
    fwd = lambda c: pl.BlockSpec((1, step, hgw), lambda b, j: (b, j, c))
    bwd = lambda c: pl.BlockSpec((1, step, hgw), lambda b, j: (b, nstep - 1 - j, c))
    st = pl.BlockSpec((1, heads, LANES, LANES), lambda b, j: (b, 0, 0, 0))
    return pl.pallas_call(
        functools.partial(_gla_kernel, blk=blk, heads=heads, nsub=nsub),
        grid=(bsz, nstep),
        in_specs=[fwd(0), bwd(1), fwd(0), bwd(0), fwd(0), bwd(0), _layer_spec(lb_all, l), st, st],
        out_specs=[fwd(0), bwd(0), st, st],
        out_shape=[jax.ShapeDtypeStruct((bsz, n, hgw), BF16)] * 2
        + [jax.ShapeDtypeStruct((bsz, heads, LANES, LANES), F32)] * 2,
        scratch_shapes=[pltpu.VMEM((2, heads, LANES, LANES), F32)],
        compiler_params=_cparams("arbitrary", "arbitrary"),
        name="gla",
    )(fraw, fraw, v, v, q, q, lb_all, s0f, s0b)


def _tail_kernel(x_ref, u_ref, y_ref, of_ref, ob_ref, g_ref, g1_ref, sh_ref, sc_ref, g2_ref,
                 d_ref, wg_ref, bg_ref, nw_ref, wo_ref, l1g_ref, l1b_ref,
                 wu_ref, cw_ref, cb_ref, wd_ref, l2g_ref, l2b_ref, o_ref, act_scr,
                 *, alpha, heads, row_w, tb, tf, dff, ctx_row):
    s5_y = jax.nn.gelu(y_ref[0] + u_ref[0] * d_ref[...])
    s5_out = s5_y * jax.nn.sigmoid(_dot(s5_y.astype(BF16), wg_ref[...]) + bg_ref[...])
    o = of_ref[0].astype(F32) + ob_ref[0].astype(F32)
    gate = g_ref[0].astype(F32)
    nw = nw_ref[...]
    mixed = [s5_out.astype(BF16)]
    for hd in range(heads):
        sl = slice(hd * LANES, (hd + 1) * LANES)
        oh = o[:, sl]
        ms = jnp.mean(oh * oh, axis=-1, keepdims=True)
        mixed.append((oh * lax.rsqrt(ms + RMS_EPS) * nw * gate[:, sl]).astype(BF16))
    proj = _dot(jnp.concatenate(mixed, axis=1), wo_ref[...])
    x = _layer_norm(alpha * x_ref[0] + _mod_row(g1_ref, ctx_row) * proj, l1g_ref[...], l1b_ref[...])

    h = (x * (1.0 + _mod_row(sc_ref, ctx_row)) + _mod_row(sh_ref, ctx_row)).astype(BF16)
    pos = lax.broadcasted_iota(jnp.int32, (tb, 1), 0) % row_w
    has_prev = pos != 0
    has_next = pos != row_w - 1

    def conv(up, c0):
        prev = jnp.where(has_prev, pltpu.roll(up, 1, 0), 0.0)
        nxt = jnp.where(has_next, pltpu.roll(up, tb - 1, 0), 0.0)
        cols = slice(c0, c0 + tf)
        return prev * cw_ref[0:1, cols] + up * cw_ref[1:2, cols] + nxt * cw_ref[2:3, cols] + cb_ref[:, cols]

    for t in range(dff // tf):
        a = conv(_dot(h, wu_ref[:, t * tf:(t + 1) * tf]), t * tf)
        g = conv(_dot(h, wu_ref[:, dff + t * tf:dff + (t + 1) * tf]), dff + t * tf)
        act_scr[:, t * tf:(t + 1) * tf] = (_silu(a) * g).astype(BF16)
    z = alpha * x + _mod_row(g2_ref, ctx_row) * _dot(act_scr[...], wd_ref[...])
    o_ref[0] = _layer_norm(z, l2g_ref[...], l2b_ref[...])


def _layer_tail(x, u, y5, o_f, o_b, g, mods, post_consts, ffn_consts, l, alpha, row_w, ctx_row):
    bsz, n, d = x.shape
    w5 = u.shape[-1]
    hgw = g.shape[-1]
    dff = ffn_consts[3].shape[1]
    tf = FFN_TILE
    tb = min(n, TAIL_TOKENS)
    tok = lambda c: pl.BlockSpec((1, tb, c), lambda b, i: (b, i, 0))
    return pl.pallas_call(
        functools.partial(_tail_kernel, alpha=alpha, heads=hgw // LANES, row_w=row_w, tb=tb, tf=tf, dff=dff,
                          ctx_row=ctx_row),
        grid=(bsz, n // tb),
        in_specs=[tok(d), tok(w5), tok(w5), tok(hgw), tok(hgw), tok(hgw)]
        + [_mod_spec(mods, l, i, d) for i in (2, 3, 4, 5)]
        + [_layer_spec(a, l, True) for a in post_consts + ffn_consts],
        out_specs=tok(d),
        out_shape=jax.ShapeDtypeStruct((bsz, n, d), F32),
        scratch_shapes=[pltpu.VMEM((tb, dff), BF16)],
        compiler_params=_cparams("arbitrary", "arbitrary"),
        name="layer_tail",
    )(x, u, y5, o_f, o_b, g, mods, mods, mods, mods, *post_consts, *ffn_consts)


def _token_mixer(h_in, mods, w_in, filters, lb_all, l, init, with_out, w5, hg, ctx_row):
    u, fraw, v, q, g = _inproj(h_in, mods, w_in, l, w5, hg, ctx_row)
    s5_init, hg_init = init
    y5, s5_fin = _s5_mixer(u, filters, l, s5_init, with_out)
    o_f, o_b, hg_fin_f, hg_fin_b = _gla(fraw, v, q, lb_all, l, hg_init[0], hg_init[1])
    return (u, y5, o_f, o_b, g), (s5_fin, (hg_fin_f, hg_fin_b))


def kernel(x, c, ctx, c_ctx, w_mod, b_mod, w_in, s5_lam_re, s5_lam_im, s5_log_dt, s5_b_re, s5_b_im,
           s5_c_re, s5_c_im, s5_d, w_glu, b_glu, hg_lb, hg_norm_w, w_out, ln1_g, ln1_b,
           w_up, conv_w, conv_b, w_down, ln2_g, ln2_b):
    depth = w_mod.shape[0]
    bsz, n, d = x.shape
    n_ctx = ctx.shape[1]
    w5 = s5_d.shape[-1]
    hg = hg_lb.shape[-1]
    heads = hg // LANES
    npair = s5_lam_re.shape[2] // 2
    alpha = (2 * depth) ** 0.25

    lb_all = jnp.cumsum(jax.nn.softmax(hg_lb.astype(F32), axis=0), axis=0)
    lb_all = lb_all - lb_all[:1]

    rb = -(-(bsz + 1) // SUBLANES) * SUBLANES
    rows = jnp.concatenate([c, c_ctx[None], jnp.zeros((rb - bsz - 1, d), F32)], axis=0)
    mods = _mod_vectors(rows, w_mod, b_mod)

    vec = lambda a: a.reshape(depth, 1, a.shape[-1])
    w_in_b = w_in.astype(BF16)
    post_consts = [vec(s5_d), w_glu.astype(BF16), vec(b_glu), vec(hg_norm_w), w_out.astype(BF16), vec(ln1_g), vec(ln1_b)]
    ffn_consts = [w_up.astype(BF16), conv_w, vec(conv_b), w_down.astype(BF16), vec(ln2_g), vec(ln2_b)]

    bbp, ccp, pwp, a_pow = jax.vmap(_s5_filter_inputs)(
        s5_lam_re, s5_lam_im, s5_log_dt, s5_b_re, s5_b_im, s5_c_re, s5_c_im)
    filters = tuple(_s5_filters(bbp, ccp, pwp)) + (a_pow,)

    zero_init = (jnp.zeros((4, bsz, npair, LANES), F32),
                 (jnp.zeros((bsz, heads, LANES, LANES), F32),) * 2)

    for l in range(depth):
        last = l == depth - 1
        c_parts, ctx_states = _token_mixer(ctx, mods, w_in_b, filters, lb_all, l, zero_init, not last, w5, hg, bsz)
        x_parts, _ = _token_mixer(x, mods, w_in_b, filters, lb_all, l, ctx_states, True, w5, hg, None)
        x = _layer_tail(x, *x_parts, mods, post_consts, ffn_consts, l, alpha, GRID_W, None)
        if not last:
            ctx = _layer_tail(ctx, *c_parts, mods, post_consts, ffn_consts, l, alpha, n_ctx, bsz)
    return x
```

```python
import functools
import math

import jax
import jax.numpy as jnp
from jax import lax
from jax.experimental import pallas as pl
from jax.experimental.pallas import tpu as pltpu

F32 = jnp.float32
BF16 = jnp.bfloat16

GRID_W = 64
HG_CHUNK = 32
S5_CHUNK = 16
LN_EPS = 1e-5
RMS_EPS = 1e-6
LANES = 128
SUBLANES = 8
GRAN = LANES // SUBLANES
VMEM_LIMIT = 56 * 1024 * 1024
INPROJ_TOKENS = 1024
TAIL_TOKENS = 512
FFN_TILE = 256
S5_ROWS = 128
HG_BLOCK = 256
HG_BLOCKS_PER_STEP = 4


def _cparams(*sem):
    return pltpu.CompilerParams(dimension_semantics=sem, vmem_limit_bytes=VMEM_LIMIT)


def _silu(x):
    return x * jax.nn.sigmoid(x)


def _dot(a, b):
    return jnp.dot(a, b, preferred_element_type=F32)


def _dot_t0(a, b):
    return lax.dot_general(a, b, (((0,), (0,)), ((), ())), preferred_element_type=F32)


def _dot_t1(a, b):
    return lax.dot_general(a, b, (((1,), (1,)), ((), ())), preferred_element_type=F32)


def _split(x, parts):
    out = []
    for _ in range(parts - 1):
        piece = x.astype(BF16)
        out.append(piece)
        x = x - piece.astype(F32)
    out.append(x.astype(BF16))
    return out


def _hp_dot_t1(a, b):
    ah, al = _split(a, 2)
    bh, bl = _split(b, 2)
    return _dot_t1(ah, bh) + _dot_t1(ah, bl) + _dot_t1(al, bh)


def _layer_norm(z, g, b):
    mu = jnp.mean(z, axis=-1, keepdims=True)
    zc = z - mu
    var = jnp.mean(zc * zc, axis=-1, keepdims=True)
    return zc * lax.rsqrt(var + LN_EPS) * g + b


def _layer_spec(a, l, single=False):
    mode = dict(pipeline_mode=pl.Buffered(1)) if single else {}
    return pl.BlockSpec((None,) + a.shape[1:], lambda *_: (l,) + (0,) * (a.ndim - 1), **mode)


def _mod_spec(mods, l, chunk, d):
    return pl.BlockSpec((None, mods.shape[1], d), lambda *_: (l, 0, chunk))


def _mod_row(m_ref, ctx_row):
    row = pl.program_id(0) if ctx_row is None else ctx_row
    return m_ref[pl.ds(row, 1), :]


def _mod_kernel(c_ref, w_ref, b_ref, o_ref):
    s = _silu(c_ref[...]).astype(BF16)
    o_ref[0] = _dot(s, w_ref[0].astype(BF16)) + b_ref[0]


def _mod_vectors(rows, w_mod, b_mod):
    depth, d, d6 = w_mod.shape
    rb = rows.shape[0]
    tn = 1536 if d6 % 1536 == 0 else d6
    return pl.pallas_call(
        _mod_kernel,
        grid=(depth, d6 // tn),
        in_specs=[
            pl.BlockSpec((rb, d), lambda l, j: (0, 0)),
            pl.BlockSpec((1, d, tn), lambda l, j: (l, 0, j)),
            pl.BlockSpec((1, 1, tn), lambda l, j: (l, 0, j)),
        ],
        out_specs=pl.BlockSpec((1, rb, tn), lambda l, j: (l, 0, j)),
        out_shape=jax.ShapeDtypeStruct((depth, rb, d6), F32),
        compiler_params=_cparams("arbitrary", "arbitrary"),
        name="mod_vectors",
    )(rows, w_mod, b_mod.reshape(depth, 1, d6))


def _inproj_kernel(x_ref, sh_ref, sc_ref, w_ref, u_ref, f_ref, v_ref, q_ref, g_ref, *, w5, hg, ctx_row):
    h = (x_ref[0] * (1.0 + _mod_row(sc_ref, ctx_row)) + _mod_row(sh_ref, ctx_row)).astype(BF16)
    o = 0
    u_ref[0] = _dot(h, w_ref[:, o:o + w5]); o += w5
    f_ref[0] = _dot(h, w_ref[:, o:o + 2 * hg]); o += 2 * hg
    v_ref[0] = _dot(h, w_ref[:, o:o + hg]).astype(BF16); o += hg
    q_ref[0] = _silu(_dot(h, w_ref[:, o:o + hg])).astype(BF16); o += hg
    g_ref[0] = _silu(_dot(h, w_ref[:, o:o + hg])).astype(BF16)


def _inproj(x, mods, w_in, l, w5, hg, ctx_row):
    bsz, n, d = x.shape
    tb = min(n, INPROJ_TOKENS)
    tok = lambda c: pl.BlockSpec((1, tb, c), lambda b, j: (b, j, 0))
    return pl.pallas_call(
        functools.partial(_inproj_kernel, w5=w5, hg=hg, ctx_row=ctx_row),
        grid=(bsz, n // tb),
        in_specs=[tok(d), _mod_spec(mods, l, 0, d), _mod_spec(mods, l, 1, d), _layer_spec(w_in, l, True)],
        out_specs=[tok(w5), tok(2 * hg), tok(hg), tok(hg), tok(hg)],
        out_shape=[jax.ShapeDtypeStruct((bsz, n, c), t)
                   for c, t in ((w5, F32), (2 * hg, F32), (hg, BF16), (hg, BF16), (hg, BF16))],
        compiler_params=_cparams("arbitrary", "arbitrary"),
        name="inproj",
    )(x, mods, mods, w_in)


def _slot_masks(shape):
    lane = lax.broadcasted_iota(jnp.int32, shape, len(shape) - 1)
    return [(lane // GRAN) == s for s in range(SUBLANES)]


def _row_copies(hbm_ref, buf_ref, sem_ref, step, slot, tr, to_hbm):
    out = []
    for l in range(S5_CHUNK):
        hbm = hbm_ref.at[pl.ds(step * tr, tr), l, :]
        vm = buf_ref.at[slot, l]
        out.append(pltpu.make_async_copy(vm, hbm, sem_ref.at[slot, l]) if to_hbm
                   else pltpu.make_async_copy(hbm, vm, sem_ref.at[slot, l]))
    return out


def _s5_in_kernel(u_hbm, t_ref, w_ref, yi_ref, s_ref, buf, sem, uf_scr, s_scr, *, tr, h, nsteps):
    i = pl.program_id(0)
    slot = i % 2
    npair = uf_scr.shape[0]
    gw = S5_CHUNK * h
    gpb = LANES // h

    @pl.when(i == 0)
    def _():
        for cp in _row_copies(u_hbm, buf, sem, 0, 0, tr, False):
            cp.start()

    for cp in _row_copies(u_hbm, buf, sem, i, slot, tr, False):
        cp.wait()

    @pl.when(i + 1 < nsteps)
    def _():
        for cp in _row_copies(u_hbm, buf, sem, i + 1, 1 - slot, tr, False):
            cp.start()

    masks = _slot_masks((tr, LANES))
    for blk in range(buf.shape[-1] // LANES):
        for t in range(S5_CHUNK // SUBLANES):
            rot = []
            for l8 in range(SUBLANES):
                x = buf[slot, t * SUBLANES + l8, :, blk * LANES:(blk + 1) * LANES]
                rot.append(pltpu.roll(x, l8 * h, 1) if l8 else x)
            for gi in range(gpb):
                g = blk * gpb + gi
                dest = rot[0]
                for l8 in range(1, SUBLANES):
                    dest = jnp.where(masks[(gi + l8) % SUBLANES], rot[l8], dest)
                c0 = (g % 2) * gw + t * LANES
                uf_scr[g // 2, :, c0:c0 + LANES] = dest
    for p in range(npair):
        ub = uf_scr[p].astype(BF16)
        s = _dot(ub, w_ref[p])
        for k in range(4):
            s_scr[k, pl.ds(p, tr, stride=npair), :] = s[:, k * LANES:(k + 1) * LANES]
        yi_ref[p] = jnp.concatenate([_dot(ub[:, a * gw:(a + 1) * gw], t_ref[p, a]) for a in range(2)],
                                    axis=1).astype(BF16)
    s_ref[...] = s_scr[...].astype(BF16)


def _s5_in(u3, toep, w_in_pair, l, h, tr):
    r, ell, w5 = u3.shape
    _, npair, _, gw, _ = toep.shape
    nsteps = r // tr
    return pl.pallas_call(
        functools.partial(_s5_in_kernel, tr=tr, h=h, nsteps=nsteps),
        grid=(nsteps,),
        in_specs=[pl.BlockSpec(memory_space=pl.ANY), _layer_spec(toep, l, True), _layer_spec(w_in_pair, l, True)],
        out_specs=[
            pl.BlockSpec((npair, tr, 2 * gw), lambda i: (0, i, 0)),
            pl.BlockSpec((4, tr * npair, LANES), lambda i: (0, i, 0)),
        ],
        out_shape=[
            jax.ShapeDtypeStruct((npair, r, 2 * gw), BF16),
            jax.ShapeDtypeStruct((4, r * npair, LANES), BF16),
        ],
        scratch_shapes=[pltpu.VMEM((2, ell, tr, w5), F32), pltpu.SemaphoreType.DMA((2, ell)),
                        pltpu.VMEM((npair, tr, 2 * gw), F32), pltpu.VMEM((4, tr * npair, LANES), F32)],
        compiler_params=_cparams("arbitrary"),
        name="s5_in",
    )(u3, toep, w_in_pair)


def _s5_scan_kernel(s_ref, s0_ref, a_ref, xp_ref, xf_ref, *, nc):
    far, fai, bar, bai = a_ref[0], a_ref[1], a_ref[2], a_ref[3]

    def body(i, carry):
        fr, fi, br, bi = carry
        ib = nc - 1 - i
        xp_ref[0, 0, i] = fr.astype(BF16)
        xp_ref[1, 0, i] = fi.astype(BF16)
        xp_ref[2, 0, ib] = br.astype(BF16)
        xp_ref[3, 0, ib] = bi.astype(BF16)
        nfr = far * fr - fai * fi + s_ref[0, 0, i].astype(F32)
        nfi = far * fi + fai * fr + s_ref[1, 0, i].astype(F32)
        nbr = bar * br - bai * bi + s_ref[2, 0, ib].astype(F32)
        nbi = bar * bi + bai * br + s_ref[3, 0, ib].astype(F32)
        return nfr, nfi, nbr, nbi

    init = (s0_ref[0, 0], s0_ref[1, 0], s0_ref[2, 0], s0_ref[3, 0])
    fr, fi, br, bi = lax.fori_loop(0, nc, body, init, unroll=4)
    xf_ref[0, 0] = fr
    xf_ref[1, 0] = fi
    xf_ref[2, 0] = br
    xf_ref[3, 0] = bi


def _s5_scan(s_loc, s0, a_pow, l):
    _, bsz, nc, npair, _ = s_loc.shape
    return pl.pallas_call(
        functools.partial(_s5_scan_kernel, nc=nc),
        grid=(bsz,),
        in_specs=[
            pl.BlockSpec((4, 1, nc, npair, LANES), lambda b: (0, b, 0, 0, 0)),
            pl.BlockSpec((4, 1, npair, LANES), lambda b: (0, b, 0, 0)),
            _layer_spec(a_pow, l),
        ],
        out_specs=[
            pl.BlockSpec((4, 1, nc, npair, LANES), lambda b: (0, b, 0, 0, 0)),
            pl.BlockSpec((4, 1, npair, LANES), lambda b: (0, b, 0, 0)),
        ],
        out_shape=[
            jax.ShapeDtypeStruct(s_loc.shape, BF16),
            jax.ShapeDtypeStruct((4, bsz, npair, LANES), F32),
        ],
        compiler_params=_cparams("arbitrary"),
        name="s5_scan",
    )(s_loc, s0, a_pow)


def _s5_out_kernel(yi_ref, xp_ref, w_ref, y_hbm, yf_scr, buf, sem, xp_scr, *, tr, h, nsteps):
    i = pl.program_id(0)
    npair = yi_ref.shape[0]
    gw = S5_CHUNK * h
    gpb = LANES // h
    xp_scr[...] = xp_ref[...].astype(F32)
    for p in range(npair):
        xcat = jnp.concatenate([xp_scr[k, pl.ds(p, tr, stride=npair), :] for k in range(4)], axis=1)
        yf_scr[p] = yi_ref[p].astype(F32) + _dot(xcat.astype(BF16), w_ref[p])

    @pl.when(i > 0)
    def _():
        for cp in _row_copies(y_hbm, buf, sem, i - 1, 0, tr, True):
            cp.wait()

    masks = _slot_masks((tr, LANES))
    for blk in range(npair * 2 * h // LANES):
        for t in range(S5_CHUNK // SUBLANES):
            src = []
            for gi in range(gpb):
                g = blk * gpb + gi
                c0 = (g % 2) * gw + t * LANES
                src.append(yf_scr[g // 2, :, c0:c0 + LANES])
            for l8 in range(SUBLANES):
                m = src[0]
                for gi in range(1, gpb):
                    m = jnp.where(masks[(gi + l8) % SUBLANES], src[gi], m)
                buf[0, t * SUBLANES + l8, :, blk * LANES:(blk + 1) * LANES] = (
                    pltpu.roll(m, LANES - l8 * h, 1) if l8 else m)
    copies = _row_copies(y_hbm, buf, sem, i, 0, tr, True)
    for cp in copies:
        cp.start()

    @pl.when(i == nsteps - 1)
    def _():
        for cp in copies:
            cp.wait()


def _s5_out(y_intra, x_prev, w_out_pair, l, h, tr):
    npair, r, width = y_intra.shape
    w5 = npair * 2 * h
    nsteps = r // tr
    return pl.pallas_call(
        functools.partial(_s5_out_kernel, tr=tr, h=h, nsteps=nsteps),
        grid=(nsteps,),
        in_specs=[
            pl.BlockSpec((npair, tr, width), lambda i: (0, i, 0)),
            pl.BlockSpec((4, tr * npair, LANES), lambda i: (0, i, 0)),
            _layer_spec(w_out_pair, l, True),
        ],
        out_specs=pl.BlockSpec(memory_space=pl.ANY),
        out_shape=jax.ShapeDtypeStruct((r, S5_CHUNK, w5), F32),
        scratch_shapes=[pltpu.VMEM((npair, tr, width), F32), pltpu.VMEM((1, S5_CHUNK, tr, w5), F32),
                        pltpu.SemaphoreType.DMA((1, S5_CHUNK)), pltpu.VMEM((4, tr * npair, LANES), F32)],
        compiler_params=_cparams("arbitrary"),
        name="s5_out",
    )(y_intra, x_prev, w_out_pair)


def _filters_kernel(bb_ref, cc_ref, pw_ref, t_ref, win_ref, wout_ref, ca_scr, t_scr, *, h, ppb):
    ell = S5_CHUNK
    gw = ell * h
    lane = lax.broadcasted_iota(jnp.int32, (h, gw), 1)
    for pi in range(ppb):
        for a in range(2):
            gm = 2 * pi + a
            lag = []
            for d in range(2):
                cr, ci = cc_ref[0, d, pi, a], cc_ref[1, d, pi, a]
                for j in range(ell):
                    tau = j if d == 0 else ell - 1 - j
                    pr, pim = pw_ref[0, d, pi, tau:tau + 1, :], pw_ref[1, d, pi, tau:tau + 1, :]
                    ca_scr[0, j * h:(j + 1) * h, :] = cr * pr - ci * pim
                    ca_scr[1, j * h:(j + 1) * h, :] = cr * pim + ci * pr
                lag.append(_hp_dot_t1(bb_ref[0, d, pi, a], ca_scr[0]) - _hp_dot_t1(bb_ref[1, d, pi, a], ca_scr[1]))
            for lp in range(ell):
                sf, sb = lp * h, (ell - 1 - lp) * h
                fwd = lag[0] if sf == 0 else jnp.where(lane >= sf, pltpu.roll(lag[0], sf, 1), 0.0)
                bwd = lag[1] if sb == 0 else jnp.where(lane < gw - sb, pltpu.roll(lag[1], gw - sb, 1), 0.0)
                row = fwd + bwd
                if gm:
                    row = jnp.concatenate([pltpu.roll(row[:, t * LANES:(t + 1) * LANES], gm * h, 1)
                                           for t in range(gw // LANES)], axis=1)
                r0 = (lp // SUBLANES) * LANES + ((gm + lp) % SUBLANES) * h
                t_ref[pi, a, r0:r0 + h, :] = row.astype(BF16)
            for d in range(2):
                for l in range(ell):
                    r0 = a * gw + (l // SUBLANES) * LANES + ((gm + l) % SUBLANES) * h
                    t_in = ell - 1 - l if d == 0 else l
                    t_out = l + 1 if d == 0 else ell - l
                    pr, pim = pw_ref[0, d, pi, t_in:t_in + 1, :], pw_ref[1, d, pi, t_in:t_in + 1, :]
                    br, bi = bb_ref[0, d, pi, a], bb_ref[1, d, pi, a]
                    win_ref[pi, r0:r0 + h, (2 * d) * LANES:(2 * d + 1) * LANES] = (pr * br - pim * bi).astype(BF16)
                    win_ref[pi, r0:r0 + h, (2 * d + 1) * LANES:(2 * d + 2) * LANES] = (pr * bi + pim * br).astype(BF16)
                    pr, pim = pw_ref[0, d, pi, t_out:t_out + 1, :], pw_ref[1, d, pi, t_out:t_out + 1, :]
                    cr, ci = cc_ref[0, d, pi, a], cc_ref[1, d, pi, a]
                    t_scr[r0:r0 + h, (2 * d) * LANES:(2 * d + 1) * LANES] = cr * pr - ci * pim
                    t_scr[r0:r0 + h, (2 * d + 1) * LANES:(2 * d + 2) * LANES] = -(cr * pim + ci * pr)
        wout_ref[pi] = t_scr[...].T.astype(BF16)


def _s5_filters(bbp, ccp, pwp):
    depth, _, _, npair, _, h, _ = bbp.shape
    ell1 = pwp.shape[-2]
    ppb = LANES // h // 2
    gw = S5_CHUNK * h
    width = 2 * gw
    return pl.pallas_call(
        functools.partial(_filters_kernel, h=h, ppb=ppb),
        grid=(depth, npair // ppb),
        in_specs=[
            pl.BlockSpec((None, 2, 2, ppb, 2, h, LANES), lambda l, b: (l, 0, 0, b, 0, 0, 0)),
            pl.BlockSpec((None, 2, 2, ppb, 2, h, LANES), lambda l, b: (l, 0, 0, b, 0, 0, 0)),
            pl.BlockSpec((None, 2, 2, ppb, ell1, LANES), lambda l, b: (l, 0, 0, b, 0, 0)),
        ],
        out_specs=[pl.BlockSpec((None, ppb, 2, gw, gw), lambda l, b: (l, b, 0, 0, 0)),
                   pl.BlockSpec((None, ppb, width, 4 * LANES), lambda l, b: (l, b, 0, 0)),
                   pl.BlockSpec((None, ppb, 4 * LANES, width), lambda l, b: (l, b, 0, 0))],
        out_shape=[jax.ShapeDtypeStruct((depth, npair, 2, gw, gw), BF16),
                   jax.ShapeDtypeStruct((depth, npair, width, 4 * LANES), BF16),
                   jax.ShapeDtypeStruct((depth, npair, 4 * LANES, width), BF16)],
        scratch_shapes=[pltpu.VMEM((2, gw, LANES), F32), pltpu.VMEM((width, 4 * LANES), F32)],
        compiler_params=_cparams("arbitrary", "arbitrary"),
        name="s5_filters",
    )(bbp, ccp, pwp)


def _pair_halves(w):
    *lead, g, h, p = w.shape
    w = w.reshape(*lead, g // 2, 2, h, p)
    z = jnp.zeros_like(w[..., 0, :, :])
    return jnp.stack([jnp.concatenate([w[..., 0, :, :], z], axis=-1),
                      jnp.concatenate([z, w[..., 1, :, :]], axis=-1)], axis=-3)


def _s5_filter_inputs(lam_re, lam_im, log_dt, b_re, b_im, c_re, c_im):
    ndir, g, p = lam_re.shape
    h = b_re.shape[-1]
    ell = S5_CHUNK
    npair = g // 2
    lr, li = lam_re.astype(F32), lam_im.astype(F32)
    dt = jnp.exp(log_dt.astype(F32))[..., None]
    mag, ang = jnp.exp(lr * dt), li * dt
    abar_re, abar_im = mag * jnp.cos(ang), mag * jnp.sin(ang)
    den = lr * lr + li * li
    nr, ni = abar_re - 1.0, abar_im
    coef_re = ((nr * lr + ni * li) / den)[..., None]
    coef_im = ((ni * lr - nr * li) / den)[..., None]
    bb_re = coef_re * b_re - coef_im * b_im
    bb_im = coef_re * b_im + coef_im * b_re
    tau = jnp.arange(ell + 1, dtype=F32)[:, None, None, None]
    pmag, pang = jnp.exp(tau * (lr * dt)), tau * (li * dt)
    pw_re, pw_im = pmag * jnp.cos(pang), pmag * jnp.sin(pang)
    bbp = jnp.stack([_pair_halves(b.transpose(0, 1, 3, 2)) for b in (bb_re, bb_im)])
    ccp = jnp.stack([_pair_halves(c) for c in (c_re.astype(F32), c_im.astype(F32))])
    pwp = jnp.stack([pw.reshape(ell + 1, ndir, npair, 2 * p).transpose(1, 2, 0, 3) for pw in (pw_re, pw_im)])
    a_pow = jnp.stack([pw[ell, d].reshape(npair, 2 * p) for d in range(2) for pw in (pw_re, pw_im)])
    return bbp, ccp, pwp, a_pow


def _s5_mixer(u, filters, l, s0, with_out):
    toep, w_in_pair, w_out_pair, a_pow = filters
    bsz, n, w5 = u.shape
    npair = toep.shape[1]
    h = w5 // (2 * npair)
    nc = n // S5_CHUNK
    tr = min(bsz * nc, S5_ROWS)
    y_intra, s_loc = _s5_in(u.reshape(bsz * nc, S5_CHUNK, w5), toep, w_in_pair, l, h, tr)
    x_prev, x_fin = _s5_scan(s_loc.reshape(4, bsz, nc, npair, LANES), s0, a_pow, l)
    if not with_out:
        return None, x_fin
    y = _s5_out(y_intra, x_prev.reshape(4, bsz * nc * npair, LANES), w_out_pair, l, h, tr)
    return y.reshape(bsz, n, w5), x_fin


def _gla_direction(fr, q, v, lb, s_ref, o_ref, row0, *, blk, reverse, heads):
    n = blk // HG_CHUNK
    width = heads * LANES
    f = lb + (1.0 - lb) * jax.nn.sigmoid(fr)
    k = 1.0 - f
    hi, lo = _split(jnp.log(f), 2)
    row = lax.broadcasted_iota(jnp.int32, (blk, blk), 0)
    col = lax.broadcasted_iota(jnp.int32, (blk, blk), 1)
    tri = jnp.where((col >= row) if reverse else (col <= row), 1.0, 0.0).astype(BF16)
    c = _dot(tri, hi) + _dot(tri, lo)

    def rows(i):
        return slice(blk - (i + 1) * HG_CHUNK, blk - i * HG_CHUNK) if reverse else slice(i * HG_CHUNK, (i + 1) * HG_CHUNK)

    def mem_order(chunks):
        return sorted(chunks, reverse=reverse)

    r = [jnp.zeros((1, width), F32)]
    for i in range(n):
        edge = rows(i).start if reverse else rows(i).stop - 1
        r.append(c[edge:edge + 1])

    def per_chunk(vals):
        return jnp.concatenate([jnp.broadcast_to(vals[i], (HG_CHUNK, width)) for i in mem_order(range(n))], axis=0)

    qs = q * jnp.exp(c - per_chunk(r[:n]))
    kdl = k * jnp.exp(per_chunk(r[1:]) - c)
    qi = (qs * per_chunk([jnp.exp(r[i]) for i in range(n)])).astype(BF16)
    kd = (kdl * per_chunk([jnp.exp(r[n] - r[i + 1]) for i in range(n)])).astype(BF16)
    kdlb = kdl.astype(BF16)
    hop = {(i, j): jnp.exp(r[i] - r[j + 1]) for j in range(n) for i in range(j, n)}
    dparts = _split(jnp.concatenate([r[n], jnp.zeros((SUBLANES - 1, width), F32)], axis=0), 3)
    ones8 = jnp.ones((SUBLANES, LANES), BF16)
    lr = lax.broadcasted_iota(jnp.int32, (HG_CHUNK, HG_CHUNK), 0)
    lc = lax.broadcasted_iota(jnp.int32, (HG_CHUNK, HG_CHUNK), 1)
    causal = (lc >= lr) if reverse else (lc <= lr)

    for hd in range(heads):
        sl = slice(hd * LANES, (hd + 1) * LANES)
        state = s_ref[hd]
        dcol = sum(_dot_t0(p[:, sl], ones8) for p in dparts)
        o_inter = _dot(qi[:, sl], state.astype(BF16))
        s_ref[hd] = jnp.exp(dcol) * state + _dot_t0(kd[:, sl], v[:, sl])
        acc = {i: o_inter[rows(i)] for i in range(n)}
        for j in range(n):
            queries = mem_order(range(j, n))
            lhs = jnp.concatenate([qs[rows(i), sl] * hop[i, j][:, sl] for i in queries], axis=0).astype(BF16)
            att = _dot_t1(lhs, kdlb[rows(j), sl])
            pieces = [att[a * HG_CHUNK:(a + 1) * HG_CHUNK] for a in range(len(queries))]
            dpos = queries.index(j)
            pieces[dpos] = jnp.where(causal, pieces[dpos], 0.0)
            o_j = _dot(jnp.concatenate(pieces, axis=0).astype(BF16), v[rows(j), sl])
            for a, i in enumerate(queries):
                acc[i] = acc[i] + o_j[a * HG_CHUNK:(a + 1) * HG_CHUNK]
        o_ref[0, row0:row0 + blk, sl] = jnp.concatenate([acc[i] for i in mem_order(range(n))], axis=0).astype(BF16)


def _gla_kernel(ff_ref, fb_ref, vf_ref, vb_ref, qf_ref, qb_ref, lb_ref, s0f_ref, s0b_ref,
                of_ref, ob_ref, sff_ref, sfb_ref, s_scr, *, blk, heads, nsub):
    j = pl.program_id(1)
    last = pl.num_programs(1) - 1

    @pl.when(j == 0)
    def _():
        s_scr[0] = s0f_ref[0]
        s_scr[1] = s0b_ref[0]

    for sb in range(nsub):
        rf, rb = sb * blk, (nsub - 1 - sb) * blk
        _gla_direction(ff_ref[0, rf:rf + blk], qf_ref[0, rf:rf + blk].astype(F32), vf_ref[0, rf:rf + blk], lb_ref[0:1],
                       s_scr.at[0], of_ref, rf, blk=blk, reverse=False, heads=heads)
        _gla_direction(fb_ref[0, rb:rb + blk], qb_ref[0, rb:rb + blk].astype(F32), vb_ref[0, rb:rb + blk], lb_ref[1:2],
                       s_scr.at[1], ob_ref, rb, blk=blk, reverse=True, heads=heads)

    @pl.when(j == last)
    def _():
        sff_ref[0] = s_scr[0]
        sfb_ref[0] = s_scr[1]


def _gla(fraw, v, q, lb_all, l, s0f, s0b):
    bsz, n, hgw = v.shape
    heads = hgw // LANES
    blk = min(n, HG_BLOCK)
    nsub = math.gcd(n // blk, HG_BLOCKS_PER_STEP)
    step = blk * nsub
    nstep = n // step
    fwd = lambda c: pl.BlockSpec((1, step, hgw), lambda b, j: (b, j, c))
    bwd = lambda c: pl.BlockSpec((1, step, hgw), lambda b, j: (b, nstep - 1 - j, c))
    st = pl.BlockSpec((1, heads, LANES, LANES), lambda b, j: (b, 0, 0, 0))
    return pl.pallas_call(
        functools.partial(_gla_kernel, blk=blk, heads=heads, nsub=nsub),
        grid=(bsz, nstep),
        in_specs=[fwd(0), bwd(1), fwd(0), bwd(0), fwd(0), bwd(0), _layer_spec(lb_all, l), st, st],
        out_specs=[fwd(0), bwd(0), st, st],
        out_shape=[jax.ShapeDtypeStruct((bsz, n, hgw), BF16)] * 2
        + [jax.ShapeDtypeStruct((bsz, heads, LANES, LANES), F32)] * 2,
        scratch_shapes=[pltpu.VMEM((2, heads, LANES, LANES), F32)],
        compiler_params=_cparams("arbitrary", "arbitrary"),
        name="gla",
    )(fraw, fraw, v, v, q, q, lb_all, s0f, s0b)


def _tail_kernel(x_ref, u_ref, y_ref, of_ref, ob_ref, g_ref, g1_ref, sh_ref, sc_ref, g2_ref,
                 d_ref, wg_ref, bg_ref, nw_ref, wo_ref, l1g_ref, l1b_ref,
                 wu_ref, cw_ref, cb_ref, wd_ref, l2g_ref, l2b_ref, o_ref, act_scr,
                 *, alpha, heads, row_w, tb, tf, dff, ctx_row):
    s5_y = jax.nn.gelu(y_ref[0] + u_ref[0] * d_ref[...])
    s5_out = s5_y * jax.nn.sigmoid(_dot(s5_y.astype(BF16), wg_ref[...]) + bg_ref[...])
    o = of_ref[0].astype(F32) + ob_ref[0].astype(F32)
    gate = g_ref[0].astype(F32)
    nw = nw_ref[...]
    mixed = [s5_out.astype(BF16)]
    for hd in range(heads):
        sl = slice(hd * LANES, (hd + 1) * LANES)
        oh = o[:, sl]
        ms = jnp.mean(oh * oh, axis=-1, keepdims=True)
        mixed.append((oh * lax.rsqrt(ms + RMS_EPS) * nw * gate[:, sl]).astype(BF16))
    proj = _dot(jnp.concatenate(mixed, axis=1), wo_ref[...])
    x = _layer_norm(alpha * x_ref[0] + _mod_row(g1_ref, ctx_row) * proj, l1g_ref[...], l1b_ref[...])

    h = (x * (1.0 + _mod_row(sc_ref, ctx_row)) + _mod_row(sh_ref, ctx_row)).astype(BF16)
    pos = lax.broadcasted_iota(jnp.int32, (tb, 1), 0) % row_w
    has_prev = pos != 0
    has_next = pos != row_w - 1

    def conv(up, c0):
        prev = jnp.where(has_prev, pltpu.roll(up, 1, 0), 0.0)
        nxt = jnp.where(has_next, pltpu.roll(up, tb - 1, 0), 0.0)
        cols = slice(c0, c0 + tf)
        return prev * cw_ref[0:1, cols] + up * cw_ref[1:2, cols] + nxt * cw_ref[2:3, cols] + cb_ref[:, cols]

    for t in range(dff // tf):
        a = conv(_dot(h, wu_ref[:, t * tf:(t + 1) * tf]), t * tf)
        g = conv(_dot(h, wu_ref[:, dff + t * tf:dff + (t + 1) * tf]), dff + t * tf)
        act_scr[:, t * tf:(t + 1) * tf] = (_silu(a) * g).astype(BF16)
    z = alpha * x + _mod_row(g2_ref, ctx_row) * _dot(act_scr[...], wd_ref[...])
    o_ref[0] = _layer_norm(z, l2g_ref[...], l2b_ref[...])


def _layer_tail(x, u, y5, o_f, o_b, g, mods, post_consts, ffn_consts, l, alpha, row_w, ctx_row):
    bsz, n, d = x.shape
    w5 = u.shape[-1]
    hgw = g.shape[-1]
    dff = ffn_consts[3].shape[1]
    tf = FFN_TILE
    tb = min(n, TAIL_TOKENS)
    tok = lambda c: pl.BlockSpec((1, tb, c), lambda b, i: (b, i, 0))
    return pl.pallas_call(
        functools.partial(_tail_kernel, alpha=alpha, heads=hgw // LANES, row_w=row_w, tb=tb, tf=tf, dff=dff,
                          ctx_row=ctx_row),
        grid=(bsz, n // tb),
        in_specs=[tok(d), tok(w5), tok(w5), tok(hgw), tok(hgw), tok(hgw)]
        + [_mod_spec(mods, l, i, d) for i in (2, 3, 4, 5)]
        + [_layer_spec(a, l, True) for a in post_consts + ffn_consts],
        out_specs=tok(d),
        out_shape=jax.ShapeDtypeStruct((bsz, n, d), F32),
        scratch_shapes=[pltpu.VMEM((tb, dff), BF16)],
        compiler_params=_cparams("arbitrary", "arbitrary"),
        name="layer_tail",
    )(x, u, y5, o_f, o_b, g, mods, mods, mods, mods, *post_consts, *ffn_consts)


def _token_mixer(h_in, mods, w_in, filters, lb_all, l, init, with_out, w5, hg, ctx_row):
    u, fraw, v, q, g = _inproj(h_in, mods, w_in, l, w5, hg, ctx_row)
    s5_init, hg_init = init
    y5, s5_fin = _s5_mixer(u, filters, l, s5_init, with_out)
    o_f, o_b, hg_fin_f, hg_fin_b = _gla(fraw, v, q, lb_all, l, hg_init[0], hg_init[1])
    return (u, y5, o_f, o_b, g), (s5_fin, (hg_fin_f, hg_fin_b))


def kernel(x, c, ctx, c_ctx, w_mod, b_mod, w_in, s5_lam_re, s5_lam_im, s5_log_dt, s5_b_re, s5_b_im,
           s5_c_re, s5_c_im, s5_d, w_glu, b_glu, hg_lb, hg_norm_w, w_out, ln1_g, ln1_b,
           w_up, conv_w, conv_b, w_down, ln2_g, ln2_b):
    depth = w_mod.shape[0]
    bsz, n, d = x.shape
    n_ctx = ctx.shape[1]
    w5 = s5_d.shape[-1]
    hg = hg_lb.shape[-1]
    heads = hg // LANES
    npair = s5_lam_re.shape[2] // 2
    alpha = (2 * depth) ** 0.25

    lb_all = jnp.cumsum(jax.nn.softmax(hg_lb.astype(F32), axis=0), axis=0)
    lb_all = lb_all - lb_all[:1]

    rb = -(-(bsz + 1) // SUBLANES) * SUBLANES
    rows = jnp.concatenate([c, c_ctx[None], jnp.zeros((rb - bsz - 1, d), F32)], axis=0)
    mods = _mod_vectors(rows, w_mod, b_mod)

    vec = lambda a: a.reshape(depth, 1, a.shape[-1])
    w_in_b = w_in.astype(BF16)
    post_consts = [vec(s5_d), w_glu.astype(BF16), vec(b_glu), vec(hg_norm_w), w_out.astype(BF16), vec(ln1_g), vec(ln1_b)]
    ffn_consts = [w_up.astype(BF16), conv_w, vec(conv_b), w_down.astype(BF16), vec(ln2_g), vec(ln2_b)]

    bbp, ccp, pwp, a_pow = jax.vmap(_s5_filter_inputs)(
        s5_lam_re, s5_lam_im, s5_log_dt, s5_b_re, s5_b_im, s5_c_re, s5_c_im)
    filters = tuple(_s5_filters(bbp, ccp, pwp)) + (a_pow,)

    zero_init = (jnp.zeros((4, bsz, npair, LANES), F32),
                 (jnp.zeros((bsz, heads, LANES, LANES), F32),) * 2)

    for l in range(depth):
        last = l == depth - 1
        c_parts, ctx_states = _token_mixer(ctx, mods, w_in_b, filters, lb_all, l, zero_init, not last, w5, hg, bsz)
        x_parts, _ = _token_mixer(x, mods, w_in_b, filters, lb_all, l, ctx_states, True, w5, hg, None)
        x = _layer_tail(x, *x_parts, mods, post_consts, ffn_consts, l, alpha, GRID_W, None)
        if not last:
            ctx = _layer_tail(ctx, *c_parts, mods, post_consts, ffn_consts, l, alpha, n_ctx, bsz)
    return x
```

```python
import functools
import math

import jax
import jax.numpy as jnp
from jax import lax
from jax.experimental import pallas as pl
from jax.experimental.pallas import tpu as pltpu

F32 = jnp.float32
BF16 = jnp.bfloat16

GRID_W = 64
HG_CHUNK = 32
S5_CHUNK = 16
LN_EPS = 1e-5
RMS_EPS = 1e-6
LANES = 128
SUBLANES = 8
GRAN = LANES // SUBLANES
VMEM_LIMIT = 56 * 1024 * 1024
INPROJ_TOKENS = 1024
TAIL_TOKENS = 512
FFN_TILE = 256
S5_ROWS = 128
HG_BLOCK = 256
HG_BLOCKS_PER_STEP = 8


def _cparams(*sem):
    return pltpu.CompilerParams(dimension_semantics=sem, vmem_limit_bytes=VMEM_LIMIT)


def _silu(x):
    return x * jax.nn.sigmoid(x)


def _dot(a, b):
    return jnp.dot(a, b, preferred_element_type=F32)


def _dot_t0(a, b):
    return lax.dot_general(a, b, (((0,), (0,)), ((), ())), preferred_element_type=F32)


def _dot_t1(a, b):
    return lax.dot_general(a, b, (((1,), (1,)), ((), ())), preferred_element_type=F32)


def _split(x, parts):
    out = []
    for _ in range(parts - 1):
        piece = x.astype(BF16)
        out.append(piece)
        x = x - piece.astype(F32)
    out.append(x.astype(BF16))
    return out


def _hp_dot_t1(a, b):
    ah, al = _split(a, 2)
    bh, bl = _split(b, 2)
    return _dot_t1(ah, bh) + _dot_t1(ah, bl) + _dot_t1(al, bh)


def _layer_norm(z, g, b):
    mu = jnp.mean(z, axis=-1, keepdims=True)
    zc = z - mu
    var = jnp.mean(zc * zc, axis=-1, keepdims=True)
    return zc * lax.rsqrt(var + LN_EPS) * g + b


def _layer_spec(a, l, single=False):
    mode = dict(pipeline_mode=pl.Buffered(1)) if single else {}
    return pl.BlockSpec((None,) + a.shape[1:], lambda *_: (l,) + (0,) * (a.ndim - 1), **mode)


def _mod_spec(mods, l, chunk, d):
    return pl.BlockSpec((None, mods.shape[1], d), lambda *_: (l, 0, chunk))


def _mod_row(m_ref, ctx_row):
    row = pl.program_id(0) if ctx_row is None else ctx_row
    return m_ref[pl.ds(row, 1), :]


def _mod_kernel(c_ref, w_ref, b_ref, o_ref):
    s = _silu(c_ref[...]).astype(BF16)
    o_ref[0] = _dot(s, w_ref[0].astype(BF16)) + b_ref[0]


def _mod_vectors(rows, w_mod, b_mod):
    depth, d, d6 = w_mod.shape
    rb = rows.shape[0]
    tn = 1536 if d6 % 1536 == 0 else d6
    return pl.pallas_call(
        _mod_kernel,
        grid=(depth, d6 // tn),
        in_specs=[
            pl.BlockSpec((rb, d), lambda l, j: (0, 0)),
            pl.BlockSpec((1, d, tn), lambda l, j: (l, 0, j)),
            pl.BlockSpec((1, 1, tn), lambda l, j: (l, 0, j)),
        ],
        out_specs=pl.BlockSpec((1, rb, tn), lambda l, j: (l, 0, j)),
        out_shape=jax.ShapeDtypeStruct((depth, rb, d6), F32),
        compiler_params=_cparams("arbitrary", "arbitrary"),
        name="mod_vectors",
    )(rows, w_mod, b_mod.reshape(depth, 1, d6))


def _inproj_kernel(x_ref, sh_ref, sc_ref, w_ref, u_ref, f_ref, v_ref, q_ref, g_ref, *, w5, hg, ctx_row):
    h = (x_ref[0] * (1.0 + _mod_row(sc_ref, ctx_row)) + _mod_row(sh_ref, ctx_row)).astype(BF16)
    o = 0
    u_ref[0] = _dot(h, w_ref[:, o:o + w5]); o += w5
    f_ref[0] = _dot(h, w_ref[:, o:o + 2 * hg]); o += 2 * hg
    v_ref[0] = _dot(h, w_ref[:, o:o + hg]).astype(BF16); o += hg
    q_ref[0] = _silu(_dot(h, w_ref[:, o:o + hg])).astype(BF16); o += hg
    g_ref[0] = _silu(_dot(h, w_ref[:, o:o + hg])).astype(BF16)


def _inproj(x, mods, w_in, l, w5, hg, ctx_row):
    bsz, n, d = x.shape
    tb = min(n, INPROJ_TOKENS)
    tok = lambda c: pl.BlockSpec((1, tb, c), lambda b, j: (b, j, 0))
    return pl.pallas_call(
        functools.partial(_inproj_kernel, w5=w5, hg=hg, ctx_row=ctx_row),
        grid=(bsz, n // tb),
        in_specs=[tok(d), _mod_spec(mods, l, 0, d), _mod_spec(mods, l, 1, d), _layer_spec(w_in, l, True)],
        out_specs=[tok(w5), tok(2 * hg), tok(hg), tok(hg), tok(hg)],
        out_shape=[jax.ShapeDtypeStruct((bsz, n, c), t)
                   for c, t in ((w5, F32), (2 * hg, F32), (hg, BF16), (hg, BF16), (hg, BF16))],
        compiler_params=_cparams("arbitrary", "arbitrary"),
        name="inproj",
    )(x, mods, mods, w_in)


def _slot_masks(shape):
    lane = lax.broadcasted_iota(jnp.int32, shape, len(shape) - 1)
    return [(lane // GRAN) == s for s in range(SUBLANES)]


def _row_copies(hbm_ref, buf_ref, sem_ref, step, slot, tr, to_hbm):
    out = []
    for l in range(S5_CHUNK):
        hbm = hbm_ref.at[pl.ds(step * tr, tr), l, :]
        vm = buf_ref.at[slot, l]
        out.append(pltpu.make_async_copy(vm, hbm, sem_ref.at[slot, l]) if to_hbm
                   else pltpu.make_async_copy(hbm, vm, sem_ref.at[slot, l]))
    return out


def _s5_in_kernel(u_hbm, t_ref, w_ref, yi_ref, s_ref, buf, sem, uf_scr, s_scr, *, tr, h, nsteps):
    i = pl.program_id(0)
    slot = i % 2
    npair = uf_scr.shape[0]
    gw = S5_CHUNK * h
    gpb = LANES // h

    @pl.when(i == 0)
    def _():
        for cp in _row_copies(u_hbm, buf, sem, 0, 0, tr, False):
            cp.start()

    for cp in _row_copies(u_hbm, buf, sem, i, slot, tr, False):
        cp.wait()

    @pl.when(i + 1 < nsteps)
    def _():
        for cp in _row_copies(u_hbm, buf, sem, i + 1, 1 - slot, tr, False):
            cp.start()

    masks = _slot_masks((tr, LANES))
    for blk in range(buf.shape[-1] // LANES):
        for t in range(S5_CHUNK // SUBLANES):
            rot = []
            for l8 in range(SUBLANES):
                x = buf[slot, t * SUBLANES + l8, :, blk * LANES:(blk + 1) * LANES]
                rot.append(pltpu.roll(x, l8 * h, 1) if l8 else x)
            for gi in range(gpb):
                g = blk * gpb + gi
                dest = rot[0]
                for l8 in range(1, SUBLANES):
                    dest = jnp.where(masks[(gi + l8) % SUBLANES], rot[l8], dest)
                c0 = (g % 2) * gw + t * LANES
                uf_scr[g // 2, :, c0:c0 + LANES] = dest
    for p in range(npair):
        ub = uf_scr[p].astype(BF16)
        s = _dot(ub, w_ref[p])
        for k in range(4):
            s_scr[k, pl.ds(p, tr, stride=npair), :] = s[:, k * LANES:(k + 1) * LANES]
        yi_ref[p] = jnp.concatenate([_dot(ub[:, a * gw:(a + 1) * gw], t_ref[p, a]) for a in range(2)],
                                    axis=1).astype(BF16)
    s_ref[...] = s_scr[...].astype(BF16)


def _s5_in(u3, toep, w_in_pair, l, h, tr):
    r, ell, w5 = u3.shape
    _, npair, _, gw, _ = toep.shape
    nsteps = r // tr
    return pl.pallas_call(
        functools.partial(_s5_in_kernel, tr=tr, h=h, nsteps=nsteps),
        grid=(nsteps,),
        in_specs=[pl.BlockSpec(memory_space=pl.ANY), _layer_spec(toep, l, True), _layer_spec(w_in_pair, l, True)],
        out_specs=[
            pl.BlockSpec((npair, tr, 2 * gw), lambda i: (0, i, 0)),
            pl.BlockSpec((4, tr * npair, LANES), lambda i: (0, i, 0)),
        ],
        out_shape=[
            jax.ShapeDtypeStruct((npair, r, 2 * gw), BF16),
            jax.ShapeDtypeStruct((4, r * npair, LANES), BF16),
        ],
        scratch_shapes=[pltpu.VMEM((2, ell, tr, w5), F32), pltpu.SemaphoreType.DMA((2, ell)),
                        pltpu.VMEM((npair, tr, 2 * gw), F32), pltpu.VMEM((4, tr * npair, LANES), F32)],
        compiler_params=_cparams("arbitrary"),
        name="s5_in",
    )(u3, toep, w_in_pair)


def _s5_scan_kernel(s_ref, s0_ref, a_ref, xp_ref, xf_ref, *, nc):
    far, fai, bar, bai = a_ref[0], a_ref[1], a_ref[2], a_ref[3]

    def body(i, carry):
        fr, fi, br, bi = carry
        ib = nc - 1 - i
        xp_ref[0, 0, i] = fr.astype(BF16)
        xp_ref[1, 0, i] = fi.astype(BF16)
        xp_ref[2, 0, ib] = br.astype(BF16)
        xp_ref[3, 0, ib] = bi.astype(BF16)
        nfr = far * fr - fai * fi + s_ref[0, 0, i].astype(F32)
        nfi = far * fi + fai * fr + s_ref[1, 0, i].astype(F32)
        nbr = bar * br - bai * bi + s_ref[2, 0, ib].astype(F32)
        nbi = bar * bi + bai * br + s_ref[3, 0, ib].astype(F32)
        return nfr, nfi, nbr, nbi

    init = (s0_ref[0, 0], s0_ref[1, 0], s0_ref[2, 0], s0_ref[3, 0])
    fr, fi, br, bi = lax.fori_loop(0, nc, body, init, unroll=4)
    xf_ref[0, 0] = fr
    xf_ref[1, 0] = fi
    xf_ref[2, 0] = br
    xf_ref[3, 0] = bi


def _s5_scan(s_loc, s0, a_pow, l):
    _, bsz, nc, npair, _ = s_loc.shape
    return pl.pallas_call(
        functools.partial(_s5_scan_kernel, nc=nc),
        grid=(bsz,),
        in_specs=[
            pl.BlockSpec((4, 1, nc, npair, LANES), lambda b: (0, b, 0, 0, 0)),
            pl.BlockSpec((4, 1, npair, LANES), lambda b: (0, b, 0, 0)),
            _layer_spec(a_pow, l),
        ],
        out_specs=[
            pl.BlockSpec((4, 1, nc, npair, LANES), lambda b: (0, b, 0, 0, 0)),
            pl.BlockSpec((4, 1, npair, LANES), lambda b: (0, b, 0, 0)),
        ],
        out_shape=[
            jax.ShapeDtypeStruct(s_loc.shape, BF16),
            jax.ShapeDtypeStruct((4, bsz, npair, LANES), F32),
        ],
        compiler_params=_cparams("arbitrary"),
        name="s5_scan",
    )(s_loc, s0, a_pow)


def _s5_out_kernel(yi_ref, xp_ref, w_ref, y_hbm, yf_scr, buf, sem, xp_scr, *, tr, h, nsteps):
    i = pl.program_id(0)
    npair = yi_ref.shape[0]
    gw = S5_CHUNK * h
    gpb = LANES // h
    xp_scr[...] = xp_ref[...].astype(F32)
    for p in range(npair):
        xcat = jnp.concatenate([xp_scr[k, pl.ds(p, tr, stride=npair), :] for k in range(4)], axis=1)
        yf_scr[p] = yi_ref[p].astype(F32) + _dot(xcat.astype(BF16), w_ref[p])

    @pl.when(i > 0)
    def _():
        for cp in _row_copies(y_hbm, buf, sem, i - 1, 0, tr, True):
            cp.wait()

    masks = _slot_masks((tr, LANES))
    for blk in range(npair * 2 * h // LANES):
        for t in range(S5_CHUNK // SUBLANES):
            src = []
            for gi in range(gpb):
                g = blk * gpb + gi
                c0 = (g % 2) * gw + t * LANES
                src.append(yf_scr[g // 2, :, c0:c0 + LANES])
            for l8 in range(SUBLANES):
                m = src[0]
                for gi in range(1, gpb):
                    m = jnp.where(masks[(gi + l8) % SUBLANES], src[gi], m)
                buf[0, t * SUBLANES + l8, :, blk * LANES:(blk + 1) * LANES] = (
                    pltpu.roll(m, LANES - l8 * h, 1) if l8 else m)
    copies = _row_copies(y_hbm, buf, sem, i, 0, tr, True)
    for cp in copies:
        cp.start()

    @pl.when(i == nsteps - 1)
    def _():
        for cp in copies:
            cp.wait()


def _s5_out(y_intra, x_prev, w_out_pair, l, h, tr):
    npair, r, width = y_intra.shape
    w5 = npair * 2 * h
    nsteps = r // tr
    return pl.pallas_call(
        functools.partial(_s5_out_kernel, tr=tr, h=h, nsteps=nsteps),
        grid=(nsteps,),
        in_specs=[
            pl.BlockSpec((npair, tr, width), lambda i: (0, i, 0)),
            pl.BlockSpec((4, tr * npair, LANES), lambda i: (0, i, 0)),
            _layer_spec(w_out_pair, l, True),
        ],
        out_specs=pl.BlockSpec(memory_space=pl.ANY),
        out_shape=jax.ShapeDtypeStruct((r, S5_CHUNK, w5), F32),
        scratch_shapes=[pltpu.VMEM((npair, tr, width), F32), pltpu.VMEM((1, S5_CHUNK, tr, w5), F32),
                        pltpu.SemaphoreType.DMA((1, S5_CHUNK)), pltpu.VMEM((4, tr * npair, LANES), F32)],
        compiler_params=_cparams("arbitrary"),
        name="s5_out",
    )(y_intra, x_prev, w_out_pair)


def _filters_kernel(bb_ref, cc_ref, pw_ref, t_ref, win_ref, wout_ref, ca_scr, t_scr, *, h, ppb):
    ell = S5_CHUNK
    gw = ell * h
    lane = lax.broadcasted_iota(jnp.int32, (h, gw), 1)
    for pi in range(ppb):
        for a in range(2):
            gm = 2 * pi + a
            lag = []
            for d in range(2):
                cr, ci = cc_ref[0, d, pi, a], cc_ref[1, d, pi, a]
                for j in range(ell):
                    tau = j if d == 0 else ell - 1 - j
                    pr, pim = pw_ref[0, d, pi, tau:tau + 1, :], pw_ref[1, d, pi, tau:tau + 1, :]
                    ca_scr[0, j * h:(j + 1) * h, :] = cr * pr - ci * pim
                    ca_scr[1, j * h:(j + 1) * h, :] = cr * pim + ci * pr
                lag.append(_hp_dot_t1(bb_ref[0, d, pi, a], ca_scr[0]) - _hp_dot_t1(bb_ref[1, d, pi, a], ca_scr[1]))
            for lp in range(ell):
                sf, sb = lp * h, (ell - 1 - lp) * h
                fwd = lag[0] if sf == 0 else jnp.where(lane >= sf, pltpu.roll(lag[0], sf, 1), 0.0)
                bwd = lag[1] if sb == 0 else jnp.where(lane < gw - sb, pltpu.roll(lag[1], gw - sb, 1), 0.0)
                row = fwd + bwd
                if gm:
                    row = jnp.concatenate([pltpu.roll(row[:, t * LANES:(t + 1) * LANES], gm * h, 1)
                                           for t in range(gw // LANES)], axis=1)
                r0 = (lp // SUBLANES) * LANES + ((gm + lp) % SUBLANES) * h
                t_ref[pi, a, r0:r0 + h, :] = row.astype(BF16)
            for d in range(2):
                for l in range(ell):
                    r0 = a * gw + (l // SUBLANES) * LANES + ((gm + l) % SUBLANES) * h
                    t_in = ell - 1 - l if d == 0 else l
                    t_out = l + 1 if d == 0 else ell - l
                    pr, pim = pw_ref[0, d, pi, t_in:t_in + 1, :], pw_ref[1, d, pi, t_in:t_in + 1, :]
                    br, bi = bb_ref[0, d, pi, a], bb_ref[1, d, pi, a]
                    win_ref[pi, r0:r0 + h, (2 * d) * LANES:(2 * d + 1) * LANES] = (pr * br - pim * bi).astype(BF16)
                    win_ref[pi, r0:r0 + h, (2 * d + 1) * LANES:(2 * d + 2) * LANES] = (pr * bi + pim * br).astype(BF16)
                    pr, pim = pw_ref[0, d, pi, t_out:t_out + 1, :], pw_ref[1, d, pi, t_out:t_out + 1, :]
                    cr, ci = cc_ref[0, d, pi, a], cc_ref[1, d, pi, a]
                    t_scr[r0:r0 + h, (2 * d) * LANES:(2 * d + 1) * LANES] = cr * pr - ci * pim
                    t_scr[r0:r0 + h, (2 * d + 1) * LANES:(2 * d + 2) * LANES] = -(cr * pim + ci * pr)
        wout_ref[pi] = t_scr[...].T.astype(BF16)


def _s5_filters(bbp, ccp, pwp):
    depth, _, _, npair, _, h, _ = bbp.shape
    ell1 = pwp.shape[-2]
    ppb = LANES // h // 2
    gw = S5_CHUNK * h
    width = 2 * gw
    return pl.pallas_call(
        functools.partial(_filters_kernel, h=h, ppb=ppb),
        grid=(depth, npair // ppb),
        in_specs=[
            pl.BlockSpec((None, 2, 2, ppb, 2, h, LANES), lambda l, b: (l, 0, 0, b, 0, 0, 0)),
            pl.BlockSpec((None, 2, 2, ppb, 2, h, LANES), lambda l, b: (l, 0, 0, b, 0, 0, 0)),
            pl.BlockSpec((None, 2, 2, ppb, ell1, LANES), lambda l, b: (l, 0, 0, b, 0, 0)),
        ],
        out_specs=[pl.BlockSpec((None, ppb, 2, gw, gw), lambda l, b: (l, b, 0, 0, 0)),
                   pl.BlockSpec((None, ppb, width, 4 * LANES), lambda l, b: (l, b, 0, 0)),
                   pl.BlockSpec((None, ppb, 4 * LANES, width), lambda l, b: (l, b, 0, 0))],
        out_shape=[jax.ShapeDtypeStruct((depth, npair, 2, gw, gw), BF16),
                   jax.ShapeDtypeStruct((depth, npair, width, 4 * LANES), BF16),
                   jax.ShapeDtypeStruct((depth, npair, 4 * LANES, width), BF16)],
        scratch_shapes=[pltpu.VMEM((2, gw, LANES), F32), pltpu.VMEM((width, 4 * LANES), F32)],
        compiler_params=_cparams("arbitrary", "arbitrary"),
        name="s5_filters",
    )(bbp, ccp, pwp)


def _pair_halves(w):
    *lead, g, h, p = w.shape
    w = w.reshape(*lead, g // 2, 2, h, p)
    z = jnp.zeros_like(w[..., 0, :, :])
    return jnp.stack([jnp.concatenate([w[..., 0, :, :], z], axis=-1),
                      jnp.concatenate([z, w[..., 1, :, :]], axis=-1)], axis=-3)


def _s5_filter_inputs(lam_re, lam_im, log_dt, b_re, b_im, c_re, c_im):
    ndir, g, p = lam_re.shape
    h = b_re.shape[-1]
    ell = S5_CHUNK
    npair = g // 2
    lr, li = lam_re.astype(F32), lam_im.astype(F32)
    dt = jnp.exp(log_dt.astype(F32))[..., None]
    mag, ang = jnp.exp(lr * dt), li * dt
    abar_re, abar_im = mag * jnp.cos(ang), mag * jnp.sin(ang)
    den = lr * lr + li * li
    nr, ni = abar_re - 1.0, abar_im
    coef_re = ((nr * lr + ni * li) / den)[..., None]
    coef_im = ((ni * lr - nr * li) / den)[..., None]
    bb_re = coef_re * b_re - coef_im * b_im
    bb_im = coef_re * b_im + coef_im * b_re
    tau = jnp.arange(ell + 1, dtype=F32)[:, None, None, None]
    pmag, pang = jnp.exp(tau * (lr * dt)), tau * (li * dt)
    pw_re, pw_im = pmag * jnp.cos(pang), pmag * jnp.sin(pang)
    bbp = jnp.stack([_pair_halves(b.transpose(0, 1, 3, 2)) for b in (bb_re, bb_im)])
    ccp = jnp.stack([_pair_halves(c) for c in (c_re.astype(F32), c_im.astype(F32))])
    pwp = jnp.stack([pw.reshape(ell + 1, ndir, npair, 2 * p).transpose(1, 2, 0, 3) for pw in (pw_re, pw_im)])
    a_pow = jnp.stack([pw[ell, d].reshape(npair, 2 * p) for d in range(2) for pw in (pw_re, pw_im)])
    return bbp, ccp, pwp, a_pow


def _s5_mixer(u, filters, l, s0, with_out):
    toep, w_in_pair, w_out_pair, a_pow = filters
    bsz, n, w5 = u.shape
    npair = toep.shape[1]
    h = w5 // (2 * npair)
    nc = n // S5_CHUNK
    tr = min(bsz * nc, S5_ROWS)
    y_intra, s_loc = _s5_in(u.reshape(bsz * nc, S5_CHUNK, w5), toep, w_in_pair, l, h, tr)
    x_prev, x_fin = _s5_scan(s_loc.reshape(4, bsz, nc, npair, LANES), s0, a_pow, l)
    if not with_out:
        return None, x_fin
    y = _s5_out(y_intra, x_prev.reshape(4, bsz * nc * npair, LANES), w_out_pair, l, h, tr)
    return y.reshape(bsz, n, w5), x_fin


def _gla_direction(fr, q, v, lb, s_ref, o_ref, row0, *, blk, reverse, heads):
    n = blk // HG_CHUNK
    width = heads * LANES
    f = lb + (1.0 - lb) * jax.nn.sigmoid(fr)
    k = 1.0 - f
    hi, lo = _split(jnp.log(f), 2)
    row = lax.broadcasted_iota(jnp.int32, (blk, blk), 0)
    col = lax.broadcasted_iota(jnp.int32, (blk, blk), 1)
    tri = jnp.where((col >= row) if reverse else (col <= row), 1.0, 0.0).astype(BF16)
    c = _dot(tri, hi) + _dot(tri, lo)

    def rows(i):
        return slice(blk - (i + 1) * HG_CHUNK, blk - i * HG_CHUNK) if reverse else slice(i * HG_CHUNK, (i + 1) * HG_CHUNK)

    def mem_order(chunks):
        return sorted(chunks, reverse=reverse)

    r = [jnp.zeros((1, width), F32)]
    for i in range(n):
        edge = rows(i).start if reverse else rows(i).stop - 1
        r.append(c[edge:edge + 1])

    def per_chunk(vals):
        return jnp.concatenate([jnp.broadcast_to(vals[i], (HG_CHUNK, width)) for i in mem_order(range(n))], axis=0)

    qs = q * jnp.exp(c - per_chunk(r[:n]))
    kdl = k * jnp.exp(per_chunk(r[1:]) - c)
    qi = (qs * per_chunk([jnp.exp(r[i]) for i in range(n)])).astype(BF16)
    kd = (kdl * per_chunk([jnp.exp(r[n] - r[i + 1]) for i in range(n)])).astype(BF16)
    kdlb = kdl.astype(BF16)
    hop = {(i, j): jnp.exp(r[i] - r[j + 1]) for j in range(n) for i in range(j, n)}
    dparts = _split(jnp.concatenate([r[n], jnp.zeros((SUBLANES - 1, width), F32)], axis=0), 3)
    ones8 = jnp.ones((SUBLANES, LANES), BF16)
    lr = lax.broadcasted_iota(jnp.int32, (HG_CHUNK, HG_CHUNK), 0)
    lc = lax.broadcasted_iota(jnp.int32, (HG_CHUNK, HG_CHUNK), 1)
    causal = (lc >= lr) if reverse else (lc <= lr)

    for hd in range(heads):
        sl = slice(hd * LANES, (hd + 1) * LANES)
        state = s_ref[hd]
        dcol = sum(_dot_t0(p[:, sl], ones8) for p in dparts)
        o_inter = _dot(qi[:, sl], state.astype(BF16))
        s_ref[hd] = jnp.exp(dcol) * state + _dot_t0(kd[:, sl], v[:, sl])
        acc = {i: o_inter[rows(i)] for i in range(n)}
        for j in range(n):
            queries = mem_order(range(j, n))
            lhs = jnp.concatenate([qs[rows(i), sl] * hop[i, j][:, sl] for i in queries], axis=0).astype(BF16)
            att = _dot_t1(lhs, kdlb[rows(j), sl])
            pieces = [att[a * HG_CHUNK:(a + 1) * HG_CHUNK] for a in range(len(queries))]
            dpos = queries.index(j)
            pieces[dpos] = jnp.where(causal, pieces[dpos], 0.0)
            o_j = _dot(jnp.concatenate(pieces, axis=0).astype(BF16), v[rows(j), sl])
            for a, i in enumerate(queries):
                acc[i] = acc[i] + o_j[a * HG_CHUNK:(a + 1) * HG_CHUNK]
        o_ref[0, row0:row0 + blk, sl] = jnp.concatenate([acc[i] for i in mem_order(range(n))], axis=0).astype(BF16)


def _gla_kernel(ff_ref, fb_ref, vf_ref, vb_ref, qf_ref, qb_ref, lb_ref, s0f_ref, s0b_ref,
                of_ref, ob_ref, sff_ref, sfb_ref, s_scr, *, blk, heads, nsub):
    j = pl.program_id(1)
    last = pl.num_programs(1) - 1

    @pl.when(j == 0)
    def _():
        s_scr[0] = s0f_ref[0]
        s_scr[1] = s0b_ref[0]

    for sb in range(nsub):
        rf, rb = sb * blk, (nsub - 1 - sb) * blk
        _gla_direction(ff_ref[0, rf:rf + blk], qf_ref[0, rf:rf + blk].astype(F32), vf_ref[0, rf:rf + blk], lb_ref[0:1],
                       s_scr.at[0], of_ref, rf, blk=blk, reverse=False, heads=heads)
        _gla_direction(fb_ref[0, rb:rb + blk], qb_ref[0, rb:rb + blk].astype(F32), vb_ref[0, rb:rb + blk], lb_ref[1:2],
                       s_scr.at[1], ob_ref, rb, blk=blk, reverse=True, heads=heads)

    @pl.when(j == last)
    def _():
        sff_ref[0] = s_scr[0]
        sfb_ref[0] = s_scr[1]


def _gla(fraw, v, q, lb_all, l, s0f, s0b):
    bsz, n, hgw = v.shape
    heads = hgw // LANES
    blk = min(n, HG_BLOCK)
    nsub = math.gcd(n // blk, HG_BLOCKS_PER_STEP)
    step = blk * nsub
    nstep = n // step
    fwd = lambda c: pl.BlockSpec((1, step, hgw), lambda b, j: (b, j, c))
    bwd = lambda c: pl.BlockSpec((1, step, hgw), lambda b, j: (b, nstep - 1 - j, c))
    st = pl.BlockSpec((1, heads, LANES, LANES), lambda b, j: (b, 0, 0, 0))
    return pl.pallas_call(
        functools.partial(_gla_kernel, blk=blk, heads=heads, nsub=nsub),
        grid=(bsz, nstep),
        in_specs=[fwd(0), bwd(1), fwd(0), bwd(0), fwd(0), bwd(0), _layer_spec(lb_all, l), st, st],
        out_specs=[fwd(0), bwd(0), st, st],
        out_shape=[jax.ShapeDtypeStruct((bsz, n, hgw), BF16)] * 2
        + [jax.ShapeDtypeStruct((bsz, heads, LANES, LANES), F32)] * 2,
        scratch_shapes=[pltpu.VMEM((2, heads, LANES, LANES), F32)],
        compiler_params=_cparams("arbitrary", "arbitrary"),
        name="gla",
    )(fraw, fraw, v, v, q, q, lb_all, s0f, s0b)


def _tail_kernel(x_ref, u_ref, y_ref, of_ref, ob_ref, g_ref, g1_ref, sh_ref, sc_ref, g2_ref,
                 d_ref, wg_ref, bg_ref, nw_ref, wo_ref, l1g_ref, l1b_ref,
                 wu_ref, cw_ref, cb_ref, wd_ref, l2g_ref, l2b_ref, o_ref, act_scr,
                 *, alpha, heads, row_w, tb, tf, dff, ctx_row):
    s5_y = jax.nn.gelu(y_ref[0] + u_ref[0] * d_ref[...])
    s5_out = s5_y * jax.nn.sigmoid(_dot(s5_y.astype(BF16), wg_ref[...]) + bg_ref[...])
    o = of_ref[0].astype(F32) + ob_ref[0].astype(F32)
    gate = g_ref[0].astype(F32)
    nw = nw_ref[...]
    mixed = [s5_out.astype(BF16)]
    for hd in range(heads):
        sl = slice(hd * LANES, (hd + 1) * LANES)
        oh = o[:, sl]
        ms = jnp.mean(oh * oh, axis=-1, keepdims=True)
        mixed.append((oh * lax.rsqrt(ms + RMS_EPS) * nw * gate[:, sl]).astype(BF16))
    proj = _dot(jnp.concatenate(mixed, axis=1), wo_ref[...])
    x = _layer_norm(alpha * x_ref[0] + _mod_row(g1_ref, ctx_row) * proj, l1g_ref[...], l1b_ref[...])

    h = (x * (1.0 + _mod_row(sc_ref, ctx_row)) + _mod_row(sh_ref, ctx_row)).astype(BF16)
    pos = lax.broadcasted_iota(jnp.int32, (tb, 1), 0) % row_w
    has_prev = pos != 0
    has_next = pos != row_w - 1

    def conv(up, c0):
        prev = jnp.where(has_prev, pltpu.roll(up, 1, 0), 0.0)
        nxt = jnp.where(has_next, pltpu.roll(up, tb - 1, 0), 0.0)
        cols = slice(c0, c0 + tf)
        return prev * cw_ref[0:1, cols] + up * cw_ref[1:2, cols] + nxt * cw_ref[2:3, cols] + cb_ref[:, cols]

    for t in range(dff // tf):
        a = conv(_dot(h, wu_ref[:, t * tf:(t + 1) * tf]), t * tf)
        g = conv(_dot(h, wu_ref[:, dff + t * tf:dff + (t + 1) * tf]), dff + t * tf)
        act_scr[:, t * tf:(t + 1) * tf] = (_silu(a) * g).astype(BF16)
    z = alpha * x + _mod_row(g2_ref, ctx_row) * _dot(act_scr[...], wd_ref[...])
    o_ref[0] = _layer_norm(z, l2g_ref[...], l2b_ref[...])


def _layer_tail(x, u, y5, o_f, o_b, g, mods, post_consts, ffn_consts, l, alpha, row_w, ctx_row):
    bsz, n, d = x.shape
    w5 = u.shape[-1]
    hgw = g.shape[-1]
    dff = ffn_consts[3].shape[1]
    tf = FFN_TILE
    tb = min(n, TAIL_TOKENS)
    tok = lambda c: pl.BlockSpec((1, tb, c), lambda b, i: (b, i, 0))
    return pl.pallas_call(
        functools.partial(_tail_kernel, alpha=alpha, heads=hgw // LANES, row_w=row_w, tb=tb, tf=tf, dff=dff,
                          ctx_row=ctx_row),
        grid=(bsz, n // tb),
        in_specs=[tok(d), tok(w5), tok(w5), tok(hgw), tok(hgw), tok(hgw)]
        + [_mod_spec(mods, l, i, d) for i in (2, 3, 4, 5)]
        + [_layer_spec(a, l, True) for a in post_consts + ffn_consts],
        out_specs=tok(d),
        out_shape=jax.ShapeDtypeStruct((bsz, n, d), F32),
        scratch_shapes=[pltpu.VMEM((tb, dff), BF16)],
        compiler_params=_cparams("arbitrary", "arbitrary"),
        name="layer_tail",
    )(x, u, y5, o_f, o_b, g, mods, mods, mods, mods, *post_consts, *ffn_consts)


def _token_mixer(h_in, mods, w_in, filters, lb_all, l, init, with_out, w5, hg, ctx_row):
    u, fraw, v, q, g = _inproj(h_in, mods, w_in, l, w5, hg, ctx_row)
    s5_init, hg_init = init
    y5, s5_fin = _s5_mixer(u, filters, l, s5_init, with_out)
    o_f, o_b, hg_fin_f, hg_fin_b = _gla(fraw, v, q, lb_all, l, hg_init[0], hg_init[1])
    return (u, y5, o_f, o_b, g), (s5_fin, (hg_fin_f, hg_fin_b))


def kernel(x, c, ctx, c_ctx, w_mod, b_mod, w_in, s5_lam_re, s5_lam_im, s5_log_dt, s5_b_re, s5_b_im,
           s5_c_re, s5_c_im, s5_d, w_glu, b_glu, hg_lb, hg_norm_w, w_out, ln1_g, ln1_b,
           w_up, conv_w, conv_b, w_down, ln2_g, ln2_b):
    depth = w_mod.shape[0]
    bsz, n, d = x.shape
    n_ctx = ctx.shape[1]
    w5 = s5_d.shape[-1]
    hg = hg_lb.shape[-1]
    heads = hg // LANES
    npair = s5_lam_re.shape[2] // 2
    alpha = (2 * depth) ** 0.25

    lb_all = jnp.cumsum(jax.nn.softmax(hg_lb.astype(F32), axis=0), axis=0)
    lb_all = lb_all - lb_all[:1]

    rb = -(-(bsz + 1) // SUBLANES) * SUBLANES
    rows = jnp.concatenate([c, c_ctx[None], jnp.zeros((rb - bsz - 1, d), F32)], axis=0)
    mods = _mod_vectors(rows, w_mod, b_mod)

    vec = lambda a: a.reshape(depth, 1, a.shape[-1])
    w_in_b = w_in.astype(BF16)
    post_consts = [vec(s5_d), w_glu.astype(BF16), vec(b_glu), vec(hg_norm_w), w_out.astype(BF16), vec(ln1_g), vec(ln1_b)]
    ffn_consts = [w_up.astype(BF16), conv_w, vec(conv_b), w_down.astype(BF16), vec(ln2_g), vec(ln2_b)]

    bbp, ccp, pwp, a_pow = jax.vmap(_s5_filter_inputs)(
        s5_lam_re, s5_lam_im, s5_log_dt, s5_b_re, s5_b_im, s5_c_re, s5_c_im)
    filters = tuple(_s5_filters(bbp, ccp, pwp)) + (a_pow,)

    zero_init = (jnp.zeros((4, bsz, npair, LANES), F32),
                 (jnp.zeros((bsz, heads, LANES, LANES), F32),) * 2)

    for l in range(depth):
        last = l == depth - 1
        c_parts, ctx_states = _token_mixer(ctx, mods, w_in_b, filters, lb_all, l, zero_init, not last, w5, hg, bsz)
        x_parts, _ = _token_mixer(x, mods, w_in_b, filters, lb_all, l, ctx_states, True, w5, hg, None)
        x = _layer_tail(x, *x_parts, mods, post_consts, ffn_consts, l, alpha, GRID_W, None)
        if not last:
            ctx = _layer_tail(ctx, *c_parts, mods, post_consts, ffn_consts, l, alpha, n_ctx, bsz)
    return x
```

```python
import functools
import math

import jax
import jax.numpy as jnp
from jax import lax
from jax.experimental import pallas as pl
from jax.experimental.pallas import tpu as pltpu

F32 = jnp.float32
BF16 = jnp.bfloat16

GRID_W = 64
HG_CHUNK = 32
S5_CHUNK = 16
LN_EPS = 1e-5
RMS_EPS = 1e-6
LANES = 128
SUBLANES = 8
GRAN = LANES // SUBLANES
VMEM_LIMIT = 56 * 1024 * 1024
INPROJ_TOKENS = 1024
TAIL_TOKENS = 512
FFN_TILE = 256
S5_ROWS = 128
HG_BLOCK = 256
HG_BLOCKS_PER_STEP = 8


def _cparams(*sem):
    return pltpu.CompilerParams(dimension_semantics=sem, vmem_limit_bytes=VMEM_LIMIT)


def _silu(x):
    return x * jax.nn.sigmoid(x)


def _dot(a, b):
    return jnp.dot(a, b, preferred_element_type=F32)


def _dot_t0(a, b):
    return lax.dot_general(a, b, (((0,), (0,)), ((), ())), preferred_element_type=F32)


def _dot_t1(a, b):
    return lax.dot_general(a, b, (((1,), (1,)), ((), ())), preferred_element_type=F32)


def _split(x, parts):
    out = []
    for _ in range(parts - 1):
        piece = x.astype(BF16)
        out.append(piece)
        x = x - piece.astype(F32)
    out.append(x.astype(BF16))
    return out


def _hp_dot_t1(a, b):
    ah, al = _split(a, 2)
    bh, bl = _split(b, 2)
    return _dot_t1(ah, bh) + _dot_t1(ah, bl) + _dot_t1(al, bh)


def _layer_norm(z, g, b):
    mu = jnp.mean(z, axis=-1, keepdims=True)
    zc = z - mu
    var = jnp.mean(zc * zc, axis=-1, keepdims=True)
    return zc * lax.rsqrt(var + LN_EPS) * g + b


def _layer_spec(a, l, single=False):
    mode = dict(pipeline_mode=pl.Buffered(1)) if single else {}
    return pl.BlockSpec((None,) + a.shape[1:], lambda *_: (l,) + (0,) * (a.ndim - 1), **mode)


def _mod_spec(mods, l, chunk, d):
    return pl.BlockSpec((None, mods.shape[1], d), lambda *_: (l, 0, chunk))


def _mod_row(m_ref, ctx_row):
    row = pl.program_id(0) if ctx_row is None else ctx_row
    return m_ref[pl.ds(row, 1), :]


def _mod_kernel(c_ref, w_ref, b_ref, o_ref):
    s = _silu(c_ref[...]).astype(BF16)
    o_ref[0] = _dot(s, w_ref[0].astype(BF16)) + b_ref[0]


def _mod_vectors(rows, w_mod, b_mod):
    depth, d, d6 = w_mod.shape
    rb = rows.shape[0]
    tn = 1536 if d6 % 1536 == 0 else d6
    return pl.pallas_call(
        _mod_kernel,
        grid=(depth, d6 // tn),
        in_specs=[
            pl.BlockSpec((rb, d), lambda l, j: (0, 0)),
            pl.BlockSpec((1, d, tn), lambda l, j: (l, 0, j)),
            pl.BlockSpec((1, 1, tn), lambda l, j: (l, 0, j)),
        ],
        out_specs=pl.BlockSpec((1, rb, tn), lambda l, j: (l, 0, j)),
        out_shape=jax.ShapeDtypeStruct((depth, rb, d6), F32),
        compiler_params=_cparams("arbitrary", "arbitrary"),
        name="mod_vectors",
    )(rows, w_mod, b_mod.reshape(depth, 1, d6))


def _inproj_kernel(x_ref, sh_ref, sc_ref, w_ref, u_ref, f_ref, v_ref, q_ref, g_ref, *, w5, hg, ctx_row):
    h = (x_ref[0] * (1.0 + _mod_row(sc_ref, ctx_row)) + _mod_row(sh_ref, ctx_row)).astype(BF16)
    o = 0
    u_ref[0] = _dot(h, w_ref[:, o:o + w5]); o += w5
    f_ref[0] = _dot(h, w_ref[:, o:o + 2 * hg]); o += 2 * hg
    v_ref[0] = _dot(h, w_ref[:, o:o + hg]).astype(BF16); o += hg
    q_ref[0] = _silu(_dot(h, w_ref[:, o:o + hg])).astype(BF16); o += hg
    g_ref[0] = _silu(_dot(h, w_ref[:, o:o + hg])).astype(BF16)


def _inproj(x, mods, w_in, l, w5, hg, ctx_row):
    bsz, n, d = x.shape
    tb = min(n, INPROJ_TOKENS)
    tok = lambda c: pl.BlockSpec((1, tb, c), lambda b, j: (b, j, 0))
    return pl.pallas_call(
        functools.partial(_inproj_kernel, w5=w5, hg=hg, ctx_row=ctx_row),
        grid=(bsz, n // tb),
        in_specs=[tok(d), _mod_spec(mods, l, 0, d), _mod_spec(mods, l, 1, d), _layer_spec(w_in, l, True)],
        out_specs=[tok(w5), tok(2 * hg), tok(hg), tok(hg), tok(hg)],
        out_shape=[jax.ShapeDtypeStruct((bsz, n, c), t)
                   for c, t in ((w5, F32), (2 * hg, F32), (hg, BF16), (hg, BF16), (hg, BF16))],
        compiler_params=_cparams("arbitrary", "arbitrary"),
        name="inproj",
    )(x, mods, mods, w_in)


def _slot_masks(shape):
    lane = lax.broadcasted_iota(jnp.int32, shape, len(shape) - 1)
    return [(lane // GRAN) == s for s in range(SUBLANES)]


def _row_copies(hbm_ref, buf_ref, sem_ref, step, slot, tr, to_hbm):
    out = []
    for l in range(S5_CHUNK):
        hbm = hbm_ref.at[pl.ds(step * tr, tr), l, :]
        vm = buf_ref.at[slot, l]
        out.append(pltpu.make_async_copy(vm, hbm, sem_ref.at[slot, l]) if to_hbm
                   else pltpu.make_async_copy(hbm, vm, sem_ref.at[slot, l]))
    return out


def _s5_in_kernel(u_hbm, t_ref, w_ref, yi_ref, s_ref, buf, sem, uf_scr, s_scr, *, tr, h, nsteps):
    i = pl.program_id(0)
    slot = i % 2
    npair = uf_scr.shape[0]
    gw = S5_CHUNK * h
    gpb = LANES // h

    @pl.when(i == 0)
    def _():
        for cp in _row_copies(u_hbm, buf, sem, 0, 0, tr, False):
            cp.start()

    for cp in _row_copies(u_hbm, buf, sem, i, slot, tr, False):
        cp.wait()

    @pl.when(i + 1 < nsteps)
    def _():
        for cp in _row_copies(u_hbm, buf, sem, i + 1, 1 - slot, tr, False):
            cp.start()

    masks = _slot_masks((tr, LANES))
    for blk in range(buf.shape[-1] // LANES):
        for t in range(S5_CHUNK // SUBLANES):
            rot = []
            for l8 in range(SUBLANES):
                x = buf[slot, t * SUBLANES + l8, :, blk * LANES:(blk + 1) * LANES]
                rot.append(pltpu.roll(x, l8 * h, 1) if l8 else x)
            for gi in range(gpb):
                g = blk * gpb + gi
                dest = rot[0]
                for l8 in range(1, SUBLANES):
                    dest = jnp.where(masks[(gi + l8) % SUBLANES], rot[l8], dest)
                c0 = (g % 2) * gw + t * LANES
                uf_scr[g // 2, :, c0:c0 + LANES] = dest
    for p in range(npair):
        ub = uf_scr[p].astype(BF16)
        s = _dot(ub, w_ref[p])
        for k in range(4):
            s_scr[k, pl.ds(p, tr, stride=npair), :] = s[:, k * LANES:(k + 1) * LANES]
        yi_ref[p] = jnp.concatenate([_dot(ub[:, a * gw:(a + 1) * gw], t_ref[p, a]) for a in range(2)],
                                    axis=1).astype(BF16)
    s_ref[...] = s_scr[...].astype(BF16)


def _s5_in(u3, toep, w_in_pair, l, h, tr):
    r, ell, w5 = u3.shape
    _, npair, _, gw, _ = toep.shape
    nsteps = r // tr
    return pl.pallas_call(
        functools.partial(_s5_in_kernel, tr=tr, h=h, nsteps=nsteps),
        grid=(nsteps,),
        in_specs=[pl.BlockSpec(memory_space=pl.ANY), _layer_spec(toep, l, True), _layer_spec(w_in_pair, l, True)],
        out_specs=[
            pl.BlockSpec((npair, tr, 2 * gw), lambda i: (0, i, 0)),
            pl.BlockSpec((4, tr * npair, LANES), lambda i: (0, i, 0)),
        ],
        out_shape=[
            jax.ShapeDtypeStruct((npair, r, 2 * gw), BF16),
            jax.ShapeDtypeStruct((4, r * npair, LANES), BF16),
        ],
        scratch_shapes=[pltpu.VMEM((2, ell, tr, w5), F32), pltpu.SemaphoreType.DMA((2, ell)),
                        pltpu.VMEM((npair, tr, 2 * gw), F32), pltpu.VMEM((4, tr * npair, LANES), F32)],
        compiler_params=_cparams("arbitrary"),
        name="s5_in",
    )(u3, toep, w_in_pair)


def _s5_scan_kernel(s_ref, s0_ref, a_ref, xp_ref, xf_ref, *, nc):
    far, fai, bar, bai = a_ref[0], a_ref[1], a_ref[2], a_ref[3]

    def body(i, carry):
        fr, fi, br, bi = carry
        ib = nc - 1 - i
        xp_ref[0, 0, i] = fr.astype(BF16)
        xp_ref[1, 0, i] = fi.astype(BF16)
        xp_ref[2, 0, ib] = br.astype(BF16)
        xp_ref[3, 0, ib] = bi.astype(BF16)
        nfr = far * fr - fai * fi + s_ref[0, 0, i].astype(F32)
        nfi = far * fi + fai * fr + s_ref[1, 0, i].astype(F32)
        nbr = bar * br - bai * bi + s_ref[2, 0, ib].astype(F32)
        nbi = bar * bi + bai * br + s_ref[3, 0, ib].astype(F32)
        return nfr, nfi, nbr, nbi

    init = (s0_ref[0, 0], s0_ref[1, 0], s0_ref[2, 0], s0_ref[3, 0])
    fr, fi, br, bi = lax.fori_loop(0, nc, body, init, unroll=4)
    xf_ref[0, 0] = fr
    xf_ref[1, 0] = fi
    xf_ref[2, 0] = br
    xf_ref[3, 0] = bi


def _s5_scan(s_loc, s0, a_pow, l):
    _, bsz, nc, npair, _ = s_loc.shape
    return pl.pallas_call(
        functools.partial(_s5_scan_kernel, nc=nc),
        grid=(bsz,),
        in_specs=[
            pl.BlockSpec((4, 1, nc, npair, LANES), lambda b: (0, b, 0, 0, 0)),
            pl.BlockSpec((4, 1, npair, LANES), lambda b: (0, b, 0, 0)),
            _layer_spec(a_pow, l),
        ],
        out_specs=[
            pl.BlockSpec((4, 1, nc, npair, LANES), lambda b: (0, b, 0, 0, 0)),
            pl.BlockSpec((4, 1, npair, LANES), lambda b: (0, b, 0, 0)),
        ],
        out_shape=[
            jax.ShapeDtypeStruct(s_loc.shape, BF16),
            jax.ShapeDtypeStruct((4, bsz, npair, LANES), F32),
        ],
        compiler_params=_cparams("arbitrary"),
        name="s5_scan",
    )(s_loc, s0, a_pow)


def _s5_out_kernel(yi_ref, xp_ref, w_ref, y_hbm, yf_scr, buf, sem, xp_scr, *, tr, h, nsteps):
    i = pl.program_id(0)
    npair = yi_ref.shape[0]
    gw = S5_CHUNK * h
    gpb = LANES // h
    xp_scr[...] = xp_ref[...].astype(F32)
    for p in range(npair):
        xcat = jnp.concatenate([xp_scr[k, pl.ds(p, tr, stride=npair), :] for k in range(4)], axis=1)
        yf_scr[p] = yi_ref[p].astype(F32) + _dot(xcat.astype(BF16), w_ref[p])

    @pl.when(i > 0)
    def _():
        for cp in _row_copies(y_hbm, buf, sem, i - 1, 0, tr, True):
            cp.wait()

    masks = _slot_masks((tr, LANES))
    for blk in range(npair * 2 * h // LANES):
        for t in range(S5_CHUNK // SUBLANES):
            src = []
            for gi in range(gpb):
                g = blk * gpb + gi
                c0 = (g % 2) * gw + t * LANES
                src.append(yf_scr[g // 2, :, c0:c0 + LANES])
            for l8 in range(SUBLANES):
                m = src[0]
                for gi in range(1, gpb):
                    m = jnp.where(masks[(gi + l8) % SUBLANES], src[gi], m)
                buf[0, t * SUBLANES + l8, :, blk * LANES:(blk + 1) * LANES] = (
                    pltpu.roll(m, LANES - l8 * h, 1) if l8 else m)
    copies = _row_copies(y_hbm, buf, sem, i, 0, tr, True)
    for cp in copies:
        cp.start()

    @pl.when(i == nsteps - 1)
    def _():
        for cp in copies:
            cp.wait()


def _s5_out(y_intra, x_prev, w_out_pair, l, h, tr):
    npair, r, width = y_intra.shape
    w5 = npair * 2 * h
    nsteps = r // tr
    return pl.pallas_call(
        functools.partial(_s5_out_kernel, tr=tr, h=h, nsteps=nsteps),
        grid=(nsteps,),
        in_specs=[
            pl.BlockSpec((npair, tr, width), lambda i: (0, i, 0)),
            pl.BlockSpec((4, tr * npair, LANES), lambda i: (0, i, 0)),
            _layer_spec(w_out_pair, l, True),
        ],
        out_specs=pl.BlockSpec(memory_space=pl.ANY),
        out_shape=jax.ShapeDtypeStruct((r, S5_CHUNK, w5), F32),
        scratch_shapes=[pltpu.VMEM((npair, tr, width), F32), pltpu.VMEM((1, S5_CHUNK, tr, w5), F32),
                        pltpu.SemaphoreType.DMA((1, S5_CHUNK)), pltpu.VMEM((4, tr * npair, LANES), F32)],
        compiler_params=_cparams("arbitrary"),
        name="s5_out",
    )(y_intra, x_prev, w_out_pair)


def _filters_kernel(bb_ref, cc_ref, pw_ref, t_ref, win_ref, wout_ref, ca_scr, t_scr, *, h, ppb):
    ell = S5_CHUNK
    gw = ell * h
    lane = lax.broadcasted_iota(jnp.int32, (h, gw), 1)
    for pi in range(ppb):
        for a in range(2):
            gm = 2 * pi + a
            lag = []
            for d in range(2):
                cr, ci = cc_ref[0, d, pi, a], cc_ref[1, d, pi, a]
                for j in range(ell):
                    tau = j if d == 0 else ell - 1 - j
                    pr, pim = pw_ref[0, d, pi, tau:tau + 1, :], pw_ref[1, d, pi, tau:tau + 1, :]
                    ca_scr[0, j * h:(j + 1) * h, :] = cr * pr - ci * pim
                    ca_scr[1, j * h:(j + 1) * h, :] = cr * pim + ci * pr
                lag.append(_hp_dot_t1(bb_ref[0, d, pi, a], ca_scr[0]) - _hp_dot_t1(bb_ref[1, d, pi, a], ca_scr[1]))
            for lp in range(ell):
                sf, sb = lp * h, (ell - 1 - lp) * h
                fwd = lag[0] if sf == 0 else jnp.where(lane >= sf, pltpu.roll(lag[0], sf, 1), 0.0)
                bwd = lag[1] if sb == 0 else jnp.where(lane < gw - sb, pltpu.roll(lag[1], gw - sb, 1), 0.0)
                row = fwd + bwd
                if gm:
                    row = jnp.concatenate([pltpu.roll(row[:, t * LANES:(t + 1) * LANES], gm * h, 1)
                                           for t in range(gw // LANES)], axis=1)
                r0 = (lp // SUBLANES) * LANES + ((gm + lp) % SUBLANES) * h
                t_ref[pi, a, r0:r0 + h, :] = row.astype(BF16)
            for d in range(2):
                for l in range(ell):
                    r0 = a * gw + (l // SUBLANES) * LANES + ((gm + l) % SUBLANES) * h
                    t_in = ell - 1 - l if d == 0 else l
                    t_out = l + 1 if d == 0 else ell - l
                    pr, pim = pw_ref[0, d, pi, t_in:t_in + 1, :], pw_ref[1, d, pi, t_in:t_in + 1, :]
                    br, bi = bb_ref[0, d, pi, a], bb_ref[1, d, pi, a]
                    win_ref[pi, r0:r0 + h, (2 * d) * LANES:(2 * d + 1) * LANES] = (pr * br - pim * bi).astype(BF16)
                    win_ref[pi, r0:r0 + h, (2 * d + 1) * LANES:(2 * d + 2) * LANES] = (pr * bi + pim * br).astype(BF16)
                    pr, pim = pw_ref[0, d, pi, t_out:t_out + 1, :], pw_ref[1, d, pi, t_out:t_out + 1, :]
                    cr, ci = cc_ref[0, d, pi, a], cc_ref[1, d, pi, a]
                    t_scr[r0:r0 + h, (2 * d) * LANES:(2 * d + 1) * LANES] = cr * pr - ci * pim
                    t_scr[r0:r0 + h, (2 * d + 1) * LANES:(2 * d + 2) * LANES] = -(cr * pim + ci * pr)
        wout_ref[pi] = t_scr[...].T.astype(BF16)


def _s5_filters(bbp, ccp, pwp):
    depth, _, _, npair, _, h, _ = bbp.shape
    ell1 = pwp.shape[-2]
    ppb = LANES // h // 2
    gw = S5_CHUNK * h
    width = 2 * gw
    return pl.pallas_call(
        functools.partial(_filters_kernel, h=h, ppb=ppb),
        grid=(depth, npair // ppb),
        in_specs=[
            pl.BlockSpec((None, 2, 2, ppb, 2, h, LANES), lambda l, b: (l, 0, 0, b, 0, 0, 0)),
            pl.BlockSpec((None, 2, 2, ppb, 2, h, LANES), lambda l, b: (l, 0, 0, b, 0, 0, 0)),
            pl.BlockSpec((None, 2, 2, ppb, ell1, LANES), lambda l, b: (l, 0, 0, b, 0, 0)),
        ],
        out_specs=[pl.BlockSpec((None, ppb, 2, gw, gw), lambda l, b: (l, b, 0, 0, 0)),
                   pl.BlockSpec((None, ppb, width, 4 * LANES), lambda l, b: (l, b, 0, 0)),
                   pl.BlockSpec((None, ppb, 4 * LANES, width), lambda l, b: (l, b, 0, 0))],
        out_shape=[jax.ShapeDtypeStruct((depth, npair, 2, gw, gw), BF16),
                   jax.ShapeDtypeStruct((depth, npair, width, 4 * LANES), BF16),
                   jax.ShapeDtypeStruct((depth, npair, 4 * LANES, width), BF16)],
        scratch_shapes=[pltpu.VMEM((2, gw, LANES), F32), pltpu.VMEM((width, 4 * LANES), F32)],
        compiler_params=_cparams("arbitrary", "arbitrary"),
        name="s5_filters",
    )(bbp, ccp, pwp)


def _pair_halves(w):
    *lead, g, h, p = w.shape
    w = w.reshape(*lead, g // 2, 2, h, p)
    z = jnp.zeros_like(w[..., 0, :, :])
    return jnp.stack([jnp.concatenate([w[..., 0, :, :], z], axis=-1),
                      jnp.concatenate([z, w[..., 1, :, :]], axis=-1)], axis=-3)


def _s5_filter_inputs(lam_re, lam_im, log_dt, b_re, b_im, c_re, c_im):
    ndir, g, p = lam_re.shape
    h = b_re.shape[-1]
    ell = S5_CHUNK
    npair = g // 2
    lr, li = lam_re.astype(F32), lam_im.astype(F32)
    dt = jnp.exp(log_dt.astype(F32))[..., None]
    mag, ang = jnp.exp(lr * dt), li * dt
    abar_re, abar_im = mag * jnp.cos(ang), mag * jnp.sin(ang)
    den = lr * lr + li * li
    nr, ni = abar_re - 1.0, abar_im
    coef_re = ((nr * lr + ni * li) / den)[..., None]
    coef_im = ((ni * lr - nr * li) / den)[..., None]
    bb_re = coef_re * b_re - coef_im * b_im
    bb_im = coef_re * b_im + coef_im * b_re
    tau = jnp.arange(ell + 1, dtype=F32)[:, None, None, None]
    pmag, pang = jnp.exp(tau * (lr * dt)), tau * (li * dt)
    pw_re, pw_im = pmag * jnp.cos(pang), pmag * jnp.sin(pang)
    bbp = jnp.stack([_pair_halves(b.transpose(0, 1, 3, 2)) for b in (bb_re, bb_im)])
    ccp = jnp.stack([_pair_halves(c) for c in (c_re.astype(F32), c_im.astype(F32))])
    pwp = jnp.stack([pw.reshape(ell + 1, ndir, npair, 2 * p).transpose(1, 2, 0, 3) for pw in (pw_re, pw_im)])
    a_pow = jnp.stack([pw[ell, d].reshape(npair, 2 * p) for d in range(2) for pw in (pw_re, pw_im)])
    return bbp, ccp, pwp, a_pow


def _s5_mixer(u, filters, l, s0, with_out):
    toep, w_in_pair, w_out_pair, a_pow = filters
    bsz, n, w5 = u.shape
    npair = toep.shape[1]
    h = w5 // (2 * npair)
    nc = n // S5_CHUNK
    tr = min(bsz * nc, S5_ROWS)
    y_intra, s_loc = _s5_in(u.reshape(bsz * nc, S5_CHUNK, w5), toep, w_in_pair, l, h, tr)
    x_prev, x_fin = _s5_scan(s_loc.reshape(4, bsz, nc, npair, LANES), s0, a_pow, l)
    if not with_out:
        return None, x_fin
    y = _s5_out(y_intra, x_prev.reshape(4, bsz * nc * npair, LANES), w_out_pair, l, h, tr)
    return y.reshape(bsz, n, w5), x_fin


def _gla_direction(fr, q, v, lb, s_ref, o_ref, row0, *, blk, reverse, heads):
    n = blk // HG_CHUNK
    width = heads * LANES
    f = lb + (1.0 - lb) * jax.nn.sigmoid(fr)
    k = 1.0 - f
    hi, lo = _split(jnp.log(f), 2)
    row = lax.broadcasted_iota(jnp.int32, (blk, blk), 0)
    col = lax.broadcasted_iota(jnp.int32, (blk, blk), 1)
    tri = jnp.where((col >= row) if reverse else (col <= row), 1.0, 0.0).astype(BF16)
    c = _dot(tri, hi) + _dot(tri, lo)

    def rows(i):
        return slice(blk - (i + 1) * HG_CHUNK, blk - i * HG_CHUNK) if reverse else slice(i * HG_CHUNK, (i + 1) * HG_CHUNK)

    def mem_order(chunks):
        return sorted(chunks, reverse=reverse)

    r = [jnp.zeros((1, width), F32)]
    for i in range(n):
        edge = rows(i).start if reverse else rows(i).stop - 1
        r.append(c[edge:edge + 1])

    def per_chunk(vals):
        return jnp.concatenate([jnp.broadcast_to(vals[i], (HG_CHUNK, width)) for i in mem_order(range(n))], axis=0)

    qs = q * jnp.exp(c - per_chunk(r[:n]))
    kdl = k * jnp.exp(per_chunk(r[1:]) - c)
    qi = (qs * per_chunk([jnp.exp(r[i]) for i in range(n)])).astype(BF16)
    kd = (kdl * per_chunk([jnp.exp(r[n] - r[i + 1]) for i in range(n)])).astype(BF16)
    kdlb = kdl.astype(BF16)
    hop = {(i, j): jnp.exp(r[i] - r[j + 1]) for j in range(n) for i in range(j, n)}
    dparts = _split(jnp.concatenate([r[n], jnp.zeros((SUBLANES - 1, width), F32)], axis=0), 3)
    ones8 = jnp.ones((SUBLANES, LANES), BF16)
    lr = lax.broadcasted_iota(jnp.int32, (HG_CHUNK, HG_CHUNK), 0)
    lc = lax.broadcasted_iota(jnp.int32, (HG_CHUNK, HG_CHUNK), 1)
    causal = (lc >= lr) if reverse else (lc <= lr)

    for hd in range(heads):
        sl = slice(hd * LANES, (hd + 1) * LANES)
        state = s_ref[hd]
        dcol = sum(_dot_t0(p[:, sl], ones8) for p in dparts)
        o_inter = _dot(qi[:, sl], state.astype(BF16))
        s_ref[hd] = jnp.exp(dcol) * state + _dot_t0(kd[:, sl], v[:, sl])
        acc = {i: o_inter[rows(i)] for i in range(n)}
        for j in range(n):
            queries = mem_order(range(j, n))
            lhs = jnp.concatenate([qs[rows(i), sl] * hop[i, j][:, sl] for i in queries], axis=0).astype(BF16)
            att = _dot_t1(lhs, kdlb[rows(j), sl])
            pieces = [att[a * HG_CHUNK:(a + 1) * HG_CHUNK] for a in range(len(queries))]
            dpos = queries.index(j)
            pieces[dpos] = jnp.where(causal, pieces[dpos], 0.0)
            o_j = _dot(jnp.concatenate(pieces, axis=0).astype(BF16), v[rows(j), sl])
            for a, i in enumerate(queries):
                acc[i] = acc[i] + o_j[a * HG_CHUNK:(a + 1) * HG_CHUNK]
        o_ref[0, row0:row0 + blk, sl] = jnp.concatenate([acc[i] for i in mem_order(range(n))], axis=0).astype(BF16)


def _gla_kernel(ff_ref, fb_ref, vf_ref, vb_ref, qf_ref, qb_ref, lb_ref, s0f_ref, s0b_ref,
                of_ref, ob_ref, sff_ref, sfb_ref, s_scr, *, blk, heads, nsub):
    j = pl.program_id(1)
    last = pl.num_programs(1) - 1

    @pl.when(j == 0)
    def _():
        s_scr[0] = s0f_ref[0]
        s_scr[1] = s0b_ref[0]

    for sb in range(nsub):
        rf, rb = sb * blk, (nsub - 1 - sb) * blk
        _gla_direction(ff_ref[0, rf:rf + blk], qf_ref[0, rf:rf + blk].astype(F32), vf_ref[0, rf:rf + blk], lb_ref[0:1],
                       s_scr.at[0], of_ref, rf, blk=blk, reverse=False, heads=heads)
        _gla_direction(fb_ref[0, rb:rb + blk], qb_ref[0, rb:rb + blk].astype(F32), vb_ref[0, rb:rb + blk], lb_ref[1:2],
                       s_scr.at[1], ob_ref, rb, blk=blk, reverse=True, heads=heads)

    @pl.when(j == last)
    def _():
        sff_ref[0] = s_scr[0]
        sfb_ref[0] = s_scr[1]


def _gla(fraw, v, q, lb_all, l, s0f, s0b):
    bsz, n, hgw = v.shape
    heads = hgw // LANES
    blk = min(n, HG_BLOCK)
    nsub = math.gcd(n // blk, HG_BLOCKS_PER_STEP)
    step = blk * nsub
    nstep = n // step
    fwd = lambda c: pl.BlockSpec((1, step, hgw), lambda b, j: (b, j, c))
    bwd = lambda c: pl.BlockSpec((1, step, hgw), lambda b, j: (b, nstep - 1 - j, c))
    st = pl.BlockSpec((1, heads, LANES, LANES), lambda b, j: (b, 0, 0, 0))
    return pl.pallas_call(
        functools.partial(_gla_kernel, blk=blk, heads=heads, nsub=nsub),
        grid=(bsz, nstep),
        in_specs=[fwd(0), bwd(1), fwd(0), bwd(0), fwd(0), bwd(0), _layer_spec(lb_all, l), st, st],
        out_specs=[fwd(0), bwd(0), st, st],
        out_shape=[jax.ShapeDtypeStruct((bsz, n, hgw), BF16)] * 2
        + [jax.ShapeDtypeStruct((bsz, heads, LANES, LANES), F32)] * 2,
        scratch_shapes=[pltpu.VMEM((2, heads, LANES, LANES), F32)],
        compiler_params=_cparams("arbitrary", "arbitrary"),
        name="gla",
    )(fraw, fraw, v, v, q, q, lb_all, s0f, s0b)


def _tail_kernel(x_ref, u_ref, y_ref, of_ref, ob_ref, g_ref, g1_ref, sh_ref, sc_ref, g2_ref,
                 d_ref, wg_ref, bg_ref, nw_ref, wo_ref, l1g_ref, l1b_ref,
                 wu_hbm, cw_ref, cb_ref, wd_hbm, l2g_ref, l2b_ref, o_ref, act_scr, wu_ref, wd_ref, wsem,
                 *, alpha, heads, row_w, tb, tf, dff, ctx_row, layer):
    first = jnp.logical_and(pl.program_id(0) == 0, pl.program_id(1) == 0)
    fetch = (pltpu.make_async_copy(wu_hbm.at[layer], wu_ref, wsem.at[0]),
             pltpu.make_async_copy(wd_hbm.at[layer], wd_ref, wsem.at[1]))

    @pl.when(first)
    def _():
        for cp in fetch:
            cp.start()

    s5_y = jax.nn.gelu(y_ref[0] + u_ref[0] * d_ref[...])
    s5_out = s5_y * jax.nn.sigmoid(_dot(s5_y.astype(BF16), wg_ref[...]) + bg_ref[...])
    o = of_ref[0].astype(F32) + ob_ref[0].astype(F32)
    gate = g_ref[0].astype(F32)
    nw = nw_ref[...]
    mixed = [s5_out.astype(BF16)]
    for hd in range(heads):
        sl = slice(hd * LANES, (hd + 1) * LANES)
        oh = o[:, sl]
        ms = jnp.mean(oh * oh, axis=-1, keepdims=True)
        mixed.append((oh * lax.rsqrt(ms + RMS_EPS) * nw * gate[:, sl]).astype(BF16))
    proj = _dot(jnp.concatenate(mixed, axis=1), wo_ref[...])
    x = _layer_norm(alpha * x_ref[0] + _mod_row(g1_ref, ctx_row) * proj, l1g_ref[...], l1b_ref[...])

    @pl.when(first)
    def _():
        for cp in fetch:
            cp.wait()

    h = (x * (1.0 + _mod_row(sc_ref, ctx_row)) + _mod_row(sh_ref, ctx_row)).astype(BF16)
    pos = lax.broadcasted_iota(jnp.int32, (tb, 1), 0) % row_w
    has_prev = pos != 0
    has_next = pos != row_w - 1

    def conv(up, c0):
        prev = jnp.where(has_prev, pltpu.roll(up, 1, 0), 0.0)
        nxt = jnp.where(has_next, pltpu.roll(up, tb - 1, 0), 0.0)
        cols = slice(c0, c0 + tf)
        return prev * cw_ref[0:1, cols] + up * cw_ref[1:2, cols] + nxt * cw_ref[2:3, cols] + cb_ref[:, cols]

    for t in range(dff // tf):
        a = conv(_dot(h, wu_ref[:, t * tf:(t + 1) * tf]), t * tf)
        g = conv(_dot(h, wu_ref[:, dff + t * tf:dff + (t + 1) * tf]), dff + t * tf)
        act_scr[:, t * tf:(t + 1) * tf] = (_silu(a) * g).astype(BF16)
    z = alpha * x + _mod_row(g2_ref, ctx_row) * _dot(act_scr[...], wd_ref[...])
    o_ref[0] = _layer_norm(z, l2g_ref[...], l2b_ref[...])


def _layer_tail(x, u, y5, o_f, o_b, g, mods, post_consts, ffn_consts, l, alpha, row_w, ctx_row):
    bsz, n, d = x.shape
    w5 = u.shape[-1]
    hgw = g.shape[-1]
    w_up, conv_w, conv_b, w_down, ln2_g, ln2_b = ffn_consts
    dff = w_down.shape[1]
    tf = FFN_TILE
    tb = min(n, TAIL_TOKENS)
    tok = lambda c: pl.BlockSpec((1, tb, c), lambda b, i: (b, i, 0))
    res = lambda a: _layer_spec(a, l, True)
    hbm = pl.BlockSpec(memory_space=pl.ANY)
    return pl.pallas_call(
        functools.partial(_tail_kernel, alpha=alpha, heads=hgw // LANES, row_w=row_w, tb=tb, tf=tf, dff=dff,
                          ctx_row=ctx_row, layer=l),
        grid=(bsz, n // tb),
        in_specs=[tok(d), tok(w5), tok(w5), tok(hgw), tok(hgw), tok(hgw)]
        + [_mod_spec(mods, l, i, d) for i in (2, 3, 4, 5)]
        + [res(a) for a in post_consts] + [hbm, res(conv_w), res(conv_b), hbm, res(ln2_g), res(ln2_b)],
        out_specs=tok(d),
        out_shape=jax.ShapeDtypeStruct((bsz, n, d), F32),
        scratch_shapes=[pltpu.VMEM((tb, dff), BF16), pltpu.VMEM(w_up.shape[1:], BF16), pltpu.VMEM(w_down.shape[1:], BF16),
                        pltpu.SemaphoreType.DMA((2,))],
        compiler_params=_cparams("arbitrary", "arbitrary"),
        name="layer_tail",
    )(x, u, y5, o_f, o_b, g, mods, mods, mods, mods, *post_consts, *ffn_consts)


def _token_mixer(h_in, mods, w_in, filters, lb_all, l, init, with_out, w5, hg, ctx_row):
    u, fraw, v, q, g = _inproj(h_in, mods, w_in, l, w5, hg, ctx_row)
    s5_init, hg_init = init
    y5, s5_fin = _s5_mixer(u, filters, l, s5_init, with_out)
    o_f, o_b, hg_fin_f, hg_fin_b = _gla(fraw, v, q, lb_all, l, hg_init[0], hg_init[1])
    return (u, y5, o_f, o_b, g), (s5_fin, (hg_fin_f, hg_fin_b))


def kernel(x, c, ctx, c_ctx, w_mod, b_mod, w_in, s5_lam_re, s5_lam_im, s5_log_dt, s5_b_re, s5_b_im,
           s5_c_re, s5_c_im, s5_d, w_glu, b_glu, hg_lb, hg_norm_w, w_out, ln1_g, ln1_b,
           w_up, conv_w, conv_b, w_down, ln2_g, ln2_b):
    depth = w_mod.shape[0]
    bsz, n, d = x.shape
    n_ctx = ctx.shape[1]
    w5 = s5_d.shape[-1]
    hg = hg_lb.shape[-1]
    heads = hg // LANES
    npair = s5_lam_re.shape[2] // 2
    alpha = (2 * depth) ** 0.25

    lb_all = jnp.cumsum(jax.nn.softmax(hg_lb.astype(F32), axis=0), axis=0)
    lb_all = lb_all - lb_all[:1]

    rb = -(-(bsz + 1) // SUBLANES) * SUBLANES
    rows = jnp.concatenate([c, c_ctx[None], jnp.zeros((rb - bsz - 1, d), F32)], axis=0)
    mods = _mod_vectors(rows, w_mod, b_mod)

    vec = lambda a: a.reshape(depth, 1, a.shape[-1])
    w_in_b = w_in.astype(BF16)
    post_consts = [vec(s5_d), w_glu.astype(BF16), vec(b_glu), vec(hg_norm_w), w_out.astype(BF16), vec(ln1_g), vec(ln1_b)]
    ffn_consts = [w_up.astype(BF16), conv_w, vec(conv_b), w_down.astype(BF16), vec(ln2_g), vec(ln2_b)]

    bbp, ccp, pwp, a_pow = jax.vmap(_s5_filter_inputs)(
        s5_lam_re, s5_lam_im, s5_log_dt, s5_b_re, s5_b_im, s5_c_re, s5_c_im)
    filters = tuple(_s5_filters(bbp, ccp, pwp)) + (a_pow,)

    zero_init = (jnp.zeros((4, bsz, npair, LANES), F32),
                 (jnp.zeros((bsz, heads, LANES, LANES), F32),) * 2)

    for l in range(depth):
        last = l == depth - 1
        c_parts, ctx_states = _token_mixer(ctx, mods, w_in_b, filters, lb_all, l, zero_init, not last, w5, hg, bsz)
        x_parts, _ = _token_mixer(x, mods, w_in_b, filters, lb_all, l, ctx_states, True, w5, hg, None)
        x = _layer_tail(x, *x_parts, mods, post_consts, ffn_consts, l, alpha, GRID_W, None)
        if not last:
            ctx = _layer_tail(ctx, *c_parts, mods, post_consts, ffn_consts, l, alpha, n_ctx, bsz)
    return x
```
